```python
import jax, jax.numpy as jnp
from jax import lax
import numpy as np

D_MODEL = 1024
BATCH = 4
SEQ = 4096
DEPTH = 1

GRID_W = 64
CTX_LEN = 256
MLA_HEADS = 8
MLA_NOPE = 64
MLA_ROPE = 32
MLA_V = 64
Q_LORA = 256
KV_LORA = 128
MLA_WIDTH = MLA_HEADS * MLA_V
NA_HEADS = 8
NA_DIM = 64
NA_KR = 8
NA_KC = 16
NA_WIDTH = NA_HEADS * NA_DIM

ROPE_THETA = 10000.0
Q_BLOCK = 128
EPS = 1e-6

KV_SPLITS = (KV_LORA, MLA_ROPE, NA_WIDTH, NA_WIDTH)
Q_SPLITS = (Q_LORA, MLA_WIDTH, NA_WIDTH, NA_WIDTH, D_MODEL, D_MODEL)
KV_COLS = KV_LORA + MLA_ROPE + 2 * NA_WIDTH
IN_WIDTH = KV_COLS + Q_LORA + MLA_WIDTH + 2 * NA_WIDTH + 2 * D_MODEL

kernel_name = "hybrid_mla_natten_gated_dit_layer"


def rmsnorm(x, g):
    xf = x.astype(jnp.float32)
    y = xf * lax.rsqrt(jnp.mean(xf * xf, axis=-1, keepdims=True) + EPS)
    return (y * g.astype(jnp.float32)).astype(x.dtype)


def split_cols(p, sizes):
    idx = np.cumsum(np.array(sizes))[:-1].tolist()
    return jnp.split(p, idx, axis=-1)


def heads(t, n):
    b, s, _ = t.shape
    return t.reshape(b, s, n, -1)


def rope_1d(x, pos):
    half = x.shape[-1] // 2
    freqs = ROPE_THETA ** (-jnp.arange(half, dtype=jnp.float32) / half)
    ang = pos.astype(jnp.float32)[:, None] * freqs[None, :]
    cos = jnp.cos(ang)[None, :, None, :]
    sin = jnp.sin(ang)[None, :, None, :]
    x1 = x[..., :half].astype(jnp.float32)
    x2 = x[..., half:].astype(jnp.float32)
    out = jnp.concatenate([x1 * cos - x2 * sin, x2 * cos + x1 * sin], axis=-1)
    return out.astype(x.dtype)


def rope_2d(x, rows, cols):
    d = x.shape[-1] // 2
    return jnp.concatenate([rope_1d(x[..., :d], rows), rope_1d(x[..., d:], cols)], axis=-1)


def mla_q(cq, g_cq, w_uq):
    b, s, _ = cq.shape
    q = (rmsnorm(cq, g_cq) @ w_uq).reshape(b, s, MLA_HEADS, MLA_NOPE + MLA_ROPE)
    return q[..., :MLA_NOPE], q[..., MLA_NOPE:]


def mla_kv(ckv, g_ckv, w_ukv):
    b, s, _ = ckv.shape
    kv = (rmsnorm(ckv, g_ckv) @ w_ukv).reshape(b, s, MLA_HEADS, MLA_NOPE + MLA_V)
    return kv[..., :MLA_NOPE], kv[..., MLA_NOPE:]


def block_attention(q, k, v, scale):
    b, s, h, dk = q.shape
    dv = v.shape[-1]
    nb = s // Q_BLOCK
    qb = q.reshape(b, nb, Q_BLOCK, h, dk).transpose(1, 0, 2, 3, 4)

    def one(qblk):
        sc = jnp.einsum('bqhd,bkhd->bhqk', qblk, k, preferred_element_type=jnp.float32) * scale
        p = jax.nn.softmax(sc, axis=-1).astype(v.dtype)
        return jnp.einsum('bhqk,bkhd->bqhd', p, v)

    o = lax.map(one, qb)
    return o.transpose(1, 0, 2, 3, 4).reshape(b, s, h * dv)


def neighborhood_attention(q, k, v, kc, vc, rpb):
    b, s, h, d = q.shape
    rows = s // GRID_W
    kr = min(NA_KR, rows)
    qg = q.reshape(b, rows, GRID_W, h, d)
    kg = k.reshape(b, rows, GRID_W, h, d)
    vg = v.reshape(b, rows, GRID_W, h, d)
    col = jnp.arange(GRID_W)
    c_start = jnp.clip(col - NA_KC // 2, 0, GRID_W - NA_KC)
    c_idx = c_start[:, None] + jnp.arange(NA_KC)[None, :]
    c_off = c_idx - col[:, None] + (NA_KC - 1)
    scale = d ** -0.5

    def one_row(r):
        r_start = jnp.clip(r - kr // 2, 0, rows - kr)
        k_slab = lax.dynamic_slice_in_dim(kg, r_start, kr, axis=1)
        v_slab = lax.dynamic_slice_in_dim(vg, r_start, kr, axis=1)
        k_win = k_slab[:, :, c_idx]
        v_win = v_slab[:, :, c_idx]
        qr = lax.dynamic_index_in_dim(qg, r, axis=1, keepdims=False)
        r_off = r_start + jnp.arange(kr) - r + (NA_KR - 1)
        bias = rpb[:, r_off[:, None, None], c_off[None, :, :]]
        bias = bias.transpose(0, 2, 1, 3).reshape(h, GRID_W, kr * NA_KC).astype(jnp.float32)
        s_win = jnp.einsum('bwhd,brwchd->bhwrc', qr, k_win, preferred_element_type=jnp.float32)
        s_win = s_win.reshape(b, h, GRID_W, kr * NA_KC) * scale + bias[None]
        s_ctx = jnp.einsum('bwhd,blhd->bhwl', qr, kc, preferred_element_type=jnp.float32) * scale
        p = jax.nn.softmax(jnp.concatenate([s_win, s_ctx], axis=-1), axis=-1).astype(v.dtype)
        p_win = p[..., :kr * NA_KC].reshape(b, h, GRID_W, kr, NA_KC)
        p_ctx = p[..., kr * NA_KC:]
        return (jnp.einsum('bhwrc,brwchd->bwhd', p_win, v_win)
                + jnp.einsum('bhwl,blhd->bwhd', p_ctx, vc))

    o = lax.map(one_row, jnp.arange(rows))
    return o.transpose(1, 0, 2, 3, 4).reshape(b, s, h * d)


def setup_inputs(seed: int = 0) -> dict:
    key = jax.random.key(seed)
    ks = jax.random.split(key, 20)
    n = jax.random.normal
    D = D_MODEL
    return {
        "x": n(ks[0], (BATCH, SEQ, D), jnp.float32),
        "c": n(ks[1], (BATCH, D), jnp.float32),
        "ctx": n(ks[2], (BATCH, CTX_LEN, D), jnp.float32),
        "c_ctx": n(ks[3], (D,), jnp.float32),
        "w_mod": n(ks[4], (DEPTH, D, 3 * D), jnp.float32) * (0.5 * D ** -0.5),
        "b_mod": n(ks[5], (DEPTH, 3 * D), jnp.float32) * 0.01,
        "norm_g": 1.0 + 0.05 * n(ks[6], (DEPTH, D), jnp.float32),
        "w_in": n(ks[7], (DEPTH, D, IN_WIDTH), jnp.float32) * D ** -0.5,
        "g_cq": 1.0 + 0.05 * n(ks[8], (DEPTH, Q_LORA), jnp.float32),
        "w_uq": n(ks[9], (DEPTH, Q_LORA, MLA_HEADS * (MLA_NOPE + MLA_ROPE)), jnp.float32) * Q_LORA ** -0.5,
        "g_ckv": 1.0 + 0.05 * n(ks[10], (DEPTH, KV_LORA), jnp.float32),
        "w_ukv": n(ks[11], (DEPTH, KV_LORA, MLA_HEADS * (MLA_NOPE + MLA_V)), jnp.float32) * KV_LORA ** -0.5,
        "rpb": 0.02 * n(ks[12], (DEPTH, NA_HEADS, 2 * NA_KR - 1, 2 * NA_KC - 1), jnp.float32),
        "w_oa": n(ks[13], (DEPTH, MLA_WIDTH, D), jnp.float32) * MLA_WIDTH ** -0.5,
        "w_ob": n(ks[14], (DEPTH, NA_WIDTH, D), jnp.float32) * NA_WIDTH ** -0.5,
        "w_out": n(ks[15], (DEPTH, D, D), jnp.float32) * D ** -0.5,
        "final_g": 1.0 + 0.05 * n(ks[16], (D,), jnp.float32),
    }


def reference(x, c, ctx, c_ctx, w_mod, b_mod, norm_g, w_in, g_cq, w_uq, g_ckv, w_ukv,
              rpb, w_oa, w_ob, w_out, final_g):
    b, s, _ = x.shape
    t = jnp.arange(s)
    row_pos = t // GRID_W
    col_pos = t % GRID_W
    mla_scale = (MLA_NOPE + MLA_ROPE) ** -0.5
    na_scale = NA_DIM ** -0.5
    sc = jax.nn.silu(c)
    scc = jax.nn.silu(c_ctx)
    for i in range(DEPTH):
        last = i == DEPTH - 1
        shift, scale, gate = jnp.split((sc @ w_mod[i] + b_mod[i])[:, None, :], 3, axis=-1)
        shift_c, scale_c, gate_c = jnp.split(scc @ w_mod[i] + b_mod[i], 3, axis=-1)
        h = rmsnorm(x, norm_g[i]) * (1 + scale) + shift
        hc = rmsnorm(ctx, norm_g[i]) * (1 + scale_c) + shift_c

        ckv, kr, kb, vb, cq, za, qb, zb, ga, gb = split_cols(h @ w_in[i], KV_SPLITS + Q_SPLITS)
        if last:
            c_ckv, c_kr, c_kb, c_vb = split_cols(hc @ w_in[i][:, :KV_COLS], KV_SPLITS)
        else:
            (c_ckv, c_kr, c_kb, c_vb, c_cq, c_za, c_qb, c_zb, c_ga, c_gb) = split_cols(
                hc @ w_in[i], KV_SPLITS + Q_SPLITS)

        qa_n, qa_r = mla_q(cq, g_cq[i], w_uq[i])
        qa = jnp.concatenate([qa_n, rope_2d(qa_r, row_pos, col_pos)], axis=-1)
        ka_n, va = mla_kv(ckv, g_ckv[i], w_ukv[i])
        ka_r = rope_2d(kr[:, :, None, :], row_pos, col_pos)
        ka = jnp.concatenate([ka_n, jnp.broadcast_to(ka_r, ka_n.shape[:-1] + (MLA_ROPE,))], axis=-1)
        kca_n, vca = mla_kv(c_ckv, g_ckv[i], w_ukv[i])
        kca = jnp.concatenate(
            [kca_n, jnp.broadcast_to(c_kr[:, :, None, :], kca_n.shape[:-1] + (MLA_ROPE,))], axis=-1)
        oa = block_attention(qa, jnp.concatenate([ka, kca], axis=1),
                             jnp.concatenate([va, vca], axis=1), mla_scale)

        ob = neighborhood_attention(heads(qb, NA_HEADS), heads(kb, NA_HEADS), heads(vb, NA_HEADS),
                                    heads(c_kb, NA_HEADS), heads(c_vb, NA_HEADS), rpb[i])

        ya = (oa * jax.nn.silu(za)) @ w_oa[i]
        yb = (ob * jax.nn.silu(zb)) @ w_ob[i]
        y = (jax.nn.sigmoid(ga) * ya + jax.nn.sigmoid(gb) * yb) @ w_out[i]

        if not last:
            qca_n, qca_r = mla_q(c_cq, g_cq[i], w_uq[i])
            oca = block_attention(jnp.concatenate([qca_n, qca_r], axis=-1), kca, vca, mla_scale)
            ocb = block_attention(heads(c_qb, NA_HEADS), heads(c_kb, NA_HEADS),
                                  heads(c_vb, NA_HEADS), na_scale)
            yca = (oca * jax.nn.silu(c_za)) @ w_oa[i]
            ycb = (ocb * jax.nn.silu(c_zb)) @ w_ob[i]
            yc = (jax.nn.sigmoid(c_ga) * yca + jax.nn.sigmoid(c_gb) * ycb) @ w_out[i]
            ctx = ctx + gate_c * yc

        x = x + gate * y
    return rmsnorm(x, final_g)
```

```python
import functools

import jax
import jax.numpy as jnp
import numpy as np
from jax import lax
from jax.experimental import pallas as pl
from jax.experimental.pallas import tpu as pltpu

D_MODEL = 1024
SEQ = 4096
GRID_W = 64
GRID_H = SEQ // GRID_W
CTX_LEN = 256
HEADS = 8
MLA_NOPE = 64
MLA_ROPE = 32
MLA_V = 64
Q_LORA = 256
KV_LORA = 128
NA_DIM = 64
NA_KR = 8
NA_KC = 16
ROPE_THETA = 10000.0
EPS = 1e-6

LANES = 128
HEAD_PAIRS = HEADS // 2
VMEM_LIMIT = 56 * 1024 * 1024

_C_CKV = (0, 128)
_C_KR = (128, 256)
_C_KRR = (256, 384)
_C_KB = (384, 896)
_C_VB = (896, 1408)
KV_EXT = 1408
_C_CQ = (1408, 1664)
_C_ZA = (1664, 2176)
_C_QB = (2176, 2688)
_C_ZB = (2688, 3200)
_C_GA = (3200, 4224)
_C_GB = (4224, 5248)
IN_EXT = 5248

NA_ROWS_PER_BLOCK = 4
NA_BLOCKS = GRID_H // NA_ROWS_PER_BLOCK
NA_TQ = NA_ROWS_PER_BLOCK * GRID_W
NA_SLAB_ROWS = 12
NA_SLAB = NA_SLAB_ROWS * GRID_W
NA_MASK = -1e30

F32 = jnp.float32
BF16 = jnp.bfloat16


def _dot(a, b):
    return jnp.dot(a, b, preferred_element_type=F32)


def _dot_nt(a, b):
    return lax.dot_general(a, b, (((1,), (1,)), ((), ())), preferred_element_type=F32)


def _rms(x, g):
    return x * lax.rsqrt(jnp.mean(x * x, axis=-1, keepdims=True) + EPS) * g


def _mod_kernel(c_ref, w_ref, b_ref, o_ref):
    c = c_ref[...]
    sc = c * jax.nn.sigmoid(c)
    o_ref[...] = _dot(sc.astype(BF16), w_ref[...].astype(BF16)) + b_ref[...]


def _mod_call(cs, w_mod, b_mod):
    n = w_mod.shape[1]
    tn = 1024
    return pl.pallas_call(
        _mod_kernel,
        grid=(n // tn,),
        in_specs=[
            pl.BlockSpec((8, D_MODEL), lambda j: (0, 0)),
            pl.BlockSpec((D_MODEL, tn), lambda j: (0, j)),
            pl.BlockSpec((1, tn), lambda j: (0, j)),
        ],
        out_specs=pl.BlockSpec((8, tn), lambda j: (0, j)),
        out_shape=jax.ShapeDtypeStruct((8, n), F32),
        name="adaln_mod",
    )(cs, w_mod, b_mod)


def _proj_kernel(with_q, x_ref, shift_ref, scale_ref, ng_ref, w_ref, gkv_ref, wk_ref, wv_ref,
                 ck_ref, sk_ref, *rest):
    if with_q:
        (gq_ref, wq_ref, wqr_ref, cq_ref, sq_ref,
         k_ref, v_ref, kb_ref, vb_ref, q_ref, qb_ref, sza_ref, szb_ref, sga_ref, sgb_ref) = rest
    else:
        k_ref, v_ref, kb_ref, vb_ref = rest

    x = x_ref[0]
    h = _rms(x, ng_ref[...]) * (1.0 + scale_ref[0]) + shift_ref[0]
    hb = h.astype(BF16)

    def proj(c):
        return _dot(hb, w_ref[:, c[0]:c[1]])

    ckvn = _rms(proj(_C_CKV), gkv_ref[...]).astype(BF16)
    knope = _dot(ckvn, wk_ref[...])
    krope = proj(_C_KR) * ck_ref[...] + proj(_C_KRR) * sk_ref[...]
    for hd in range(HEADS):
        sl = slice(hd * LANES, (hd + 1) * LANES)
        k_ref[0, :, sl] = (knope[:, sl] + krope).astype(BF16)
    v_ref[0] = _dot(ckvn, wv_ref[...]).astype(BF16)
    kb_ref[0] = proj(_C_KB).astype(BF16)
    vb_ref[0] = proj(_C_VB).astype(BF16)

    if with_q:
        cqn = _rms(proj(_C_CQ), gq_ref[...]).astype(BF16)
        q1 = _dot(cqn, wq_ref[...])
        q2 = _dot(cqn, wqr_ref[...])
        cq = cq_ref[...]
        sq = sq_ref[...]
        for hd in range(HEADS):
            sl = slice(hd * LANES, (hd + 1) * LANES)
            q_ref[0, :, sl] = (q1[:, sl] * cq + q2[:, sl] * sq).astype(BF16)
        qb_ref[0] = proj(_C_QB).astype(BF16)
        za = proj(_C_ZA)
        sza_ref[0] = (za * jax.nn.sigmoid(za)).astype(BF16)
        zb = proj(_C_ZB)
        szb_ref[0] = (zb * jax.nn.sigmoid(zb)).astype(BF16)
        sga_ref[0] = jax.nn.sigmoid(proj(_C_GA)).astype(BF16)
        sgb_ref[0] = jax.nn.sigmoid(proj(_C_GB)).astype(BF16)


def _proj_call(x, shift, scale, per_batch_mod, norm_g, w_ext, g_ckv, wk, wv, ck, sk,
               q_side=None, tm=256):
    b, s, d = x.shape
    with_q = q_side is not None
    n_ext = w_ext.shape[1]
    mod_map = (lambda i, j: (i, 0, 0)) if per_batch_mod else (lambda i, j: (0, 0, 0))
    const2 = lambda i, j: (0, 0)
    tok3 = lambda i, j: (i, j, 0)
    tab = lambda i, j: (j, 0)

    in_specs = [
        pl.BlockSpec((1, tm, d), tok3),
        pl.BlockSpec((1, 1, d), mod_map),
        pl.BlockSpec((1, 1, d), mod_map),
        pl.BlockSpec((1, d), const2),
        pl.BlockSpec((d, n_ext), const2),
        pl.BlockSpec((1, KV_LORA), const2),
        pl.BlockSpec((KV_LORA, HEADS * LANES), const2),
        pl.BlockSpec((KV_LORA, HEADS * MLA_V), const2),
        pl.BlockSpec((tm, LANES), tab),
        pl.BlockSpec((tm, LANES), tab),
    ]
    args = [x, shift, scale, norm_g, w_ext, g_ckv, wk, wv, ck, sk]
    widths = [HEADS * LANES, HEADS * MLA_V, HEADS * NA_DIM, HEADS * NA_DIM]
    if with_q:
        g_cq, wq, wqr, cq, sq = q_side
        in_specs += [
            pl.BlockSpec((1, Q_LORA), const2),
            pl.BlockSpec((Q_LORA, HEADS * LANES), const2),
            pl.BlockSpec((Q_LORA, HEADS * LANES), const2),
            pl.BlockSpec((tm, LANES), tab),
            pl.BlockSpec((tm, LANES), tab),
        ]
        args += [g_cq, wq, wqr, cq, sq]
        widths += [HEADS * LANES, HEADS * NA_DIM, HEADS * MLA_V, HEADS * NA_DIM, d, d]
    out_specs = [pl.BlockSpec((1, tm, w), tok3) for w in widths]
    out_shape = [jax.ShapeDtypeStruct((b, s, w), BF16) for w in widths]
    return pl.pallas_call(
        functools.partial(_proj_kernel, with_q),
        grid=(b, s // tm),
        in_specs=in_specs,
        out_specs=out_specs,
        out_shape=out_shape,
        compiler_params=pltpu.CompilerParams(vmem_limit_bytes=VMEM_LIMIT),
        name="in_proj_q" if with_q else "in_proj_ctx",
    )(*args)


def _mla_kernel(q_ref, k_ref, kc_ref, v_ref, vc_ref, o_ref):
    v = v_ref[0]
    vc = vc_ref[0]
    lane = lax.broadcasted_iota(jnp.int32, o_ref.shape[1:], 1)
    out = jnp.zeros(o_ref.shape[1:], F32)
    for hh in range(2):
        sl = slice(hh * LANES, (hh + 1) * LANES)
        q = q_ref[0, :, sl]
        s1 = _dot_nt(q, k_ref[0, :, sl])
        s2 = _dot_nt(q, kc_ref[0, :, sl])
        m = jnp.maximum(jnp.max(s1, axis=-1, keepdims=True), jnp.max(s2, axis=-1, keepdims=True))
        p1 = jnp.exp(s1 - m)
        p2 = jnp.exp(s2 - m)
        l = jnp.sum(p1, axis=-1, keepdims=True) + jnp.sum(p2, axis=-1, keepdims=True)
        o = (_dot(p1.astype(BF16), v) + _dot(p2.astype(BF16), vc)) / l
        out = jnp.where((lane // MLA_V) == hh, o, out)
    o_ref[0] = out.astype(BF16)


def _mla_call(q, k, kc, v, vc, tq=256):
    b, s, _ = q.shape
    t = kc.shape[1]
    return pl.pallas_call(
        _mla_kernel,
        grid=(b, HEAD_PAIRS, s // tq),
        in_specs=[
            pl.BlockSpec((1, tq, 2 * LANES), lambda i, p, j: (i, j, p)),
            pl.BlockSpec((1, s, 2 * LANES), lambda i, p, j: (i, 0, p)),
            pl.BlockSpec((1, t, 2 * LANES), lambda i, p, j: (i, 0, p)),
            pl.BlockSpec((1, s, LANES), lambda i, p, j: (i, 0, p)),
            pl.BlockSpec((1, t, LANES), lambda i, p, j: (i, 0, p)),
        ],
        out_specs=pl.BlockSpec((1, tq, LANES), lambda i, p, j: (i, j, p)),
        out_shape=jax.ShapeDtypeStruct((b, s, HEADS * MLA_V), BF16),
        compiler_params=pltpu.CompilerParams(vmem_limit_bytes=VMEM_LIMIT),
        name="mla_attention",
    )(q, k, kc, v, vc)


def _na_slab_start(blk):
    return jnp.clip(blk * NA_ROWS_PER_BLOCK - NA_KR // 2, 0, GRID_H - NA_SLAB_ROWS)


def _na_kernel(q_ref, k_ref, v_ref, kc_ref, vc_ref, bias_ref, o_ref):
    blk = pl.program_id(2)
    start = pl.multiple_of(_na_slab_start(blk) * GRID_W, GRID_W)
    ks = k_ref[0, pl.ds(start, NA_SLAB), :]
    vs = v_ref[0, pl.ds(start, NA_SLAB), :]
    kc = kc_ref[0]
    vc = vc_ref[0]
    q = q_ref[0]
    half = lax.broadcasted_iota(jnp.int32, q.shape, 1) // NA_DIM
    out = jnp.zeros(q.shape, F32)
    for hh in range(2):
        qm = jnp.where(half == hh, q, jnp.zeros_like(q))
        s1 = _dot_nt(qm, ks) + bias_ref[0, hh]
        s2 = _dot_nt(qm, kc)
        m = jnp.maximum(jnp.max(s1, axis=-1, keepdims=True), jnp.max(s2, axis=-1, keepdims=True))
        p1 = jnp.exp(s1 - m)
        p2 = jnp.exp(s2 - m)
        l = jnp.sum(p1, axis=-1, keepdims=True) + jnp.sum(p2, axis=-1, keepdims=True)
        o = (_dot(p1.astype(BF16), vs) + _dot(p2.astype(BF16), vc)) / l
        out = jnp.where(half == hh, o, out)
    o_ref[0] = out.astype(BF16)


def _na_variant(blk):
    return jnp.where(blk == 0, 0, jnp.where(blk == NA_BLOCKS - 1, 2, 1))


def _na_call(qb, kb, vb, kcb, vcb, bias):
    b, s, _ = qb.shape
    t = kcb.shape[1]
    return pl.pallas_call(
        _na_kernel,
        grid=(b, HEAD_PAIRS, NA_BLOCKS),
        in_specs=[
            pl.BlockSpec((1, NA_TQ, LANES), lambda i, p, j: (i, j, p)),
            pl.BlockSpec((1, s, LANES), lambda i, p, j: (i, 0, p)),
            pl.BlockSpec((1, s, LANES), lambda i, p, j: (i, 0, p)),
            pl.BlockSpec((1, t, LANES), lambda i, p, j: (i, 0, p)),
            pl.BlockSpec((1, t, LANES), lambda i, p, j: (i, 0, p)),
            pl.BlockSpec((1, 2, NA_TQ, NA_SLAB), lambda i, p, j: (_na_variant(j), p, 0, 0)),
        ],
        out_specs=pl.BlockSpec((1, NA_TQ, LANES), lambda i, p, j: (i, j, p)),
        out_shape=jax.ShapeDtypeStruct((b, s, HEADS * NA_DIM), BF16),
        compiler_params=pltpu.CompilerParams(vmem_limit_bytes=VMEM_LIMIT),
        name="na_attention",
    )(qb, kb, vb, kcb, vcb, bias)


def _na_bias_tables(rpb):
    idx = np.zeros((3, NA_TQ, NA_SLAB), np.int32)
    valid = np.zeros((3, NA_TQ, NA_SLAB), bool)
    i = np.arange(NA_ROWS_PER_BLOCK)[:, None, None, None]
    w = np.arange(GRID_W)[None, :, None, None]
    j = np.arange(NA_SLAB_ROWS)[None, None, :, None]
    wk = np.arange(GRID_W)[None, None, None, :]
    for n, blk in enumerate((0, 1, NA_BLOCKS - 1)):
        sb = int(np.clip(blk * NA_ROWS_PER_BLOCK - NA_KR // 2, 0, GRID_H - NA_SLAB_ROWS))
        r = blk * NA_ROWS_PER_BLOCK + i
        kr = sb + j
        r_start = np.clip(r - NA_KR // 2, 0, GRID_H - NA_KR)
        c_start = np.clip(w - NA_KC // 2, 0, GRID_W - NA_KC)
        ok = (kr >= r_start) & (kr < r_start + NA_KR) & (wk >= c_start) & (wk < c_start + NA_KC)
        r_off = np.clip(kr - r + NA_KR - 1, 0, 2 * NA_KR - 2)
        c_off = np.clip(wk - w + NA_KC - 1, 0, 2 * NA_KC - 2)
        flat = r_off * (2 * NA_KC - 1) + c_off
        idx[n] = np.broadcast_to(flat, ok.shape).reshape(NA_TQ, NA_SLAB)
        valid[n] = ok.reshape(NA_TQ, NA_SLAB)
    g = jnp.take(rpb.reshape(HEADS, -1), jnp.asarray(idx), axis=1)
    g = jnp.where(jnp.asarray(valid)[None], g, NA_MASK)
    return jnp.transpose(g, (1, 0, 2, 3))


def _out_kernel(oa_ref, ob_ref, sza_ref, szb_ref, sga_ref, sgb_ref, x_ref, gate_ref,
                woa_ref, wob_ref, wout_ref, fg_ref, o_ref):
    ya = _dot(oa_ref[0] * sza_ref[0], woa_ref[...])
    yb = _dot(ob_ref[0] * szb_ref[0], wob_ref[...])
    mix = sga_ref[0].astype(F32) * ya + sgb_ref[0].astype(F32) * yb
    y = _dot(mix.astype(BF16), wout_ref[...])
    r = x_ref[0] + gate_ref[0] * y
    o_ref[0] = _rms(r, fg_ref[...])


def _out_call(oa, ob, sza, szb, sga, sgb, x, gate, w_oa, w_ob, w_out, final_g, tm=512):
    b, s, d = x.shape
    tok3 = lambda i, j: (i, j, 0)
    const2 = lambda i, j: (0, 0)
    half = HEADS * MLA_V
    return pl.pallas_call(
        _out_kernel,
        grid=(b, s // tm),
        in_specs=[
            pl.BlockSpec((1, tm, half), tok3),
            pl.BlockSpec((1, tm, half), tok3),
            pl.BlockSpec((1, tm, half), tok3),
            pl.BlockSpec((1, tm, half), tok3),
            pl.BlockSpec((1, tm, d), tok3),
            pl.BlockSpec((1, tm, d), tok3),
            pl.BlockSpec((1, tm, d), tok3),
            pl.BlockSpec((1, 1, d), lambda i, j: (i, 0, 0)),
            pl.BlockSpec((half, d), const2),
            pl.BlockSpec((half, d), const2),
            pl.BlockSpec((d, d), const2),
            pl.BlockSpec((1, d), const2),
        ],
        out_specs=pl.BlockSpec((1, tm, d), tok3),
        out_shape=jax.ShapeDtypeStruct((b, s, d), F32),
        compiler_params=pltpu.CompilerParams(vmem_limit_bytes=VMEM_LIMIT),
        name="gated_out",
    )(oa, ob, sza, szb, sga, sgb, x, gate, w_oa, w_ob, w_out, final_g)


def _rope_partner():
    j = np.arange(MLA_ROPE)
    first = (j % 16) < 8
    src = np.where(first, j + 8, j - 8)
    sign = np.where(first, -1.0, 1.0).astype(np.float32)
    return src, sign


def _rope_tables():
    t = jnp.arange(SEQ)
    half = MLA_ROPE // 4
    freqs = ROPE_THETA ** (-jnp.arange(half, dtype=F32) / half)
    ang_r = (t // GRID_W).astype(F32)[:, None] * freqs[None, :]
    ang_c = (t % GRID_W).astype(F32)[:, None] * freqs[None, :]
    cos = jnp.concatenate([jnp.cos(ang_r)] * 2 + [jnp.cos(ang_c)] * 2, axis=-1)
    sin = jnp.concatenate([jnp.sin(ang_r)] * 2 + [jnp.sin(ang_c)] * 2, axis=-1)
    return cos, sin


def _lane_group(nope, rope, n):
    pad = jnp.zeros((n, LANES - MLA_NOPE - MLA_ROPE), F32)
    return jnp.concatenate([nope, rope, pad], axis=-1)


def kernel(x, c, ctx, c_ctx, w_mod, b_mod, norm_g, w_in, g_cq, w_uq, g_ckv, w_ukv, rpb,
           w_oa, w_ob, w_out, final_g):
    b, s, d = x.shape
    depth = w_mod.shape[0]
    assert depth == 1 and s == SEQ and d == D_MODEL
    mla_scale = (MLA_NOPE + MLA_ROPE) ** -0.5
    na_scale = NA_DIM ** -0.5
    src, sign = _rope_partner()

    w = w_in[0]
    o = np.cumsum([0, KV_LORA, MLA_ROPE, HEADS * NA_DIM, HEADS * NA_DIM, Q_LORA, HEADS * MLA_V,
                   HEADS * NA_DIM, HEADS * NA_DIM, d, d])
    seg = [w[:, o[n]:o[n + 1]] for n in range(10)]
    w_ckv, w_kr, w_kb, w_vb, w_cq, w_za, w_qb, w_zb, w_ga, w_gb = seg
    z64 = jnp.zeros((d, MLA_NOPE), F32)
    w_ext = jnp.concatenate([
        w_ckv,
        _lane_group(z64, w_kr, d),
        _lane_group(z64, w_kr[:, src] * sign, d),
        w_kb, w_vb, w_cq, w_za, w_qb * na_scale, w_zb, w_ga, w_gb], axis=-1).astype(BF16)

    wkv3 = w_ukv[0].reshape(KV_LORA, HEADS, MLA_NOPE + MLA_V)
    wk = jnp.concatenate([wkv3[:, :, :MLA_NOPE], jnp.zeros((KV_LORA, HEADS, LANES - MLA_NOPE), F32)],
                         axis=-1).reshape(KV_LORA, HEADS * LANES).astype(BF16)
    wv = wkv3[:, :, MLA_NOPE:].reshape(KV_LORA, HEADS * MLA_V).astype(BF16)
    wq3 = w_uq[0].reshape(Q_LORA, HEADS, MLA_NOPE + MLA_ROPE)
    zq = jnp.zeros((Q_LORA, HEADS, LANES - MLA_NOPE - MLA_ROPE), F32)
    wq = jnp.concatenate([wq3, zq], axis=-1).reshape(Q_LORA, HEADS * LANES).astype(BF16)
    wqr = jnp.concatenate([jnp.zeros((Q_LORA, HEADS, MLA_NOPE), F32),
                           wq3[:, :, MLA_NOPE:][:, :, src] * sign, zq],
                          axis=-1).reshape(Q_LORA, HEADS * LANES).astype(BF16)

    cos, sin = _rope_tables()
    ones_s = jnp.ones((s, MLA_NOPE), F32)
    zeros_s = jnp.zeros((s, MLA_NOPE), F32)
    cq_tab = _lane_group(ones_s, cos, s) * mla_scale
    sq_tab = _lane_group(zeros_s, sin, s) * mla_scale
    ck_tab = _lane_group(zeros_s, cos, s)
    sk_tab = _lane_group(zeros_s, sin, s)
    zeros_c = jnp.zeros((CTX_LEN, MLA_NOPE), F32)
    ck_ctx = _lane_group(zeros_c, jnp.ones((CTX_LEN, MLA_ROPE), F32), CTX_LEN)
    sk_ctx = jnp.zeros((CTX_LEN, LANES), F32)

    cs = jnp.concatenate([c, c_ctx[None, :], jnp.zeros((8 - b - 1, d), F32)], axis=0)
    mod = _mod_call(cs, w_mod[0], b_mod[0][None, :])
    shift, scale, gate = (mod[:b, None, n * d:(n + 1) * d] for n in range(3))
    shift_c, scale_c = (mod[b:b + 1, None, n * d:(n + 1) * d] for n in range(2))

    ng = norm_g[0][None, :]
    gkv = g_ckv[0][None, :]
    k, v, kb, vb, q, qb, sza, szb, sga, sgb = _proj_call(
        x, shift, scale, True, ng, w_ext, gkv, wk, wv, ck_tab, sk_tab,
        q_side=(g_cq[0][None, :], wq, wqr, cq_tab, sq_tab))
    kc, vc, kcb, vcb = _proj_call(
        ctx, shift_c, scale_c, False, ng, w_ext[:, :KV_EXT], gkv, wk, wv, ck_ctx, sk_ctx)

    oa = _mla_call(q, k, kc, v, vc)
    ob = _na_call(qb, kb, vb, kcb, vcb, _na_bias_tables(rpb[0]))

    return _out_call(oa, ob, sza, szb, sga, sgb, x, gate,
                     w_oa[0].astype(BF16), w_ob[0].astype(BF16), w_out[0].astype(BF16),
                     final_g[None, :])
```

```python
import functools

import jax
import jax.numpy as jnp
import numpy as np
from jax import lax
from jax.experimental import pallas as pl
from jax.experimental.pallas import tpu as pltpu

D_MODEL = 1024
SEQ = 4096
GRID_W = 64
GRID_H = SEQ // GRID_W
CTX_LEN = 256
HEADS = 8
MLA_NOPE = 64
MLA_ROPE = 32
MLA_V = 64
Q_LORA = 256
KV_LORA = 128
NA_DIM = 64
NA_KR = 8
NA_KC = 16
ROPE_THETA = 10000.0
EPS = 1e-6

LANES = 128
HEAD_PAIRS = HEADS // 2
VMEM_LIMIT = 56 * 1024 * 1024

_C_CKV = (0, 128)
_C_KR = (128, 256)
_C_KRR = (256, 384)
_C_KB = (384, 896)
_C_VB = (896, 1408)
KV_EXT = 1408
_C_CQ = (1408, 1664)
_C_ZA = (1664, 2176)
_C_QB = (2176, 2688)
_C_ZB = (2688, 3200)
_C_GA = (3200, 4224)
_C_GB = (4224, 5248)
IN_EXT = 5248

NA_ROWS_PER_BLOCK = 4
NA_BLOCKS = GRID_H // NA_ROWS_PER_BLOCK
NA_TQ = NA_ROWS_PER_BLOCK * GRID_W
NA_SLAB_ROWS = 12
NA_SLAB = NA_SLAB_ROWS * GRID_W
NA_MASK = -1e30

F32 = jnp.float32
BF16 = jnp.bfloat16


def _dot(a, b):
    return jnp.dot(a, b, preferred_element_type=F32)


def _dot_nt(a, b):
    return lax.dot_general(a, b, (((1,), (1,)), ((), ())), preferred_element_type=F32)


def _rms(x, g):
    return x * lax.rsqrt(jnp.mean(x * x, axis=-1, keepdims=True) + EPS) * g


def _mod_kernel(c_ref, w_ref, b_ref, o_ref):
    c = c_ref[...]
    sc = c * jax.nn.sigmoid(c)
    o_ref[...] = _dot(sc.astype(BF16), w_ref[...].astype(BF16)) + b_ref[...]


def _mod_call(cs, w_mod, b_mod):
    n = w_mod.shape[1]
    tn = 1024
    return pl.pallas_call(
        _mod_kernel,
        grid=(n // tn,),
        in_specs=[
            pl.BlockSpec((8, D_MODEL), lambda j: (0, 0)),
            pl.BlockSpec((D_MODEL, tn), lambda j: (0, j)),
            pl.BlockSpec((1, tn), lambda j: (0, j)),
        ],
        out_specs=pl.BlockSpec((8, tn), lambda j: (0, j)),
        out_shape=jax.ShapeDtypeStruct((8, n), F32),
        name="adaln_mod",
    )(cs, w_mod, b_mod)


def _proj_kernel(with_q, x_ref, shift_ref, scale_ref, ng_ref, w_ref, gkv_ref, wk_ref, wv_ref,
                 ck_ref, sk_ref, *rest):
    if with_q:
        (gq_ref, wq_ref, wqr_ref, cq_ref, sq_ref,
         k_ref, v_ref, kb_ref, vb_ref, q_ref, qb_ref, sza_ref, szb_ref, sga_ref, sgb_ref) = rest
    else:
        k_ref, v_ref, kb_ref, vb_ref = rest

    x = x_ref[0]
    h = _rms(x, ng_ref[...]) * (1.0 + scale_ref[0]) + shift_ref[0]
    hb = h.astype(BF16)

    def proj(c):
        return _dot(hb, w_ref[:, c[0]:c[1]])

    ckvn = _rms(proj(_C_CKV), gkv_ref[...]).astype(BF16)
    knope = _dot(ckvn, wk_ref[...])
    krope = proj(_C_KR) * ck_ref[...] + proj(_C_KRR) * sk_ref[...]
    for hd in range(HEADS):
        sl = slice(hd * LANES, (hd + 1) * LANES)
        k_ref[0, :, sl] = (knope[:, sl] + krope).astype(BF16)
    v_ref[0] = _dot(ckvn, wv_ref[...]).astype(BF16)
    kb_ref[0] = proj(_C_KB).astype(BF16)
    vb_ref[0] = proj(_C_VB).astype(BF16)

    if with_q:
        cqn = _rms(proj(_C_CQ), gq_ref[...]).astype(BF16)
        q1 = _dot(cqn, wq_ref[...])
        q2 = _dot(cqn, wqr_ref[...])
        cq = cq_ref[...]
        sq = sq_ref[...]
        for hd in range(HEADS):
            sl = slice(hd * LANES, (hd + 1) * LANES)
            q_ref[0, :, sl] = (q1[:, sl] * cq + q2[:, sl] * sq).astype(BF16)
        qb_ref[0] = proj(_C_QB).astype(BF16)
        za = proj(_C_ZA)
        sza_ref[0] = (za * jax.nn.sigmoid(za)).astype(BF16)
        zb = proj(_C_ZB)
        szb_ref[0] = (zb * jax.nn.sigmoid(zb)).astype(BF16)
        sga_ref[0] = jax.nn.sigmoid(proj(_C_GA)).astype(BF16)
        sgb_ref[0] = jax.nn.sigmoid(proj(_C_GB)).astype(BF16)


def _proj_call(x, shift, scale, per_batch_mod, norm_g, w_ext, g_ckv, wk, wv, ck, sk,
               q_side=None, tm=256):
    b, s, d = x.shape
    with_q = q_side is not None
    n_ext = w_ext.shape[1]
    mod_map = (lambda i, j: (i, 0, 0)) if per_batch_mod else (lambda i, j: (0, 0, 0))
    const2 = lambda i, j: (0, 0)
    tok3 = lambda i, j: (i, j, 0)
    tab = lambda i, j: (j, 0)

    in_specs = [
        pl.BlockSpec((1, tm, d), tok3),
        pl.BlockSpec((1, 1, d), mod_map),
        pl.BlockSpec((1, 1, d), mod_map),
        pl.BlockSpec((1, d), const2),
        pl.BlockSpec((d, n_ext), const2),
        pl.BlockSpec((1, KV_LORA), const2),
        pl.BlockSpec((KV_LORA, HEADS * LANES), const2),
        pl.BlockSpec((KV_LORA, HEADS * MLA_V), const2),
        pl.BlockSpec((tm, LANES), tab),
        pl.BlockSpec((tm, LANES), tab),
    ]
    args = [x, shift, scale, norm_g, w_ext, g_ckv, wk, wv, ck, sk]
    widths = [HEADS * LANES, HEADS * MLA_V, HEADS * NA_DIM, HEADS * NA_DIM]
    if with_q:
        g_cq, wq, wqr, cq, sq = q_side
        in_specs += [
            pl.BlockSpec((1, Q_LORA), const2),
            pl.BlockSpec((Q_LORA, HEADS * LANES), const2),
            pl.BlockSpec((Q_LORA, HEADS * LANES), const2),
            pl.BlockSpec((tm, LANES), tab),
            pl.BlockSpec((tm, LANES), tab),
        ]
        args += [g_cq, wq, wqr, cq, sq]
        widths += [HEADS * LANES, HEADS * NA_DIM, HEADS * MLA_V, HEADS * NA_DIM, d, d]
    out_specs = [pl.BlockSpec((1, tm, w), tok3) for w in widths]
    out_shape = [jax.ShapeDtypeStruct((b, s, w), BF16) for w in widths]
    return pl.pallas_call(
        functools.partial(_proj_kernel, with_q),
        grid=(b, s // tm),
        in_specs=in_specs,
        out_specs=out_specs,
        out_shape=out_shape,
        compiler_params=pltpu.CompilerParams(vmem_limit_bytes=VMEM_LIMIT),
        name="in_proj_q" if with_q else "in_proj_ctx",
    )(*args)


def _mla_kernel(q_ref, k_ref, kc_ref, v_ref, vc_ref, o_ref):
    v = v_ref[0]
    vc = vc_ref[0]
    lane = lax.broadcasted_iota(jnp.int32, o_ref.shape[1:], 1)
    out = jnp.zeros(o_ref.shape[1:], F32)
    for hh in range(2):
        sl = slice(hh * LANES, (hh + 1) * LANES)
        q = q_ref[0, :, sl]
        s1 = _dot_nt(q, k_ref[0, :, sl])
        s2 = _dot_nt(q, kc_ref[0, :, sl])
        m = jnp.maximum(jnp.max(s1, axis=-1, keepdims=True), jnp.max(s2, axis=-1, keepdims=True))
        p1 = jnp.exp(s1 - m)
        p2 = jnp.exp(s2 - m)
        l = jnp.sum(p1, axis=-1, keepdims=True) + jnp.sum(p2, axis=-1, keepdims=True)
        o = (_dot(p1.astype(BF16), v) + _dot(p2.astype(BF16), vc)) / l
        out = jnp.where((lane // MLA_V) == hh, o, out)
    o_ref[0] = out.astype(BF16)


def _mla_call(q, k, kc, v, vc, tq=256):
    b, s, _ = q.shape
    t = kc.shape[1]
    return pl.pallas_call(
        _mla_kernel,
        grid=(b, HEAD_PAIRS, s // tq),
        in_specs=[
            pl.BlockSpec((1, tq, 2 * LANES), lambda i, p, j: (i, j, p)),
            pl.BlockSpec((1, s, 2 * LANES), lambda i, p, j: (i, 0, p)),
            pl.BlockSpec((1, t, 2 * LANES), lambda i, p, j: (i, 0, p)),
            pl.BlockSpec((1, s, LANES), lambda i, p, j: (i, 0, p)),
            pl.BlockSpec((1, t, LANES), lambda i, p, j: (i, 0, p)),
        ],
        out_specs=pl.BlockSpec((1, tq, LANES), lambda i, p, j: (i, j, p)),
        out_shape=jax.ShapeDtypeStruct((b, s, HEADS * MLA_V), BF16),
        compiler_params=pltpu.CompilerParams(vmem_limit_bytes=VMEM_LIMIT),
        name="mla_attention",
    )(q, k, kc, v, vc)


def _na_slab_start(blk):
    return jnp.clip(blk * NA_ROWS_PER_BLOCK - NA_KR // 2, 0, GRID_H - NA_SLAB_ROWS)


def _na_kernel(q_ref, k_ref, v_ref, kc_ref, vc_ref, bias_ref, o_ref):
    blk = pl.program_id(2)
    start = pl.multiple_of(_na_slab_start(blk) * GRID_W, GRID_W)
    ks = k_ref[0, pl.ds(start, NA_SLAB), :]
    vs = v_ref[0, pl.ds(start, NA_SLAB), :]
    kc = kc_ref[0]
    vc = vc_ref[0]
    q = q_ref[0]
    half = lax.broadcasted_iota(jnp.int32, q.shape, 1) // NA_DIM
    out = jnp.zeros(q.shape, F32)
    for hh in range(2):
        qm = jnp.where(half == hh, q, jnp.zeros_like(q))
        s1 = _dot_nt(qm, ks) + bias_ref[0, hh]
        s2 = _dot_nt(qm, kc)
        m = jnp.maximum(jnp.max(s1, axis=-1, keepdims=True), jnp.max(s2, axis=-1, keepdims=True))
        p1 = jnp.exp(s1 - m)
        p2 = jnp.exp(s2 - m)
        l = jnp.sum(p1, axis=-1, keepdims=True) + jnp.sum(p2, axis=-1, keepdims=True)
        o = (_dot(p1.astype(BF16), vs) + _dot(p2.astype(BF16), vc)) / l
        out = jnp.where(half == hh, o, out)
    o_ref[0] = out.astype(BF16)


def _na_variant(blk):
    return jnp.where(blk == 0, 0, jnp.where(blk == NA_BLOCKS - 1, 2, 1))


def _na_call(qb, kb, vb, kcb, vcb, bias):
    b, s, _ = qb.shape
    t = kcb.shape[1]
    return pl.pallas_call(
        _na_kernel,
        grid=(b, HEAD_PAIRS, NA_BLOCKS),
        in_specs=[
            pl.BlockSpec((1, NA_TQ, LANES), lambda i, p, j: (i, j, p)),
            pl.BlockSpec((1, s, LANES), lambda i, p, j: (i, 0, p)),
            pl.BlockSpec((1, s, LANES), lambda i, p, j: (i, 0, p)),
            pl.BlockSpec((1, t, LANES), lambda i, p, j: (i, 0, p)),
            pl.BlockSpec((1, t, LANES), lambda i, p, j: (i, 0, p)),
            pl.BlockSpec((1, 2, NA_TQ, NA_SLAB), lambda i, p, j: (_na_variant(j), p, 0, 0)),
        ],
        out_specs=pl.BlockSpec((1, NA_TQ, LANES), lambda i, p, j: (i, j, p)),
        out_shape=jax.ShapeDtypeStruct((b, s, HEADS * NA_DIM), BF16),
        compiler_params=pltpu.CompilerParams(vmem_limit_bytes=VMEM_LIMIT),
        name="na_attention",
    )(qb, kb, vb, kcb, vcb, bias)


def _na_bias_kernel(rpb_ref, o_ref):
    rp = pltpu.roll(rpb_ref[0], LANES - (NA_KC - 1), axis=1)
    shape = (GRID_W, LANES)
    w = lax.broadcasted_iota(jnp.int32, shape, 0)
    lane = lax.broadcasted_iota(jnp.int32, shape, 1)
    c_start = jnp.clip(w - NA_KC // 2, 0, GRID_W - NA_KC)
    left = lane < GRID_W
    mask_l = jnp.where(left, NA_MASK, 0.0).astype(F32)
    mask_r = jnp.where(left, 0.0, NA_MASK).astype(F32)
    tiles_l, tiles_r = [], []
    for d in range(2 * NA_KR - 1):
        row = jnp.broadcast_to(rp[d:d + 1, :], shape)
        t = pltpu.roll(row, 0, axis=1, stride=1, stride_axis=0)
        t = jnp.where(lane >= c_start, jnp.where(lane < c_start + NA_KC, t, mask_l), mask_l)
        tiles_l.append(t)
        tiles_r.append(pltpu.roll(t, GRID_W, axis=1))
    for n, blk in enumerate((0, 1, NA_BLOCKS - 1)):
        sb = min(max(blk * NA_ROWS_PER_BLOCK - NA_KR // 2, 0), GRID_H - NA_SLAB_ROWS)
        for i in range(NA_ROWS_PER_BLOCK):
            r = blk * NA_ROWS_PER_BLOCK + i
            r_start = min(max(r - NA_KR // 2, 0), GRID_H - NA_KR)

            def pick(j, tiles, masked):
                kr = sb + j
                if r_start <= kr < r_start + NA_KR:
                    return tiles[kr - r + NA_KR - 1]
                return masked

            for jj in range(NA_SLAB_ROWS // 2):
                o_ref[n, 0, i * GRID_W:(i + 1) * GRID_W, jj * LANES:(jj + 1) * LANES] = (
                    pick(2 * jj, tiles_l, mask_l) + pick(2 * jj + 1, tiles_r, mask_r))


def _na_bias_tables(rpb):
    rpb_p = jnp.pad(rpb, ((0, 0), (0, 1), (0, LANES - (2 * NA_KC - 1))))
    return pl.pallas_call(
        _na_bias_kernel,
        grid=(HEADS,),
        in_specs=[pl.BlockSpec((1, 2 * NA_KR, LANES), lambda h: (h, 0, 0))],
        out_specs=pl.BlockSpec((3, 1, NA_TQ, NA_SLAB), lambda h: (0, h, 0, 0)),
        out_shape=jax.ShapeDtypeStruct((3, HEADS, NA_TQ, NA_SLAB), F32),
        name="na_bias_table",
    )(rpb_p)


def _out_kernel(oa_ref, ob_ref, sza_ref, szb_ref, sga_ref, sgb_ref, x_ref, gate_ref,
                woa_ref, wob_ref, wout_ref, fg_ref, o_ref):
    ya = _dot(oa_ref[0] * sza_ref[0], woa_ref[...])
    yb = _dot(ob_ref[0] * szb_ref[0], wob_ref[...])
    mix = sga_ref[0].astype(F32) * ya + sgb_ref[0].astype(F32) * yb
    y = _dot(mix.astype(BF16), wout_ref[...])
    r = x_ref[0] + gate_ref[0] * y
    o_ref[0] = _rms(r, fg_ref[...])


def _out_call(oa, ob, sza, szb, sga, sgb, x, gate, w_oa, w_ob, w_out, final_g, tm=512):
    b, s, d = x.shape
    tok3 = lambda i, j: (i, j, 0)
    const2 = lambda i, j: (0, 0)
    half = HEADS * MLA_V
    return pl.pallas_call(
        _out_kernel,
        grid=(b, s // tm),
        in_specs=[
            pl.BlockSpec((1, tm, half), tok3),
            pl.BlockSpec((1, tm, half), tok3),
            pl.BlockSpec((1, tm, half), tok3),
            pl.BlockSpec((1, tm, half), tok3),
            pl.BlockSpec((1, tm, d), tok3),
            pl.BlockSpec((1, tm, d), tok3),
            pl.BlockSpec((1, tm, d), tok3),
            pl.BlockSpec((1, 1, d), lambda i, j: (i, 0, 0)),
            pl.BlockSpec((half, d), const2),
            pl.BlockSpec((half, d), const2),
            pl.BlockSpec((d, d), const2),
            pl.BlockSpec((1, d), const2),
        ],
        out_specs=pl.BlockSpec((1, tm, d), tok3),
        out_shape=jax.ShapeDtypeStruct((b, s, d), F32),
        compiler_params=pltpu.CompilerParams(vmem_limit_bytes=VMEM_LIMIT),
        name="gated_out",
    )(oa, ob, sza, szb, sga, sgb, x, gate, w_oa, w_ob, w_out, final_g)


def _rope_partner():
    j = np.arange(MLA_ROPE)
    first = (j % 16) < 8
    src = np.where(first, j + 8, j - 8)
    sign = np.where(first, -1.0, 1.0).astype(np.float32)
    return src, sign


def _rope_tables():
    t = jnp.arange(SEQ)
    half = MLA_ROPE // 4
    freqs = ROPE_THETA ** (-jnp.arange(half, dtype=F32) / half)
    ang_r = (t // GRID_W).astype(F32)[:, None] * freqs[None, :]
    ang_c = (t % GRID_W).astype(F32)[:, None] * freqs[None, :]
    cos = jnp.concatenate([jnp.cos(ang_r)] * 2 + [jnp.cos(ang_c)] * 2, axis=-1)
    sin = jnp.concatenate([jnp.sin(ang_r)] * 2 + [jnp.sin(ang_c)] * 2, axis=-1)
    return cos, sin


def _lane_group(nope, rope, n):
    pad = jnp.zeros((n, LANES - MLA_NOPE - MLA_ROPE), F32)
    return jnp.concatenate([nope, rope, pad], axis=-1)


def kernel(x, c, ctx, c_ctx, w_mod, b_mod, norm_g, w_in, g_cq, w_uq, g_ckv, w_ukv, rpb,
           w_oa, w_ob, w_out, final_g):
    b, s, d = x.shape
    depth = w_mod.shape[0]
    assert depth == 1 and s == SEQ and d == D_MODEL
    mla_scale = (MLA_NOPE + MLA_ROPE) ** -0.5
    na_scale = NA_DIM ** -0.5
    src, sign = _rope_partner()

    w = w_in[0]
    o = np.cumsum([0, KV_LORA, MLA_ROPE, HEADS * NA_DIM, HEADS * NA_DIM, Q_LORA, HEADS * MLA_V,
                   HEADS * NA_DIM, HEADS * NA_DIM, d, d])
    seg = [w[:, o[n]:o[n + 1]] for n in range(10)]
    w_ckv, w_kr, w_kb, w_vb, w_cq, w_za, w_qb, w_zb, w_ga, w_gb = seg
    z64 = jnp.zeros((d, MLA_NOPE), F32)
    w_ext = jnp.concatenate([
        w_ckv,
        _lane_group(z64, w_kr, d),
        _lane_group(z64, w_kr[:, src] * sign, d),
        w_kb, w_vb, w_cq, w_za, w_qb * na_scale, w_zb, w_ga, w_gb], axis=-1).astype(BF16)

    wkv3 = w_ukv[0].reshape(KV_LORA, HEADS, MLA_NOPE + MLA_V)
    wk = jnp.concatenate([wkv3[:, :, :MLA_NOPE], jnp.zeros((KV_LORA, HEADS, LANES - MLA_NOPE), F32)],
                         axis=-1).reshape(KV_LORA, HEADS * LANES).astype(BF16)
    wv = wkv3[:, :, MLA_NOPE:].reshape(KV_LORA, HEADS * MLA_V).astype(BF16)
    wq3 = w_uq[0].reshape(Q_LORA, HEADS, MLA_NOPE + MLA_ROPE)
    zq = jnp.zeros((Q_LORA, HEADS, LANES - MLA_NOPE - MLA_ROPE), F32)
    wq = jnp.concatenate([wq3, zq], axis=-1).reshape(Q_LORA, HEADS * LANES).astype(BF16)
    wqr = jnp.concatenate([jnp.zeros((Q_LORA, HEADS, MLA_NOPE), F32),
                           wq3[:, :, MLA_NOPE:][:, :, src] * sign, zq],
                          axis=-1).reshape(Q_LORA, HEADS * LANES).astype(BF16)

    cos, sin = _rope_tables()
    ones_s = jnp.ones((s, MLA_NOPE), F32)
    zeros_s = jnp.zeros((s, MLA_NOPE), F32)
    cq_tab = _lane_group(ones_s, cos, s) * mla_scale
    sq_tab = _lane_group(zeros_s, sin, s) * mla_scale
    ck_tab = _lane_group(zeros_s, cos, s)
    sk_tab = _lane_group(zeros_s, sin, s)
    zeros_c = jnp.zeros((CTX_LEN, MLA_NOPE), F32)
    ck_ctx = _lane_group(zeros_c, jnp.ones((CTX_LEN, MLA_ROPE), F32), CTX_LEN)
    sk_ctx = jnp.zeros((CTX_LEN, LANES), F32)

    cs = jnp.concatenate([c, c_ctx[None, :], jnp.zeros((8 - b - 1, d), F32)], axis=0)
    mod = _mod_call(cs, w_mod[0], b_mod[0][None, :])
    shift, scale, gate = (mod[:b, None, n * d:(n + 1) * d] for n in range(3))
    shift_c, scale_c = (mod[b:b + 1, None, n * d:(n + 1) * d] for n in range(2))

    ng = norm_g[0][None, :]
    gkv = g_ckv[0][None, :]
    k, v, kb, vb, q, qb, sza, szb, sga, sgb = _proj_call(
        x, shift, scale, True, ng, w_ext, gkv, wk, wv, ck_tab, sk_tab,
        q_side=(g_cq[0][None, :], wq, wqr, cq_tab, sq_tab))
    kc, vc, kcb, vcb = _proj_call(
        ctx, shift_c, scale_c, False, ng, w_ext[:, :KV_EXT], gkv, wk, wv, ck_ctx, sk_ctx)

    oa = _mla_call(q, k, kc, v, vc)
    ob = _na_call(qb, kb, vb, kcb, vcb, _na_bias_tables(rpb[0]))

    return _out_call(oa, ob, sza, szb, sga, sgb, x, gate,
                     w_oa[0].astype(BF16), w_ob[0].astype(BF16), w_out[0].astype(BF16),
                     final_g[None, :])
```

```python
import functools

import jax
import jax.numpy as jnp
import numpy as np
from jax import lax
from jax.experimental import pallas as pl
from jax.experimental.pallas import tpu as pltpu

D_MODEL = 1024
SEQ = 4096
GRID_W = 64
GRID_H = SEQ // GRID_W
CTX_LEN = 256
HEADS = 8
MLA_NOPE = 64
MLA_ROPE = 32
MLA_V = 64
Q_LORA = 256
KV_LORA = 128
NA_DIM = 64
NA_KR = 8
NA_KC = 16
ROPE_THETA = 10000.0
EPS = 1e-6

LANES = 128
HEAD_PAIRS = HEADS // 2
VMEM_LIMIT = 56 * 1024 * 1024

_C_CKV = (0, 128)
_C_KR = (128, 256)
_C_KRR = (256, 384)
_C_KB = (384, 896)
_C_VB = (896, 1408)
KV_EXT = 1408
_C_CQ = (1408, 1664)
_C_ZA = (1664, 2176)
_C_QB = (2176, 2688)
_C_ZB = (2688, 3200)
_C_GA = (3200, 4224)
_C_GB = (4224, 5248)
IN_EXT = 5248

NA_ROWS_PER_BLOCK = 4
NA_BLOCKS = GRID_H // NA_ROWS_PER_BLOCK
NA_TQ = NA_ROWS_PER_BLOCK * GRID_W
NA_SLAB_ROWS = 12
NA_SLAB = NA_SLAB_ROWS * GRID_W
NA_MASK = -1e30
LOG2_E = 1.4426950408889634

F32 = jnp.float32
BF16 = jnp.bfloat16


def _dot(a, b):
    return jnp.dot(a, b, preferred_element_type=F32)


def _dot_nt(a, b):
    return lax.dot_general(a, b, (((1,), (1,)), ((), ())), preferred_element_type=F32)


def _rms(x, g):
    return x * lax.rsqrt(jnp.mean(x * x, axis=-1, keepdims=True) + EPS) * g


def _mod_kernel(c_ref, w_ref, b_ref, o_ref):
    c = c_ref[...]
    sc = c * jax.nn.sigmoid(c)
    o_ref[...] = _dot(sc.astype(BF16), w_ref[...].astype(BF16)) + b_ref[...]


def _mod_call(cs, w_mod, b_mod):
    n = w_mod.shape[1]
    tn = 1024
    return pl.pallas_call(
        _mod_kernel,
        grid=(n // tn,),
        in_specs=[
            pl.BlockSpec((8, D_MODEL), lambda j: (0, 0)),
            pl.BlockSpec((D_MODEL, tn), lambda j: (0, j)),
            pl.BlockSpec((1, tn), lambda j: (0, j)),
        ],
        out_specs=pl.BlockSpec((8, tn), lambda j: (0, j)),
        out_shape=jax.ShapeDtypeStruct((8, n), F32),
        name="adaln_mod",
    )(cs, w_mod, b_mod)


def _proj_kernel(with_q, x_ref, shift_ref, scale_ref, ng_ref, w_ref, gkv_ref, wk_ref, wv_ref,
                 ck_ref, sk_ref, *rest):
    if with_q:
        (gq_ref, wq_ref, wqr_ref, cq_ref, sq_ref,
         k_ref, v_ref, kb_ref, vb_ref, q_ref, qb_ref, sza_ref, szb_ref, sga_ref, sgb_ref) = rest
    else:
        k_ref, v_ref, kb_ref, vb_ref = rest

    x = x_ref[0]
    h = _rms(x, ng_ref[...]) * (1.0 + scale_ref[0]) + shift_ref[0]
    hb = h.astype(BF16)

    def proj(c):
        return _dot(hb, w_ref[:, c[0]:c[1]])

    ckvn = _rms(proj(_C_CKV), gkv_ref[...]).astype(BF16)
    knope = _dot(ckvn, wk_ref[...])
    krope = proj(_C_KR) * ck_ref[...] + proj(_C_KRR) * sk_ref[...]
    for hd in range(HEADS):
        sl = slice(hd * LANES, (hd + 1) * LANES)
        k_ref[0, :, sl] = (knope[:, sl] + krope).astype(BF16)
    v_ref[0] = _dot_nt(wv_ref[...], ckvn).astype(BF16)
    kb_ref[0] = proj(_C_KB).astype(BF16)
    vb_ref[0] = proj(_C_VB).astype(BF16)

    if with_q:
        cqn = _rms(proj(_C_CQ), gq_ref[...]).astype(BF16)
        q1 = _dot_nt(wq_ref[...], cqn)
        q2 = _dot_nt(wqr_ref[...], cqn)
        cq = cq_ref[...]
        sq = sq_ref[...]
        for hd in range(HEADS):
            sl = slice(hd * LANES, (hd + 1) * LANES)
            q_ref[0, sl, :] = (q1[sl] * cq + q2[sl] * sq).astype(BF16)
        qb_ref[0] = proj(_C_QB).astype(BF16)
        za = proj(_C_ZA)
        sza_ref[0] = (za * jax.nn.sigmoid(za)).astype(BF16)
        zb = proj(_C_ZB)
        szb_ref[0] = (zb * jax.nn.sigmoid(zb)).astype(BF16)
        sga_ref[0] = jax.nn.sigmoid(proj(_C_GA)).astype(BF16)
        sgb_ref[0] = jax.nn.sigmoid(proj(_C_GB)).astype(BF16)


def _proj_call(x, shift, scale, per_batch_mod, norm_g, w_ext, g_ckv, wk, wv, ck, sk,
               q_side=None, tm=256):
    b, s, d = x.shape
    with_q = q_side is not None
    n_ext = w_ext.shape[1]
    mod_map = (lambda i, j: (i, 0, 0)) if per_batch_mod else (lambda i, j: (0, 0, 0))
    const2 = lambda i, j: (0, 0)
    tok3 = lambda i, j: (i, j, 0)
    tab = lambda i, j: (j, 0)

    in_specs = [
        pl.BlockSpec((1, tm, d), tok3),
        pl.BlockSpec((1, 1, d), mod_map),
        pl.BlockSpec((1, 1, d), mod_map),
        pl.BlockSpec((1, d), const2),
        pl.BlockSpec((d, n_ext), const2),
        pl.BlockSpec((1, KV_LORA), const2),
        pl.BlockSpec((KV_LORA, HEADS * LANES), const2),
        pl.BlockSpec((HEADS * MLA_V, KV_LORA), const2),
        pl.BlockSpec((tm, LANES), tab),
        pl.BlockSpec((tm, LANES), tab),
    ]
    args = [x, shift, scale, norm_g, w_ext, g_ckv, wk, wv, ck, sk]
    tok_major = lambda w: (pl.BlockSpec((1, tm, w), tok3), jax.ShapeDtypeStruct((b, s, w), BF16))
    feat_major = lambda w: (pl.BlockSpec((1, w, tm), lambda i, j: (i, 0, j)),
                            jax.ShapeDtypeStruct((b, w, s), BF16))
    outs = [tok_major(HEADS * LANES), feat_major(HEADS * MLA_V), tok_major(HEADS * NA_DIM),
            tok_major(HEADS * NA_DIM)]
    if with_q:
        g_cq, wq, wqr, cq, sq = q_side
        in_specs += [
            pl.BlockSpec((1, Q_LORA), const2),
            pl.BlockSpec((HEADS * LANES, Q_LORA), const2),
            pl.BlockSpec((HEADS * LANES, Q_LORA), const2),
            pl.BlockSpec((LANES, tm), lambda i, j: (0, j)),
            pl.BlockSpec((LANES, tm), lambda i, j: (0, j)),
        ]
        args += [g_cq, wq, wqr, cq, sq]
        outs += [feat_major(HEADS * LANES), tok_major(HEADS * NA_DIM), tok_major(HEADS * MLA_V),
                 tok_major(HEADS * NA_DIM), tok_major(d), tok_major(d)]
    out_specs = [o[0] for o in outs]
    out_shape = [o[1] for o in outs]
    return pl.pallas_call(
        functools.partial(_proj_kernel, with_q),
        grid=(b, s // tm),
        in_specs=in_specs,
        out_specs=out_specs,
        out_shape=out_shape,
        compiler_params=pltpu.CompilerParams(vmem_limit_bytes=VMEM_LIMIT),
        name="in_proj_q" if with_q else "in_proj_ctx",
    )(*args)


BF16_SUBLANES = 16
MLA_TQ = 512
MLA_CHUNK = 256
MLA_LOOKAHEAD = 2


def _mla_kernel(qt_ref, k_ref, kc_ref, vt_ref, vct_ref, o_ref):
    s_len = k_ref.shape[1]
    chunks = [(k_ref, vt_ref, c * MLA_CHUNK, MLA_CHUNK) for c in range(s_len // MLA_CHUNK)]
    chunks.append((kc_ref, vct_ref, 0, kc_ref.shape[1]))
    qts = [qt_ref[0, hh * LANES:(hh + 1) * LANES, :] for hh in range(2)]
    items = [(hh,) + ch for ch in chunks for hh in range(2)]

    def scores(item):
        hh, kr, _, off, n = item
        return _dot(kr[0, off:off + n, hh * LANES:(hh + 1) * LANES], qts[hh])

    m = [None, None]
    acc = [None, None]
    ahead = [scores(it) for it in items[:MLA_LOOKAHEAD]]
    for idx, (hh, _, vr, off, n) in enumerate(items):
        if idx + MLA_LOOKAHEAD < len(items):
            ahead.append(scores(items[idx + MLA_LOOKAHEAD]))
        s = ahead.pop(0)
        mc = jnp.max(s, axis=0, keepdims=True)
        m_new = mc if m[hh] is None else jnp.maximum(m[hh], mc)
        p = jnp.exp2((s - m_new).astype(BF16))
        v_ones = jnp.concatenate([vr[0, hh * MLA_V:(hh + 1) * MLA_V, off:off + n],
                                  jnp.ones((BF16_SUBLANES, n), BF16)], axis=0)
        pv = _dot(v_ones, p)
        acc[hh] = pv if acc[hh] is None else acc[hh] * jnp.exp2(m[hh] - m_new) + pv
        m[hh] = m_new
    outs = [a[:MLA_V] / a[MLA_V:MLA_V + 1] for a in acc]
    o_ref[0] = jnp.concatenate(outs, axis=0).T.astype(BF16)


def _mla_call(qt, k, kc, vt, vct):
    b, _, s = qt.shape
    t = kc.shape[1]
    return pl.pallas_call(
        _mla_kernel,
        grid=(b, HEAD_PAIRS, s // MLA_TQ),
        in_specs=[
            pl.BlockSpec((1, 2 * LANES, MLA_TQ), lambda i, p, j: (i, p, j)),
            pl.BlockSpec((1, s, 2 * LANES), lambda i, p, j: (i, 0, p)),
            pl.BlockSpec((1, t, 2 * LANES), lambda i, p, j: (i, 0, p)),
            pl.BlockSpec((1, 2 * MLA_V, s), lambda i, p, j: (i, p, 0)),
            pl.BlockSpec((1, 2 * MLA_V, t), lambda i, p, j: (i, p, 0)),
        ],
        out_specs=pl.BlockSpec((1, MLA_TQ, LANES), lambda i, p, j: (i, j, p)),
        out_shape=jax.ShapeDtypeStruct((b, s, HEADS * MLA_V), BF16),
        compiler_params=pltpu.CompilerParams(vmem_limit_bytes=VMEM_LIMIT),
        name="mla_attention",
    )(qt, k, kc, vt, vct)


def _na_slab_start(blk):
    return jnp.clip(blk * NA_ROWS_PER_BLOCK - NA_KR // 2, 0, GRID_H - NA_SLAB_ROWS)


def _na_kernel(q_ref, k_ref, v_ref, kc_ref, vc_ref, bias_ref, o_ref):
    blk = pl.program_id(2)
    start = pl.multiple_of(_na_slab_start(blk) * GRID_W, GRID_W)
    ks = k_ref[0, pl.ds(start, NA_SLAB), :]
    vs = v_ref[0, pl.ds(start, NA_SLAB), :]
    kc = kc_ref[0]
    vc = vc_ref[0]
    q = q_ref[0]
    half = lax.broadcasted_iota(jnp.int32, q.shape, 1) // NA_DIM
    out = jnp.zeros(q.shape, F32)
    for hh in range(2):
        qm = jnp.where(half == hh, q, jnp.zeros_like(q))
        s1 = _dot_nt(qm, ks) + bias_ref[0, hh]
        s2 = _dot_nt(qm, kc)
        m = jnp.maximum(jnp.max(s1, axis=-1, keepdims=True), jnp.max(s2, axis=-1, keepdims=True))
        p1 = jnp.exp(s1 - m)
        p2 = jnp.exp(s2 - m)
        l = jnp.sum(p1, axis=-1, keepdims=True) + jnp.sum(p2, axis=-1, keepdims=True)
        o = (_dot(p1.astype(BF16), vs) + _dot(p2.astype(BF16), vc)) / l
        out = jnp.where(half == hh, o, out)
    o_ref[0] = out.astype(BF16)


def _na_variant(blk):
    return jnp.where(blk == 0, 0, jnp.where(blk == NA_BLOCKS - 1, 2, 1))


def _na_call(qb, kb, vb, kcb, vcb, bias):
    b, s, _ = qb.shape
    t = kcb.shape[1]
    return pl.pallas_call(
        _na_kernel,
        grid=(b, HEAD_PAIRS, NA_BLOCKS),
        in_specs=[
            pl.BlockSpec((1, NA_TQ, LANES), lambda i, p, j: (i, j, p)),
            pl.BlockSpec((1, s, LANES), lambda i, p, j: (i, 0, p)),
            pl.BlockSpec((1, s, LANES), lambda i, p, j: (i, 0, p)),
            pl.BlockSpec((1, t, LANES), lambda i, p, j: (i, 0, p)),
            pl.BlockSpec((1, t, LANES), lambda i, p, j: (i, 0, p)),
            pl.BlockSpec((1, 2, NA_TQ, NA_SLAB), lambda i, p, j: (_na_variant(j), p, 0, 0)),
        ],
        out_specs=pl.BlockSpec((1, NA_TQ, LANES), lambda i, p, j: (i, j, p)),
        out_shape=jax.ShapeDtypeStruct((b, s, HEADS * NA_DIM), BF16),
        compiler_params=pltpu.CompilerParams(vmem_limit_bytes=VMEM_LIMIT),
        name="na_attention",
    )(qb, kb, vb, kcb, vcb, bias)


def _na_bias_kernel(rpb_ref, o_ref):
    rp = pltpu.roll(rpb_ref[0], LANES - (NA_KC - 1), axis=1)
    shape = (GRID_W, LANES)
    w = lax.broadcasted_iota(jnp.int32, shape, 0)
    lane = lax.broadcasted_iota(jnp.int32, shape, 1)
    c_start = jnp.clip(w - NA_KC // 2, 0, GRID_W - NA_KC)
    left = lane < GRID_W
    mask_l = jnp.where(left, NA_MASK, 0.0).astype(F32)
    mask_r = jnp.where(left, 0.0, NA_MASK).astype(F32)
    tiles_l, tiles_r = [], []
    for d in range(2 * NA_KR - 1):
        row = jnp.broadcast_to(rp[d:d + 1, :], shape)
        t = pltpu.roll(row, 0, axis=1, stride=1, stride_axis=0)
        t = jnp.where(lane >= c_start, jnp.where(lane < c_start + NA_KC, t, mask_l), mask_l)
        tiles_l.append(t)
        tiles_r.append(pltpu.roll(t, GRID_W, axis=1))
    for n, blk in enumerate((0, 1, NA_BLOCKS - 1)):
        sb = min(max(blk * NA_ROWS_PER_BLOCK - NA_KR // 2, 0), GRID_H - NA_SLAB_ROWS)
        for i in range(NA_ROWS_PER_BLOCK):
            r = blk * NA_ROWS_PER_BLOCK + i
            r_start = min(max(r - NA_KR // 2, 0), GRID_H - NA_KR)

            def pick(j, tiles, masked):
                kr = sb + j
                if r_start <= kr < r_start + NA_KR:
                    return tiles[kr - r + NA_KR - 1]
                return masked

            for jj in range(NA_SLAB_ROWS // 2):
                o_ref[n, 0, i * GRID_W:(i + 1) * GRID_W, jj * LANES:(jj + 1) * LANES] = (
                    pick(2 * jj, tiles_l, mask_l) + pick(2 * jj + 1, tiles_r, mask_r))


def _na_bias_tables(rpb):
    rpb_p = jnp.pad(rpb, ((0, 0), (0, 1), (0, LANES - (2 * NA_KC - 1))))
    return pl.pallas_call(
        _na_bias_kernel,
        grid=(HEADS,),
        in_specs=[pl.BlockSpec((1, 2 * NA_KR, LANES), lambda h: (h, 0, 0))],
        out_specs=pl.BlockSpec((3, 1, NA_TQ, NA_SLAB), lambda h: (0, h, 0, 0)),
        out_shape=jax.ShapeDtypeStruct((3, HEADS, NA_TQ, NA_SLAB), F32),
        name="na_bias_table",
    )(rpb_p)


def _out_kernel(oa_ref, ob_ref, sza_ref, szb_ref, sga_ref, sgb_ref, x_ref, gate_ref,
                woa_ref, wob_ref, wout_ref, fg_ref, o_ref):
    ya = _dot(oa_ref[0] * sza_ref[0], woa_ref[...])
    yb = _dot(ob_ref[0] * szb_ref[0], wob_ref[...])
    mix = sga_ref[0].astype(F32) * ya + sgb_ref[0].astype(F32) * yb
    y = _dot(mix.astype(BF16), wout_ref[...])
    r = x_ref[0] + gate_ref[0] * y
    o_ref[0] = _rms(r, fg_ref[...])


def _out_call(oa, ob, sza, szb, sga, sgb, x, gate, w_oa, w_ob, w_out, final_g, tm=512):
    b, s, d = x.shape
    tok3 = lambda i, j: (i, j, 0)
    const2 = lambda i, j: (0, 0)
    half = HEADS * MLA_V
    return pl.pallas_call(
        _out_kernel,
        grid=(b, s // tm),
        in_specs=[
            pl.BlockSpec((1, tm, half), tok3),
            pl.BlockSpec((1, tm, half), tok3),
            pl.BlockSpec((1, tm, half), tok3),
            pl.BlockSpec((1, tm, half), tok3),
            pl.BlockSpec((1, tm, d), tok3),
            pl.BlockSpec((1, tm, d), tok3),
            pl.BlockSpec((1, tm, d), tok3),
            pl.BlockSpec((1, 1, d), lambda i, j: (i, 0, 0)),
            pl.BlockSpec((half, d), const2),
            pl.BlockSpec((half, d), const2),
            pl.BlockSpec((d, d), const2),
            pl.BlockSpec((1, d), const2),
        ],
        out_specs=pl.BlockSpec((1, tm, d), tok3),
        out_shape=jax.ShapeDtypeStruct((b, s, d), F32),
        compiler_params=pltpu.CompilerParams(vmem_limit_bytes=VMEM_LIMIT),
        name="gated_out",
    )(oa, ob, sza, szb, sga, sgb, x, gate, w_oa, w_ob, w_out, final_g)


def _rope_partner():
    j = np.arange(MLA_ROPE)
    first = (j % 16) < 8
    src = np.where(first, j + 8, j - 8)
    sign = np.where(first, -1.0, 1.0).astype(np.float32)
    return src, sign


def _rope_tables():
    t = jnp.arange(SEQ)
    half = MLA_ROPE // 4
    freqs = ROPE_THETA ** (-jnp.arange(half, dtype=F32) / half)
    ang_r = (t // GRID_W).astype(F32)[:, None] * freqs[None, :]
    ang_c = (t % GRID_W).astype(F32)[:, None] * freqs[None, :]
    cos = jnp.concatenate([jnp.cos(ang_r)] * 2 + [jnp.cos(ang_c)] * 2, axis=-1)
    sin = jnp.concatenate([jnp.sin(ang_r)] * 2 + [jnp.sin(ang_c)] * 2, axis=-1)
    return cos, sin


def _lane_group(nope, rope, n):
    pad = jnp.zeros((n, LANES - MLA_NOPE - MLA_ROPE), F32)
    return jnp.concatenate([nope, rope, pad], axis=-1)


def kernel(x, c, ctx, c_ctx, w_mod, b_mod, norm_g, w_in, g_cq, w_uq, g_ckv, w_ukv, rpb,
           w_oa, w_ob, w_out, final_g):
    b, s, d = x.shape
    depth = w_mod.shape[0]
    assert depth == 1 and s == SEQ and d == D_MODEL
    mla_scale = (MLA_NOPE + MLA_ROPE) ** -0.5 * LOG2_E
    na_scale = NA_DIM ** -0.5
    src, sign = _rope_partner()

    w = w_in[0]
    o = np.cumsum([0, KV_LORA, MLA_ROPE, HEADS * NA_DIM, HEADS * NA_DIM, Q_LORA, HEADS * MLA_V,
                   HEADS * NA_DIM, HEADS * NA_DIM, d, d])
    seg = [w[:, o[n]:o[n + 1]] for n in range(10)]
    w_ckv, w_kr, w_kb, w_vb, w_cq, w_za, w_qb, w_zb, w_ga, w_gb = seg
    z64 = jnp.zeros((d, MLA_NOPE), F32)
    w_ext = jnp.concatenate([
        w_ckv,
        _lane_group(z64, w_kr, d),
        _lane_group(z64, w_kr[:, src] * sign, d),
        w_kb, w_vb, w_cq, w_za, w_qb * na_scale, w_zb, w_ga, w_gb], axis=-1).astype(BF16)

    wkv3 = w_ukv[0].reshape(KV_LORA, HEADS, MLA_NOPE + MLA_V)
    wk = jnp.concatenate([wkv3[:, :, :MLA_NOPE], jnp.zeros((KV_LORA, HEADS, LANES - MLA_NOPE), F32)],
                         axis=-1).reshape(KV_LORA, HEADS * LANES).astype(BF16)
    wv = wkv3[:, :, MLA_NOPE:].reshape(KV_LORA, HEADS * MLA_V).T.astype(BF16)
    wq3 = w_uq[0].reshape(Q_LORA, HEADS, MLA_NOPE + MLA_ROPE)
    zq = jnp.zeros((Q_LORA, HEADS, LANES - MLA_NOPE - MLA_ROPE), F32)
    wq = jnp.concatenate([wq3, zq], axis=-1).reshape(Q_LORA, HEADS * LANES).T.astype(BF16)
    wqr = jnp.concatenate([jnp.zeros((Q_LORA, HEADS, MLA_NOPE), F32),
                           wq3[:, :, MLA_NOPE:][:, :, src] * sign, zq],
                          axis=-1).reshape(Q_LORA, HEADS * LANES).T.astype(BF16)

    cos, sin = _rope_tables()
    ones_s = jnp.ones((s, MLA_NOPE), F32)
    zeros_s = jnp.zeros((s, MLA_NOPE), F32)
    cq_tab = (_lane_group(ones_s, cos, s) * mla_scale).T
    sq_tab = (_lane_group(zeros_s, sin, s) * mla_scale).T
    ck_tab = _lane_group(zeros_s, cos, s)
    sk_tab = _lane_group(zeros_s, sin, s)
    zeros_c = jnp.zeros((CTX_LEN, MLA_NOPE), F32)
    ck_ctx = _lane_group(zeros_c, jnp.ones((CTX_LEN, MLA_ROPE), F32), CTX_LEN)
    sk_ctx = jnp.zeros((CTX_LEN, LANES), F32)

    cs = jnp.concatenate([c, c_ctx[None, :], jnp.zeros((8 - b - 1, d), F32)], axis=0)
    mod = _mod_call(cs, w_mod[0], b_mod[0][None, :])
    shift, scale, gate = (mod[:b, None, n * d:(n + 1) * d] for n in range(3))
    shift_c, scale_c = (mod[b:b + 1, None, n * d:(n + 1) * d] for n in range(2))

    ng = norm_g[0][None, :]
    gkv = g_ckv[0][None, :]
    k, v, kb, vb, q, qb, sza, szb, sga, sgb = _proj_call(
        x, shift, scale, True, ng, w_ext, gkv, wk, wv, ck_tab, sk_tab,
        q_side=(g_cq[0][None, :], wq, wqr, cq_tab, sq_tab))
    kc, vc, kcb, vcb = _proj_call(
        ctx, shift_c, scale_c, False, ng, w_ext[:, :KV_EXT], gkv, wk, wv, ck_ctx, sk_ctx)

    oa = _mla_call(q, k, kc, v, vc)
    ob = _na_call(qb, kb, vb, kcb, vcb, _na_bias_tables(rpb[0]))

    return _out_call(oa, ob, sza, szb, sga, sgb, x, gate,
                     w_oa[0].astype(BF16), w_ob[0].astype(BF16), w_out[0].astype(BF16),
                     final_g[None, :])
```

```python
import functools

import jax
import jax.numpy as jnp
import numpy as np
from jax import lax
from jax.experimental import pallas as pl
from jax.experimental.pallas import tpu as pltpu

D_MODEL = 1024
SEQ = 4096
GRID_W = 64
GRID_H = SEQ // GRID_W
CTX_LEN = 256
HEADS = 8
MLA_NOPE = 64
MLA_ROPE = 32
MLA_V = 64
Q_LORA = 256
KV_LORA = 128
NA_DIM = 64
NA_KR = 8
NA_KC = 16
ROPE_THETA = 10000.0
EPS = 1e-6

LANES = 128
HEAD_PAIRS = HEADS // 2
VMEM_LIMIT = 56 * 1024 * 1024

_C_CKV = (0, 128)
_C_KR = (128, 256)
_C_KRR = (256, 384)
_C_KB = (384, 896)
KV_EXT = 896
_C_CQ = (896, 1152)
_C_ZA = (1152, 1664)
_C_ZB = (1664, 2176)
_C_GA = (2176, 3200)
_C_GB = (3200, 4224)
_R_VB = (0, 512)
_R_QB = (512, 1024)

NA_ROWS_PER_BLOCK = 4
NA_BLOCKS = GRID_H // NA_ROWS_PER_BLOCK
NA_TQ = NA_ROWS_PER_BLOCK * GRID_W
NA_SLAB_ROWS = 12
NA_SLAB = NA_SLAB_ROWS * GRID_W
NA_MASK = -1e30
LOG2_E = 1.4426950408889634

F32 = jnp.float32
BF16 = jnp.bfloat16


def _dot(a, b):
    return jnp.dot(a, b, preferred_element_type=F32)


def _dot_nt(a, b):
    return lax.dot_general(a, b, (((1,), (1,)), ((), ())), preferred_element_type=F32)


def _rms(x, g):
    return x * lax.rsqrt(jnp.mean(x * x, axis=-1, keepdims=True) + EPS) * g


def _mod_kernel(c_ref, w_ref, b_ref, o_ref):
    c = c_ref[...]
    sc = c * jax.nn.sigmoid(c)
    o_ref[...] = _dot(sc.astype(BF16), w_ref[...].astype(BF16)) + b_ref[...]


def _mod_call(cs, w_mod, b_mod):
    n = w_mod.shape[1]
    tn = 1024
    return pl.pallas_call(
        _mod_kernel,
        grid=(n // tn,),
        in_specs=[
            pl.BlockSpec((8, D_MODEL), lambda j: (0, 0)),
            pl.BlockSpec((D_MODEL, tn), lambda j: (0, j)),
            pl.BlockSpec((1, tn), lambda j: (0, j)),
        ],
        out_specs=pl.BlockSpec((8, tn), lambda j: (0, j)),
        out_shape=jax.ShapeDtypeStruct((8, n), F32),
        name="adaln_mod",
    )(cs, w_mod, b_mod)


def _proj_kernel(with_q, x_ref, shift_ref, scale_ref, ng_ref, w_ref, wt_ref, gkv_ref, wk_ref, wv_ref,
                 ck_ref, sk_ref, *rest):
    if with_q:
        (gq_ref, wq_ref, wqr_ref, cq_ref, sq_ref,
         k_ref, v_ref, kb_ref, vb_ref, q_ref, qb_ref, sza_ref, szb_ref, sga_ref, sgb_ref) = rest
    else:
        k_ref, v_ref, kb_ref, vb_ref = rest

    x = x_ref[0]
    h = _rms(x, ng_ref[...]) * (1.0 + scale_ref[0]) + shift_ref[0]
    hb = h.astype(BF16)

    def proj(c):
        return _dot(hb, w_ref[:, c[0]:c[1]])

    def proj_t(r):
        return _dot_nt(wt_ref[r[0]:r[1], :], hb)

    ckvn = _rms(proj(_C_CKV), gkv_ref[...]).astype(BF16)
    knope = _dot(ckvn, wk_ref[...])
    krope = proj(_C_KR) * ck_ref[...] + proj(_C_KRR) * sk_ref[...]
    for hd in range(HEADS):
        sl = slice(hd * LANES, (hd + 1) * LANES)
        k_ref[0, :, sl] = (knope[:, sl] + krope).astype(BF16)
    v_ref[0] = _dot_nt(wv_ref[...], ckvn).astype(BF16)
    kb_ref[0] = proj(_C_KB).astype(BF16)
    vb_ref[0] = proj_t(_R_VB).astype(BF16)

    if with_q:
        cqn = _rms(proj(_C_CQ), gq_ref[...]).astype(BF16)
        q1 = _dot_nt(wq_ref[...], cqn)
        q2 = _dot_nt(wqr_ref[...], cqn)
        cq = cq_ref[...]
        sq = sq_ref[...]
        for hd in range(HEADS):
            sl = slice(hd * LANES, (hd + 1) * LANES)
            q_ref[0, sl, :] = (q1[sl] * cq + q2[sl] * sq).astype(BF16)
        qb_ref[0] = proj_t(_R_QB).astype(BF16)
        za = proj(_C_ZA)
        sza_ref[0] = (za * jax.nn.sigmoid(za)).astype(BF16)
        zb = proj(_C_ZB)
        szb_ref[0] = (zb * jax.nn.sigmoid(zb)).astype(BF16)
        sga_ref[0] = jax.nn.sigmoid(proj(_C_GA)).astype(BF16)
        sgb_ref[0] = jax.nn.sigmoid(proj(_C_GB)).astype(BF16)


def _proj_call(x, shift, scale, per_batch_mod, norm_g, w_ext, wt_ext, g_ckv, wk, wv, ck, sk,
               q_side=None, tm=256):
    b, s, d = x.shape
    with_q = q_side is not None
    n_ext = w_ext.shape[1]
    n_t = wt_ext.shape[0]
    mod_map = (lambda i, j: (i, 0, 0)) if per_batch_mod else (lambda i, j: (0, 0, 0))
    const2 = lambda i, j: (0, 0)
    tok3 = lambda i, j: (i, j, 0)
    tab = lambda i, j: (j, 0)

    in_specs = [
        pl.BlockSpec((1, tm, d), tok3),
        pl.BlockSpec((1, 1, d), mod_map),
        pl.BlockSpec((1, 1, d), mod_map),
        pl.BlockSpec((1, d), const2),
        pl.BlockSpec((d, n_ext), const2),
        pl.BlockSpec((n_t, d), const2),
        pl.BlockSpec((1, KV_LORA), const2),
        pl.BlockSpec((KV_LORA, HEADS * LANES), const2),
        pl.BlockSpec((HEADS * MLA_V, KV_LORA), const2),
        pl.BlockSpec((tm, LANES), tab),
        pl.BlockSpec((tm, LANES), tab),
    ]
    args = [x, shift, scale, norm_g, w_ext, wt_ext, g_ckv, wk, wv, ck, sk]
    tok_major = lambda w: (pl.BlockSpec((1, tm, w), tok3), jax.ShapeDtypeStruct((b, s, w), BF16))
    feat_major = lambda w: (pl.BlockSpec((1, w, tm), lambda i, j: (i, 0, j)),
                            jax.ShapeDtypeStruct((b, w, s), BF16))
    outs = [tok_major(HEADS * LANES), feat_major(HEADS * MLA_V), tok_major(HEADS * NA_DIM),
            feat_major(HEADS * NA_DIM)]
    if with_q:
        g_cq, wq, wqr, cq, sq = q_side
        in_specs += [
            pl.BlockSpec((1, Q_LORA), const2),
            pl.BlockSpec((HEADS * LANES, Q_LORA), const2),
            pl.BlockSpec((HEADS * LANES, Q_LORA), const2),
            pl.BlockSpec((LANES, tm), lambda i, j: (0, j)),
            pl.BlockSpec((LANES, tm), lambda i, j: (0, j)),
        ]
        args += [g_cq, wq, wqr, cq, sq]
        outs += [feat_major(HEADS * LANES), feat_major(HEADS * NA_DIM), tok_major(HEADS * MLA_V),
                 tok_major(HEADS * NA_DIM), tok_major(d), tok_major(d)]
    out_specs = [o[0] for o in outs]
    out_shape = [o[1] for o in outs]
    return pl.pallas_call(
        functools.partial(_proj_kernel, with_q),
        grid=(b, s // tm),
        in_specs=in_specs,
        out_specs=out_specs,
        out_shape=out_shape,
        compiler_params=pltpu.CompilerParams(vmem_limit_bytes=VMEM_LIMIT),
        name="in_proj_q" if with_q else "in_proj_ctx",
    )(*args)


BF16_SUBLANES = 16
MLA_TQ = 512
MLA_CHUNK = 256
MLA_LOOKAHEAD = 2


def _mla_kernel(qt_ref, k_ref, kc_ref, vt_ref, vct_ref, o_ref):
    s_len = k_ref.shape[1]
    chunks = [(k_ref, vt_ref, c * MLA_CHUNK, MLA_CHUNK) for c in range(s_len // MLA_CHUNK)]
    chunks.append((kc_ref, vct_ref, 0, kc_ref.shape[1]))
    qts = [qt_ref[0, hh * LANES:(hh + 1) * LANES, :] for hh in range(2)]
    items = [(hh,) + ch for ch in chunks for hh in range(2)]

    def scores(item):
        hh, kr, _, off, n = item
        return _dot(kr[0, off:off + n, hh * LANES:(hh + 1) * LANES], qts[hh])

    m = [None, None]
    acc = [None, None]
    ahead = [scores(it) for it in items[:MLA_LOOKAHEAD]]
    for idx, (hh, _, vr, off, n) in enumerate(items):
        if idx + MLA_LOOKAHEAD < len(items):
            ahead.append(scores(items[idx + MLA_LOOKAHEAD]))
        s = ahead.pop(0)
        mc = jnp.max(s, axis=0, keepdims=True)
        m_new = mc if m[hh] is None else jnp.maximum(m[hh], mc)
        p = jnp.exp2((s - m_new).astype(BF16))
        v_ones = jnp.concatenate([vr[0, hh * MLA_V:(hh + 1) * MLA_V, off:off + n],
                                  jnp.ones((BF16_SUBLANES, n), BF16)], axis=0)
        pv = _dot(v_ones, p)
        acc[hh] = pv if acc[hh] is None else acc[hh] * jnp.exp2(m[hh] - m_new) + pv
        m[hh] = m_new
    outs = [a[:MLA_V] / a[MLA_V:MLA_V + 1] for a in acc]
    o_ref[0] = jnp.concatenate(outs, axis=0).T.astype(BF16)


def _mla_call(qt, k, kc, vt, vct):
    b, _, s = qt.shape
    t = kc.shape[1]
    return pl.pallas_call(
        _mla_kernel,
        grid=(b, HEAD_PAIRS, s // MLA_TQ),
        in_specs=[
            pl.BlockSpec((1, 2 * LANES, MLA_TQ), lambda i, p, j: (i, p, j)),
            pl.BlockSpec((1, s, 2 * LANES), lambda i, p, j: (i, 0, p)),
            pl.BlockSpec((1, t, 2 * LANES), lambda i, p, j: (i, 0, p)),
            pl.BlockSpec((1, 2 * MLA_V, s), lambda i, p, j: (i, p, 0)),
            pl.BlockSpec((1, 2 * MLA_V, t), lambda i, p, j: (i, p, 0)),
        ],
        out_specs=pl.BlockSpec((1, MLA_TQ, LANES), lambda i, p, j: (i, j, p)),
        out_shape=jax.ShapeDtypeStruct((b, s, HEADS * MLA_V), BF16),
        compiler_params=pltpu.CompilerParams(vmem_limit_bytes=VMEM_LIMIT),
        name="mla_attention",
    )(qt, k, kc, vt, vct)


NA_HEADS_PER_STEP = 4
NA_LOOKAHEAD = 2


def _na_slab_start(blk):
    return jnp.clip(blk * NA_ROWS_PER_BLOCK - NA_KR // 2, 0, GRID_H - NA_SLAB_ROWS)


def _na_variant(blk):
    return jnp.where(blk == 0, 0, jnp.where(blk == NA_BLOCKS - 1, 2, 1))


def _with_ones(v):
    return jnp.concatenate([v, jnp.ones((BF16_SUBLANES, v.shape[1]), v.dtype)], axis=0)


def _na_kernel(qt_ref, k_ref, vt_ref, kc_ref, vct_ref, bias_ref, o_ref):
    blk = pl.program_id(2)
    start = pl.multiple_of(_na_slab_start(blk) * GRID_W, NA_TQ)
    variant = _na_variant(blk)
    row_half = lax.broadcasted_iota(jnp.int32, (LANES, NA_TQ), 0) // NA_DIM

    def scores(h):
        pair, hh = divmod(h, 2)
        cols = slice(pair * LANES, (pair + 1) * LANES)
        qt = qt_ref[0, cols, :]
        qm = jnp.where(row_half == hh, qt, jnp.zeros_like(qt))
        s1 = _dot(k_ref[0, pl.ds(start, NA_SLAB), cols], qm) + bias_ref[variant, h]
        s2 = _dot(kc_ref[0, :, cols], qm)
        return s1, s2

    outs = []
    ahead = [scores(h) for h in range(NA_LOOKAHEAD)]
    for h in range(NA_HEADS_PER_STEP):
        if h + NA_LOOKAHEAD < NA_HEADS_PER_STEP:
            ahead.append(scores(h + NA_LOOKAHEAD))
        s1, s2 = ahead.pop(0)
        m = jnp.maximum(jnp.max(s1, axis=0, keepdims=True), jnp.max(s2, axis=0, keepdims=True))
        p1 = jnp.exp((s1 - m).astype(BF16))
        p2 = jnp.exp((s2 - m).astype(BF16))
        rows = slice(h * NA_DIM, (h + 1) * NA_DIM)
        pv = (_dot(_with_ones(vt_ref[0, rows, pl.ds(start, NA_SLAB)]), p1)
              + _dot(_with_ones(vct_ref[0, rows, :]), p2))
        outs.append(pv[:NA_DIM] / pv[NA_DIM:NA_DIM + 1])
    o_ref[0] = jnp.concatenate(outs, axis=0).T.astype(BF16)


def _na_call(qbt, kb, vbt, kcb, vcbt, bias):
    b, s, _ = kb.shape
    t = kcb.shape[1]
    wide = NA_HEADS_PER_STEP * NA_DIM
    return pl.pallas_call(
        _na_kernel,
        grid=(HEADS // NA_HEADS_PER_STEP, b, NA_BLOCKS),
        in_specs=[
            pl.BlockSpec((1, wide, NA_TQ), lambda g, i, j: (i, g, j)),
            pl.BlockSpec((1, s, wide), lambda g, i, j: (i, 0, g)),
            pl.BlockSpec((1, wide, s), lambda g, i, j: (i, g, 0)),
            pl.BlockSpec((1, t, wide), lambda g, i, j: (i, 0, g)),
            pl.BlockSpec((1, wide, t), lambda g, i, j: (i, g, 0)),
            pl.BlockSpec((3, NA_HEADS_PER_STEP, NA_SLAB, NA_TQ), lambda g, i, j: (0, g, 0, 0)),
        ],
        out_specs=pl.BlockSpec((1, NA_TQ, wide), lambda g, i, j: (i, j, g)),
        out_shape=jax.ShapeDtypeStruct((b, s, HEADS * NA_DIM), BF16),
        compiler_params=pltpu.CompilerParams(vmem_limit_bytes=VMEM_LIMIT),
        name="na_attention",
    )(qbt, kb, vbt, kcb, vcbt, bias)


def _na_bias_kernel(rpb_ref, o_ref):
    rp = pltpu.roll(rpb_ref[0], LANES - (NA_KC - 1), axis=1)
    shape = (GRID_W, LANES)
    wk = lax.broadcasted_iota(jnp.int32, shape, 0)
    lane = lax.broadcasted_iota(jnp.int32, shape, 1)
    c_start = jnp.clip(lane - NA_KC // 2, 0, GRID_W - NA_KC)
    left = lane < GRID_W
    mask_l = jnp.where(left, NA_MASK, 0.0).astype(F32)
    mask_r = jnp.where(left, 0.0, NA_MASK).astype(F32)
    tiles_l, tiles_r = [], []
    for d in range(2 * NA_KR - 1):
        row = jnp.broadcast_to(rp[d:d + 1, :], shape)
        t = pltpu.roll(row, 0, axis=1, stride=1, stride_axis=0)
        t = jnp.where(wk >= c_start, jnp.where(wk < c_start + NA_KC, t, NA_MASK), NA_MASK)
        t = jnp.where(left, t, 0.0)
        tiles_l.append(t)
        tiles_r.append(pltpu.roll(t, GRID_W, axis=1))
    for n, blk in enumerate((0, 1, NA_BLOCKS - 1)):
        sb = min(max(blk * NA_ROWS_PER_BLOCK - NA_KR // 2, 0), GRID_H - NA_SLAB_ROWS)
        for j in range(NA_SLAB_ROWS):
            kr = sb + j

            def pick(i, tiles, masked):
                r = blk * NA_ROWS_PER_BLOCK + i
                r_start = min(max(r - NA_KR // 2, 0), GRID_H - NA_KR)
                if r_start <= kr < r_start + NA_KR:
                    return tiles[kr - r + NA_KR - 1]
                return masked

            for ii in range(NA_ROWS_PER_BLOCK // 2):
                o_ref[n, 0, j * GRID_W:(j + 1) * GRID_W, ii * LANES:(ii + 1) * LANES] = (
                    pick(2 * ii, tiles_l, mask_l) + pick(2 * ii + 1, tiles_r, mask_r))


def _na_bias_tables(rpb):
    rpb_p = jnp.pad(rpb[:, :, ::-1], ((0, 0), (0, 1), (0, LANES - (2 * NA_KC - 1))))
    return pl.pallas_call(
        _na_bias_kernel,
        grid=(HEADS,),
        in_specs=[pl.BlockSpec((1, 2 * NA_KR, LANES), lambda h: (h, 0, 0))],
        out_specs=pl.BlockSpec((3, 1, NA_SLAB, NA_TQ), lambda h: (0, h, 0, 0)),
        out_shape=jax.ShapeDtypeStruct((3, HEADS, NA_SLAB, NA_TQ), F32),
        name="na_bias_table",
    )(rpb_p)


def _out_kernel(oa_ref, ob_ref, sza_ref, szb_ref, sga_ref, sgb_ref, x_ref, gate_ref,
                woa_ref, wob_ref, wout_ref, fg_ref, o_ref):
    ya = _dot(oa_ref[0] * sza_ref[0], woa_ref[...])
    yb = _dot(ob_ref[0] * szb_ref[0], wob_ref[...])
    mix = sga_ref[0].astype(F32) * ya + sgb_ref[0].astype(F32) * yb
    y = _dot(mix.astype(BF16), wout_ref[...])
    r = x_ref[0] + gate_ref[0] * y
    o_ref[0] = _rms(r, fg_ref[...])


def _out_call(oa, ob, sza, szb, sga, sgb, x, gate, w_oa, w_ob, w_out, final_g, tm=512):
    b, s, d = x.shape
    tok3 = lambda i, j: (i, j, 0)
    const2 = lambda i, j: (0, 0)
    half = HEADS * MLA_V
    return pl.pallas_call(
        _out_kernel,
        grid=(b, s // tm),
        in_specs=[
            pl.BlockSpec((1, tm, half), tok3),
            pl.BlockSpec((1, tm, half), tok3),
            pl.BlockSpec((1, tm, half), tok3),
            pl.BlockSpec((1, tm, half), tok3),
            pl.BlockSpec((1, tm, d), tok3),
            pl.BlockSpec((1, tm, d), tok3),
            pl.BlockSpec((1, tm, d), tok3),
            pl.BlockSpec((1, 1, d), lambda i, j: (i, 0, 0)),
            pl.BlockSpec((half, d), const2),
            pl.BlockSpec((half, d), const2),
            pl.BlockSpec((d, d), const2),
            pl.BlockSpec((1, d), const2),
        ],
        out_specs=pl.BlockSpec((1, tm, d), tok3),
        out_shape=jax.ShapeDtypeStruct((b, s, d), F32),
        compiler_params=pltpu.CompilerParams(vmem_limit_bytes=VMEM_LIMIT),
        name="gated_out",
    )(oa, ob, sza, szb, sga, sgb, x, gate, w_oa, w_ob, w_out, final_g)


def _rope_partner():
    j = np.arange(MLA_ROPE)
    first = (j % 16) < 8
    src = np.where(first, j + 8, j - 8)
    sign = np.where(first, -1.0, 1.0).astype(np.float32)
    return src, sign


def _rope_tables():
    t = jnp.arange(SEQ)
    half = MLA_ROPE // 4
    freqs = ROPE_THETA ** (-jnp.arange(half, dtype=F32) / half)
    ang_r = (t // GRID_W).astype(F32)[:, None] * freqs[None, :]
    ang_c = (t % GRID_W).astype(F32)[:, None] * freqs[None, :]
    cos = jnp.concatenate([jnp.cos(ang_r)] * 2 + [jnp.cos(ang_c)] * 2, axis=-1)
    sin = jnp.concatenate([jnp.sin(ang_r)] * 2 + [jnp.sin(ang_c)] * 2, axis=-1)
    return cos, sin


def _lane_group(nope, rope, n):
    pad = jnp.zeros((n, LANES - MLA_NOPE - MLA_ROPE), F32)
    return jnp.concatenate([nope, rope, pad], axis=-1)


def kernel(x, c, ctx, c_ctx, w_mod, b_mod, norm_g, w_in, g_cq, w_uq, g_ckv, w_ukv, rpb,
           w_oa, w_ob, w_out, final_g):
    b, s, d = x.shape
    depth = w_mod.shape[0]
    assert depth == 1 and s == SEQ and d == D_MODEL
    mla_scale = (MLA_NOPE + MLA_ROPE) ** -0.5 * LOG2_E
    na_scale = NA_DIM ** -0.5
    src, sign = _rope_partner()

    w = w_in[0]
    o = np.cumsum([0, KV_LORA, MLA_ROPE, HEADS * NA_DIM, HEADS * NA_DIM, Q_LORA, HEADS * MLA_V,
                   HEADS * NA_DIM, HEADS * NA_DIM, d, d])
    seg = [w[:, o[n]:o[n + 1]] for n in range(10)]
    w_ckv, w_kr, w_kb, w_vb, w_cq, w_za, w_qb, w_zb, w_ga, w_gb = seg
    z64 = jnp.zeros((d, MLA_NOPE), F32)
    w_ext = jnp.concatenate([
        w_ckv,
        _lane_group(z64, w_kr, d),
        _lane_group(z64, w_kr[:, src] * sign, d),
        w_kb, w_cq, w_za, w_zb, w_ga, w_gb], axis=-1).astype(BF16)
    wt_ext = jnp.concatenate([w_vb, w_qb * na_scale], axis=-1).T.astype(BF16)

    wkv3 = w_ukv[0].reshape(KV_LORA, HEADS, MLA_NOPE + MLA_V)
    wk = jnp.concatenate([wkv3[:, :, :MLA_NOPE], jnp.zeros((KV_LORA, HEADS, LANES - MLA_NOPE), F32)],
                         axis=-1).reshape(KV_LORA, HEADS * LANES).astype(BF16)
    wv = wkv3[:, :, MLA_NOPE:].reshape(KV_LORA, HEADS * MLA_V).T.astype(BF16)
    wq3 = w_uq[0].reshape(Q_LORA, HEADS, MLA_NOPE + MLA_ROPE)
    zq = jnp.zeros((Q_LORA, HEADS, LANES - MLA_NOPE - MLA_ROPE), F32)
    wq = jnp.concatenate([wq3, zq], axis=-1).reshape(Q_LORA, HEADS * LANES).T.astype(BF16)
    wqr = jnp.concatenate([jnp.zeros((Q_LORA, HEADS, MLA_NOPE), F32),
                           wq3[:, :, MLA_NOPE:][:, :, src] * sign, zq],
                          axis=-1).reshape(Q_LORA, HEADS * LANES).T.astype(BF16)

    cos, sin = _rope_tables()
    ones_s = jnp.ones((s, MLA_NOPE), F32)
    zeros_s = jnp.zeros((s, MLA_NOPE), F32)
    cq_tab = (_lane_group(ones_s, cos, s) * mla_scale).T
    sq_tab = (_lane_group(zeros_s, sin, s) * mla_scale).T
    ck_tab = _lane_group(zeros_s, cos, s)
    sk_tab = _lane_group(zeros_s, sin, s)
    zeros_c = jnp.zeros((CTX_LEN, MLA_NOPE), F32)
    ck_ctx = _lane_group(zeros_c, jnp.ones((CTX_LEN, MLA_ROPE), F32), CTX_LEN)
    sk_ctx = jnp.zeros((CTX_LEN, LANES), F32)

    cs = jnp.concatenate([c, c_ctx[None, :], jnp.zeros((8 - b - 1, d), F32)], axis=0)
    mod = _mod_call(cs, w_mod[0], b_mod[0][None, :])
    shift, scale, gate = (mod[:b, None, n * d:(n + 1) * d] for n in range(3))
    shift_c, scale_c = (mod[b:b + 1, None, n * d:(n + 1) * d] for n in range(2))

    ng = norm_g[0][None, :]
    gkv = g_ckv[0][None, :]
    k, v, kb, vb, q, qb, sza, szb, sga, sgb = _proj_call(
        x, shift, scale, True, ng, w_ext, wt_ext, gkv, wk, wv, ck_tab, sk_tab,
        q_side=(g_cq[0][None, :], wq, wqr, cq_tab, sq_tab))
    kc, vc, kcb, vcb = _proj_call(
        ctx, shift_c, scale_c, False, ng, w_ext[:, :KV_EXT], wt_ext[:_R_VB[1]], gkv, wk, wv,
        ck_ctx, sk_ctx)

    oa = _mla_call(q, k, kc, v, vc)
    ob = _na_call(qb, kb, vb, kcb, vcb, _na_bias_tables(rpb[0]))

    return _out_call(oa, ob, sza, szb, sga, sgb, x, gate,
                     w_oa[0].astype(BF16), w_ob[0].astype(BF16), w_out[0].astype(BF16),
                     final_g[None, :])
```

```python
import functools

import jax
import jax.numpy as jnp
import numpy as np
from jax import lax
from jax.experimental import pallas as pl
from jax.experimental.pallas import tpu as pltpu

D_MODEL = 1024
SEQ = 4096
GRID_W = 64
GRID_H = SEQ // GRID_W
CTX_LEN = 256
HEADS = 8
MLA_NOPE = 64
MLA_ROPE = 32
MLA_V = 64
Q_LORA = 256
KV_LORA = 128
NA_DIM = 64
NA_KR = 8
NA_KC = 16
ROPE_THETA = 10000.0
EPS = 1e-6

LANES = 128
HEAD_PAIRS = HEADS // 2
VMEM_LIMIT = 56 * 1024 * 1024

_C_CKV = (0, 128)
_C_KR = (128, 256)
_C_KRR = (256, 384)
_C_KB = (384, 896)
KV_EXT = 896
_C_CQ = (896, 1152)
_C_ZA = (1152, 1664)
_C_ZB = (1664, 2176)
_C_GA = (2176, 3200)
_C_GB = (3200, 4224)
_R_VB = (0, 512)
_R_QB = (512, 1024)

NA_ROWS_PER_BLOCK = 4
NA_BLOCKS = GRID_H // NA_ROWS_PER_BLOCK
NA_TQ = NA_ROWS_PER_BLOCK * GRID_W
NA_SLAB_ROWS = 12
NA_SLAB = NA_SLAB_ROWS * GRID_W
NA_MASK = -1e30
LOG2_E = 1.4426950408889634

F32 = jnp.float32
BF16 = jnp.bfloat16


def _dot(a, b):
    return jnp.dot(a, b, preferred_element_type=F32)


def _dot_nt(a, b):
    return lax.dot_general(a, b, (((1,), (1,)), ((), ())), preferred_element_type=F32)


def _rms(x, g):
    return x * lax.rsqrt(jnp.mean(x * x, axis=-1, keepdims=True) + EPS) * g


def _mod_kernel(c_ref, w_ref, b_ref, o_ref):
    c = c_ref[...]
    sc = c * jax.nn.sigmoid(c)
    o_ref[...] = _dot(sc.astype(BF16), w_ref[...].astype(BF16)) + b_ref[...]


def _mod_call(cs, w_mod, b_mod):
    n = w_mod.shape[1]
    tn = 1024
    return pl.pallas_call(
        _mod_kernel,
        grid=(n // tn,),
        in_specs=[
            pl.BlockSpec((8, D_MODEL), lambda j: (0, 0)),
            pl.BlockSpec((D_MODEL, tn), lambda j: (0, j)),
            pl.BlockSpec((1, tn), lambda j: (0, j)),
        ],
        out_specs=pl.BlockSpec((8, tn), lambda j: (0, j)),
        out_shape=jax.ShapeDtypeStruct((8, n), F32),
        name="adaln_mod",
    )(cs, w_mod, b_mod)


def _proj_kernel(with_q, x_ref, shift_ref, scale_ref, ng_ref, w_ref, wt_ref, gkv_ref, wk_ref, wv_ref,
                 ck_ref, sk_ref, *rest):
    if with_q:
        (gq_ref, wq_ref, wqr_ref, cq_ref, sq_ref,
         k_ref, v_ref, kb_ref, vb_ref, q_ref, qb_ref, sza_ref, szb_ref, sga_ref, sgb_ref) = rest
    else:
        k_ref, v_ref, kb_ref, vb_ref = rest

    x = x_ref[0]
    h = _rms(x, ng_ref[...]) * (1.0 + scale_ref[0]) + shift_ref[0]
    hb = h.astype(BF16)

    def proj(c):
        return _dot(hb, w_ref[:, c[0]:c[1]])

    def proj_t(r):
        return _dot_nt(wt_ref[r[0]:r[1], :], hb)

    ckvn = _rms(proj(_C_CKV), gkv_ref[...]).astype(BF16)
    if with_q:
        cqn = _rms(proj(_C_CQ), gq_ref[...]).astype(BF16)
    krope = proj(_C_KR) * ck_ref[...] + proj(_C_KRR) * sk_ref[...]
    kb_ref[0] = proj(_C_KB).astype(BF16)
    vb_ref[0] = proj_t(_R_VB).astype(BF16)
    if with_q:
        qb_ref[0] = proj_t(_R_QB).astype(BF16)
        za = proj(_C_ZA)
        sza_ref[0] = (za * jax.nn.sigmoid(za)).astype(BF16)
        zb = proj(_C_ZB)
        szb_ref[0] = (zb * jax.nn.sigmoid(zb)).astype(BF16)
        sga_ref[0] = jax.nn.sigmoid(proj(_C_GA)).astype(BF16)
        sgb_ref[0] = jax.nn.sigmoid(proj(_C_GB)).astype(BF16)

    knope = _dot(ckvn, wk_ref[...])
    for hd in range(HEADS):
        sl = slice(hd * LANES, (hd + 1) * LANES)
        k_ref[0, :, sl] = (knope[:, sl] + krope).astype(BF16)
    v_ref[0] = _dot_nt(wv_ref[...], ckvn).astype(BF16)
    if with_q:
        q1 = _dot_nt(wq_ref[...], cqn)
        q2 = _dot_nt(wqr_ref[...], cqn)
        cq = cq_ref[...]
        sq = sq_ref[...]
        for hd in range(HEADS):
            sl = slice(hd * LANES, (hd + 1) * LANES)
            q_ref[0, sl, :] = (q1[sl] * cq + q2[sl] * sq).astype(BF16)


def _proj_call(x, shift, scale, per_batch_mod, norm_g, w_ext, wt_ext, g_ckv, wk, wv, ck, sk,
               q_side=None, tm=256):
    b, s, d = x.shape
    tm = min(tm, s)
    with_q = q_side is not None
    n_ext = w_ext.shape[1]
    n_t = wt_ext.shape[0]
    mod_map = (lambda i, j: (i, 0, 0)) if per_batch_mod else (lambda i, j: (0, 0, 0))
    const2 = lambda i, j: (0, 0)
    tok3 = lambda i, j: (i, j, 0)
    tab = lambda i, j: (j, 0)

    in_specs = [
        pl.BlockSpec((1, tm, d), tok3),
        pl.BlockSpec((1, 1, d), mod_map),
        pl.BlockSpec((1, 1, d), mod_map),
        pl.BlockSpec((1, d), const2),
        pl.BlockSpec((d, n_ext), const2),
        pl.BlockSpec((n_t, d), const2),
        pl.BlockSpec((1, KV_LORA), const2),
        pl.BlockSpec((KV_LORA, HEADS * LANES), const2),
        pl.BlockSpec((HEADS * MLA_V, KV_LORA), const2),
        pl.BlockSpec((tm, LANES), tab),
        pl.BlockSpec((tm, LANES), tab),
    ]
    args = [x, shift, scale, norm_g, w_ext, wt_ext, g_ckv, wk, wv, ck, sk]
    tok_major = lambda w: (pl.BlockSpec((1, tm, w), tok3), jax.ShapeDtypeStruct((b, s, w), BF16))
    feat_major = lambda w: (pl.BlockSpec((1, w, tm), lambda i, j: (i, 0, j)),
                            jax.ShapeDtypeStruct((b, w, s), BF16))
    outs = [tok_major(HEADS * LANES), feat_major(HEADS * MLA_V), tok_major(HEADS * NA_DIM),
            feat_major(HEADS * NA_DIM)]
    if with_q:
        g_cq, wq, wqr, cq, sq = q_side
        in_specs += [
            pl.BlockSpec((1, Q_LORA), const2),
            pl.BlockSpec((HEADS * LANES, Q_LORA), const2),
            pl.BlockSpec((HEADS * LANES, Q_LORA), const2),
            pl.BlockSpec((LANES, tm), lambda i, j: (0, j)),
            pl.BlockSpec((LANES, tm), lambda i, j: (0, j)),
        ]
        args += [g_cq, wq, wqr, cq, sq]
        outs += [feat_major(HEADS * LANES), feat_major(HEADS * NA_DIM), tok_major(HEADS * MLA_V),
                 tok_major(HEADS * NA_DIM), tok_major(d), tok_major(d)]
    out_specs = [o[0] for o in outs]
    out_shape = [o[1] for o in outs]
    return pl.pallas_call(
        functools.partial(_proj_kernel, with_q),
        grid=(b, s // tm),
        in_specs=in_specs,
        out_specs=out_specs,
        out_shape=out_shape,
        compiler_params=pltpu.CompilerParams(vmem_limit_bytes=VMEM_LIMIT),
        name="in_proj_q" if with_q else "in_proj_ctx",
    )(*args)


BF16_SUBLANES = 16
MLA_TQ = 512
MLA_CHUNK = 256
MLA_LOOKAHEAD = 2


def _mla_kernel(qt_ref, k_ref, kc_ref, vt_ref, vct_ref, o_ref):
    s_len = k_ref.shape[1]
    chunks = [(k_ref, vt_ref, c * MLA_CHUNK, MLA_CHUNK) for c in range(s_len // MLA_CHUNK)]
    chunks.append((kc_ref, vct_ref, 0, kc_ref.shape[1]))
    qts = [qt_ref[0, hh * LANES:(hh + 1) * LANES, :] for hh in range(2)]
    items = [(hh,) + ch for ch in chunks for hh in range(2)]

    def scores(item):
        hh, kr, _, off, n = item
        return _dot(kr[0, off:off + n, hh * LANES:(hh + 1) * LANES], qts[hh])

    m = [None, None]
    acc = [None, None]
    ahead = [scores(it) for it in items[:MLA_LOOKAHEAD]]
    for idx, (hh, _, vr, off, n) in enumerate(items):
        if idx + MLA_LOOKAHEAD < len(items):
            ahead.append(scores(items[idx + MLA_LOOKAHEAD]))
        s = ahead.pop(0)
        mc = jnp.max(s, axis=0, keepdims=True)
        m_new = mc if m[hh] is None else jnp.maximum(m[hh], mc)
        p = jnp.exp2((s - m_new).astype(BF16))
        v_ones = jnp.concatenate([vr[0, hh * MLA_V:(hh + 1) * MLA_V, off:off + n],
                                  jnp.ones((BF16_SUBLANES, n), BF16)], axis=0)
        pv = _dot(v_ones, p)
        acc[hh] = pv if acc[hh] is None else acc[hh] * jnp.exp2(m[hh] - m_new) + pv
        m[hh] = m_new
    outs = [a[:MLA_V] / a[MLA_V:MLA_V + 1] for a in acc]
    o_ref[0] = jnp.concatenate(outs, axis=0).T.astype(BF16)


def _mla_call(qt, k, kc, vt, vct):
    b, _, s = qt.shape
    t = kc.shape[1]
    return pl.pallas_call(
        _mla_kernel,
        grid=(b, HEAD_PAIRS, s // MLA_TQ),
        in_specs=[
            pl.BlockSpec((1, 2 * LANES, MLA_TQ), lambda i, p, j: (i, p, j)),
            pl.BlockSpec((1, s, 2 * LANES), lambda i, p, j: (i, 0, p)),
            pl.BlockSpec((1, t, 2 * LANES), lambda i, p, j: (i, 0, p)),
            pl.BlockSpec((1, 2 * MLA_V, s), lambda i, p, j: (i, p, 0)),
            pl.BlockSpec((1, 2 * MLA_V, t), lambda i, p, j: (i, p, 0)),
        ],
        out_specs=pl.BlockSpec((1, MLA_TQ, LANES), lambda i, p, j: (i, j, p)),
        out_shape=jax.ShapeDtypeStruct((b, s, HEADS * MLA_V), BF16),
        compiler_params=pltpu.CompilerParams(vmem_limit_bytes=VMEM_LIMIT),
        name="mla_attention",
    )(qt, k, kc, vt, vct)


NA_HEADS_PER_STEP = 8
NA_LOOKAHEAD = 2


def _na_slab_start(blk):
    return jnp.clip(blk * NA_ROWS_PER_BLOCK - NA_KR // 2, 0, GRID_H - NA_SLAB_ROWS)


def _na_variant(blk):
    return jnp.where(blk == 0, 0, jnp.where(blk == NA_BLOCKS - 1, 2, 1))


def _with_ones(v):
    return jnp.concatenate([v, jnp.ones((BF16_SUBLANES, v.shape[1]), v.dtype)], axis=0)


def _na_kernel(qt_ref, k_ref, vt_ref, kc_ref, vct_ref, bias_ref, o_ref):
    blk = pl.program_id(2)
    start = pl.multiple_of(_na_slab_start(blk) * GRID_W, NA_TQ)
    row_half = lax.broadcasted_iota(jnp.int32, (LANES, NA_TQ), 0) // NA_DIM

    def scores(h):
        pair, hh = divmod(h, 2)
        cols = slice(pair * LANES, (pair + 1) * LANES)
        qt = qt_ref[0, cols, :]
        qm = jnp.where(row_half == hh, qt, jnp.zeros_like(qt))
        s1 = _dot(k_ref[0, pl.ds(start, NA_SLAB), cols], qm) + bias_ref[0, h]
        s2 = _dot(kc_ref[0, :, cols], qm)
        return s1, s2

    outs = []
    ahead = [scores(h) for h in range(NA_LOOKAHEAD)]
    for h in range(NA_HEADS_PER_STEP):
        if h + NA_LOOKAHEAD < NA_HEADS_PER_STEP:
            ahead.append(scores(h + NA_LOOKAHEAD))
        s1, s2 = ahead.pop(0)
        m = jnp.maximum(jnp.max(s1, axis=0, keepdims=True), jnp.max(s2, axis=0, keepdims=True))
        p1 = jnp.exp((s1 - m).astype(BF16))
        p2 = jnp.exp((s2 - m).astype(BF16))
        rows = slice(h * NA_DIM, (h + 1) * NA_DIM)
        pv = (_dot(_with_ones(vt_ref[0, rows, pl.ds(start, NA_SLAB)]), p1)
              + _dot(_with_ones(vct_ref[0, rows, :]), p2))
        outs.append(pv[:NA_DIM] / pv[NA_DIM:NA_DIM + 1])
    o_ref[0] = jnp.concatenate(outs, axis=0).T.astype(BF16)


def _na_call(qbt, kb, vbt, kcb, vcbt, bias):
    b, s, _ = kb.shape
    t = kcb.shape[1]
    wide = NA_HEADS_PER_STEP * NA_DIM
    return pl.pallas_call(
        _na_kernel,
        grid=(HEADS // NA_HEADS_PER_STEP, b, NA_BLOCKS),
        in_specs=[
            pl.BlockSpec((1, wide, NA_TQ), lambda g, i, j: (i, g, j)),
            pl.BlockSpec((1, s, wide), lambda g, i, j: (i, 0, g)),
            pl.BlockSpec((1, wide, s), lambda g, i, j: (i, g, 0)),
            pl.BlockSpec((1, t, wide), lambda g, i, j: (i, 0, g)),
            pl.BlockSpec((1, wide, t), lambda g, i, j: (i, g, 0)),
            pl.BlockSpec((1, NA_HEADS_PER_STEP, NA_SLAB, NA_TQ),
                         lambda g, i, j: (_na_variant(j), g, 0, 0)),
        ],
        out_specs=pl.BlockSpec((1, NA_TQ, wide), lambda g, i, j: (i, j, g)),
        out_shape=jax.ShapeDtypeStruct((b, s, HEADS * NA_DIM), BF16),
        compiler_params=pltpu.CompilerParams(vmem_limit_bytes=VMEM_LIMIT),
        name="na_attention",
    )(qbt, kb, vbt, kcb, vcbt, bias)


def _na_bias_kernel(rpb_ref, o_ref):
    rp = pltpu.roll(rpb_ref[0], LANES - (NA_KC - 1), axis=1)
    shape = (GRID_W, LANES)
    wk = lax.broadcasted_iota(jnp.int32, shape, 0)
    lane = lax.broadcasted_iota(jnp.int32, shape, 1)
    c_start = jnp.clip(lane - NA_KC // 2, 0, GRID_W - NA_KC)
    left = lane < GRID_W
    mask_l = jnp.where(left, NA_MASK, 0.0).astype(F32)
    mask_r = jnp.where(left, 0.0, NA_MASK).astype(F32)
    tiles_l, tiles_r = [], []
    for d in range(2 * NA_KR - 1):
        row = jnp.broadcast_to(rp[d:d + 1, :], shape)
        t = pltpu.roll(row, 0, axis=1, stride=1, stride_axis=0)
        t = jnp.where(wk >= c_start, jnp.where(wk < c_start + NA_KC, t, NA_MASK), NA_MASK)
        t = jnp.where(left, t, 0.0)
        tiles_l.append(t)
        tiles_r.append(pltpu.roll(t, GRID_W, axis=1))
    for n, blk in enumerate((0, 1, NA_BLOCKS - 1)):
        sb = min(max(blk * NA_ROWS_PER_BLOCK - NA_KR // 2, 0), GRID_H - NA_SLAB_ROWS)
        for j in range(NA_SLAB_ROWS):
            kr = sb + j

            def pick(i, tiles, masked):
                r = blk * NA_ROWS_PER_BLOCK + i
                r_start = min(max(r - NA_KR // 2, 0), GRID_H - NA_KR)
                if r_start <= kr < r_start + NA_KR:
                    return tiles[kr - r + NA_KR - 1]
                return masked

            for ii in range(NA_ROWS_PER_BLOCK // 2):
                o_ref[n, 0, j * GRID_W:(j + 1) * GRID_W, ii * LANES:(ii + 1) * LANES] = (
                    pick(2 * ii, tiles_l, mask_l) + pick(2 * ii + 1, tiles_r, mask_r))


def _na_bias_tables(rpb):
    rpb_p = jnp.pad(rpb[:, :, ::-1], ((0, 0), (0, 1), (0, LANES - (2 * NA_KC - 1))))
    return pl.pallas_call(
        _na_bias_kernel,
        grid=(HEADS,),
        in_specs=[pl.BlockSpec((1, 2 * NA_KR, LANES), lambda h: (h, 0, 0))],
        out_specs=pl.BlockSpec((3, 1, NA_SLAB, NA_TQ), lambda h: (0, h, 0, 0)),
        out_shape=jax.ShapeDtypeStruct((3, HEADS, NA_SLAB, NA_TQ), F32),
        name="na_bias_table",
    )(rpb_p)


def _out_kernel(oa_ref, ob_ref, sza_ref, szb_ref, sga_ref, sgb_ref, x_ref, gate_ref,
                woa_ref, wob_ref, wout_ref, fg_ref, o_ref):
    subs = [slice(i, i + OUT_SUB) for i in range(0, o_ref.shape[1], OUT_SUB)]

    def branches(sl):
        ya = _dot(oa_ref[0, sl] * sza_ref[0, sl], woa_ref[...])
        yb = _dot(ob_ref[0, sl] * szb_ref[0, sl], wob_ref[...])
        return (sga_ref[0, sl].astype(F32) * ya + sgb_ref[0, sl].astype(F32) * yb).astype(BF16)

    mixes = [branches(subs[0])]
    for i, sl in enumerate(subs):
        if i + 1 < len(subs):
            mixes.append(branches(subs[i + 1]))
        y = _dot(mixes[i], wout_ref[...])
        r = x_ref[0, sl] + gate_ref[0] * y
        o_ref[0, sl] = _rms(r, fg_ref[...])


OUT_SUB = 256


def _out_call(oa, ob, sza, szb, sga, sgb, x, gate, w_oa, w_ob, w_out, final_g, tm=1024):
    b, s, d = x.shape
    tok3 = lambda i, j: (i, j, 0)
    const2 = lambda i, j: (0, 0)
    half = HEADS * MLA_V
    return pl.pallas_call(
        _out_kernel,
        grid=(b, s // tm),
        in_specs=[
            pl.BlockSpec((1, tm, half), tok3),
            pl.BlockSpec((1, tm, half), tok3),
            pl.BlockSpec((1, tm, half), tok3),
            pl.BlockSpec((1, tm, half), tok3),
            pl.BlockSpec((1, tm, d), tok3),
            pl.BlockSpec((1, tm, d), tok3),
            pl.BlockSpec((1, tm, d), tok3),
            pl.BlockSpec((1, 1, d), lambda i, j: (i, 0, 0)),
            pl.BlockSpec((half, d), const2),
            pl.BlockSpec((half, d), const2),
            pl.BlockSpec((d, d), const2),
            pl.BlockSpec((1, d), const2),
        ],
        out_specs=pl.BlockSpec((1, tm, d), tok3),
        out_shape=jax.ShapeDtypeStruct((b, s, d), F32),
        compiler_params=pltpu.CompilerParams(vmem_limit_bytes=VMEM_LIMIT),
        name="gated_out",
    )(oa, ob, sza, szb, sga, sgb, x, gate, w_oa, w_ob, w_out, final_g)


def _rope_partner():
    j = np.arange(MLA_ROPE)
    first = (j % 16) < 8
    src = np.where(first, j + 8, j - 8)
    sign = np.where(first, -1.0, 1.0).astype(np.float32)
    return src, sign


def _rope_tables():
    t = np.arange(SEQ)
    half = MLA_ROPE // 4
    freqs = (ROPE_THETA ** (-np.arange(half, dtype=np.float32) / half)).astype(np.float32)
    ang_r = (t // GRID_W).astype(np.float32)[:, None] * freqs[None, :]
    ang_c = (t % GRID_W).astype(np.float32)[:, None] * freqs[None, :]
    cos = np.concatenate([np.cos(ang_r)] * 2 + [np.cos(ang_c)] * 2, axis=-1).astype(np.float32)
    sin = np.concatenate([np.sin(ang_r)] * 2 + [np.sin(ang_c)] * 2, axis=-1).astype(np.float32)
    return cos, sin


def _np_lane_group(nope, rope):
    pad = np.zeros((rope.shape[0], LANES - MLA_NOPE - MLA_ROPE), np.float32)
    return np.concatenate([nope, rope, pad], axis=-1).astype(np.float32)


def _lane_group(nope, rope, n):
    pad = jnp.zeros((n, LANES - MLA_NOPE - MLA_ROPE), F32)
    return jnp.concatenate([nope, rope, pad], axis=-1)


def kernel(x, c, ctx, c_ctx, w_mod, b_mod, norm_g, w_in, g_cq, w_uq, g_ckv, w_ukv, rpb,
           w_oa, w_ob, w_out, final_g):
    b, s, d = x.shape
    depth = w_mod.shape[0]
    assert depth == 1 and s == SEQ and d == D_MODEL
    mla_scale = (MLA_NOPE + MLA_ROPE) ** -0.5 * LOG2_E
    na_scale = NA_DIM ** -0.5
    src, sign = _rope_partner()

    w = w_in[0]
    o = np.cumsum([0, KV_LORA, MLA_ROPE, HEADS * NA_DIM, HEADS * NA_DIM, Q_LORA, HEADS * MLA_V,
                   HEADS * NA_DIM, HEADS * NA_DIM, d, d])
    seg = [w[:, o[n]:o[n + 1]] for n in range(10)]
    w_ckv, w_kr, w_kb, w_vb, w_cq, w_za, w_qb, w_zb, w_ga, w_gb = seg
    z64 = jnp.zeros((d, MLA_NOPE), F32)
    w_ext = jnp.concatenate([
        w_ckv,
        _lane_group(z64, w_kr, d),
        _lane_group(z64, w_kr[:, src] * sign, d),
        w_kb, w_cq, w_za, w_zb, w_ga, w_gb], axis=-1).astype(BF16)
    wt_ext = jnp.concatenate([w_vb, w_qb * na_scale], axis=-1).T.astype(BF16)

    wkv3 = w_ukv[0].reshape(KV_LORA, HEADS, MLA_NOPE + MLA_V)
    wk = jnp.concatenate([wkv3[:, :, :MLA_NOPE], jnp.zeros((KV_LORA, HEADS, LANES - MLA_NOPE), F32)],
                         axis=-1).reshape(KV_LORA, HEADS * LANES).astype(BF16)
    wv = wkv3[:, :, MLA_NOPE:].reshape(KV_LORA, HEADS * MLA_V).T.astype(BF16)
    wq3 = w_uq[0].reshape(Q_LORA, HEADS, MLA_NOPE + MLA_ROPE)
    zq = jnp.zeros((Q_LORA, HEADS, LANES - MLA_NOPE - MLA_ROPE), F32)
    wq = jnp.concatenate([wq3, zq], axis=-1).reshape(Q_LORA, HEADS * LANES).T.astype(BF16)
    wqr = jnp.concatenate([jnp.zeros((Q_LORA, HEADS, MLA_NOPE), F32),
                           wq3[:, :, MLA_NOPE:][:, :, src] * sign, zq],
                          axis=-1).reshape(Q_LORA, HEADS * LANES).T.astype(BF16)

    cos, sin = _rope_tables()
    ones_s = np.ones((s, MLA_NOPE), np.float32)
    zeros_s = np.zeros((s, MLA_NOPE), np.float32)
    cq_tab = jnp.asarray(np.ascontiguousarray((_np_lane_group(ones_s, cos) * np.float32(mla_scale)).T))
    sq_tab = jnp.asarray(np.ascontiguousarray((_np_lane_group(zeros_s, sin) * np.float32(mla_scale)).T))
    ck_tab = jnp.asarray(_np_lane_group(zeros_s, cos))
    sk_tab = jnp.asarray(_np_lane_group(zeros_s, sin))
    zeros_c = np.zeros((CTX_LEN, MLA_NOPE), np.float32)
    ck_ctx = jnp.asarray(_np_lane_group(zeros_c, np.ones((CTX_LEN, MLA_ROPE), np.float32)))
    sk_ctx = jnp.zeros((CTX_LEN, LANES), F32)

    cs = jnp.concatenate([c, c_ctx[None, :], jnp.zeros((8 - b - 1, d), F32)], axis=0)
    mod = _mod_call(cs, w_mod[0], b_mod[0][None, :])
    shift, scale, gate = (mod[:b, None, n * d:(n + 1) * d] for n in range(3))
    shift_c, scale_c = (mod[b:b + 1, None, n * d:(n + 1) * d] for n in range(2))

    ng = norm_g[0][None, :]
    gkv = g_ckv[0][None, :]
    k, v, kb, vb, q, qb, sza, szb, sga, sgb = _proj_call(
        x, shift, scale, True, ng, w_ext, wt_ext, gkv, wk, wv, ck_tab, sk_tab,
        q_side=(g_cq[0][None, :], wq, wqr, cq_tab, sq_tab))
    kc, vc, kcb, vcb = _proj_call(
        ctx, shift_c, scale_c, False, ng, w_ext[:, :KV_EXT], wt_ext[:_R_VB[1]], gkv, wk, wv,
        ck_ctx, sk_ctx)

    oa = _mla_call(q, k, kc, v, vc)
    ob = _na_call(qb, kb, vb, kcb, vcb, _na_bias_tables(rpb[0]))

    return _out_call(oa, ob, sza, szb, sga, sgb, x, gate,
                     w_oa[0].astype(BF16), w_ob[0].astype(BF16), w_out[0].astype(BF16),
                     final_g[None, :])
```

```python
import functools

import jax
import jax.numpy as jnp
import numpy as np
from jax import lax
from jax.experimental import pallas as pl
from jax.experimental.pallas import tpu as pltpu

D_MODEL = 1024
SEQ = 4096
GRID_W = 64
GRID_H = SEQ // GRID_W
CTX_LEN = 256
HEADS = 8
MLA_NOPE = 64
MLA_ROPE = 32
MLA_V = 64
Q_LORA = 256
KV_LORA = 128
NA_DIM = 64
NA_KR = 8
NA_KC = 16
ROPE_THETA = 10000.0
EPS = 1e-6

LANES = 128
HEAD_PAIRS = HEADS // 2
VMEM_LIMIT = 56 * 1024 * 1024

_C_CKV = (0, 128)
_C_KR = (128, 256)
_C_KRR = (256, 384)
_C_KB = (384, 896)
KV_EXT = 896
_C_CQ = (896, 1152)
_C_ZA = (1152, 1664)
_C_ZB = (1664, 2176)
_C_GA = (2176, 3200)
_C_GB = (3200, 4224)
_R_VB = (0, 512)
_R_QB = (512, 1024)

NA_ROWS_PER_BLOCK = 4
NA_BLOCKS = GRID_H // NA_ROWS_PER_BLOCK
NA_TQ = NA_ROWS_PER_BLOCK * GRID_W
NA_SLAB_ROWS = 12
NA_SLAB = NA_SLAB_ROWS * GRID_W
NA_MASK = -1e30
LOG2_E = 1.4426950408889634

F32 = jnp.float32
BF16 = jnp.bfloat16


def _dot(a, b):
    return jnp.dot(a, b, preferred_element_type=F32)


def _dot_nt(a, b):
    return lax.dot_general(a, b, (((1,), (1,)), ((), ())), preferred_element_type=F32)


def _rms(x, g):
    return x * lax.rsqrt(jnp.mean(x * x, axis=-1, keepdims=True) + EPS) * g


def _mod_kernel(c_ref, w_ref, b_ref, o_ref):
    c = c_ref[...]
    sc = c * jax.nn.sigmoid(c)
    o_ref[...] = _dot(sc.astype(BF16), w_ref[...].astype(BF16)) + b_ref[...]


def _mod_call(cs, w_mod, b_mod):
    n = w_mod.shape[1]
    tn = 1024
    return pl.pallas_call(
        _mod_kernel,
        grid=(n // tn,),
        in_specs=[
            pl.BlockSpec((8, D_MODEL), lambda j: (0, 0)),
            pl.BlockSpec((D_MODEL, tn), lambda j: (0, j)),
            pl.BlockSpec((1, tn), lambda j: (0, j)),
        ],
        out_specs=pl.BlockSpec((8, tn), lambda j: (0, j)),
        out_shape=jax.ShapeDtypeStruct((8, n), F32),
        name="adaln_mod",
    )(cs, w_mod, b_mod)


def _proj_kernel(with_q, x_ref, shift_ref, scale_ref, ng_ref, w_ref, wt_ref, gkv_ref, wk_ref, wv_ref,
                 ck_ref, sk_ref, *rest):
    if with_q:
        (gq_ref, wq_ref, wqr_ref, cq_ref, sq_ref,
         k_ref, v_ref, kb_ref, vb_ref, q_ref, qb_ref, sza_ref, szb_ref, sga_ref, sgb_ref) = rest
    else:
        k_ref, v_ref, kb_ref, vb_ref = rest

    x = x_ref[0]
    h = _rms(x, ng_ref[...]) * (1.0 + scale_ref[0]) + shift_ref[0]
    hb = h.astype(BF16)

    def proj(c):
        return _dot(hb, w_ref[:, c[0]:c[1]])

    def proj_t(r):
        return _dot_nt(wt_ref[r[0]:r[1], :], hb)

    ckvn = _rms(proj(_C_CKV), gkv_ref[...]).astype(BF16)
    if with_q:
        cqn = _rms(proj(_C_CQ), gq_ref[...]).astype(BF16)
    krope = proj(_C_KR) * ck_ref[...] + proj(_C_KRR) * sk_ref[...]
    kb_ref[0] = proj(_C_KB).astype(BF16)
    vb_ref[0] = proj_t(_R_VB).astype(BF16)
    if with_q:
        qb_ref[0] = proj_t(_R_QB).astype(BF16)
        za = proj(_C_ZA)
        sza_ref[0] = (za * jax.nn.sigmoid(za)).astype(BF16)
        zb = proj(_C_ZB)
        szb_ref[0] = (zb * jax.nn.sigmoid(zb)).astype(BF16)
        sga_ref[0] = jax.nn.sigmoid(proj(_C_GA)).astype(BF16)
        sgb_ref[0] = jax.nn.sigmoid(proj(_C_GB)).astype(BF16)

    knope = _dot(ckvn, wk_ref[...])
    for hd in range(HEADS):
        sl = slice(hd * LANES, (hd + 1) * LANES)
        k_ref[0, :, sl] = (knope[:, sl] + krope).astype(BF16)
    v_ref[0] = _dot_nt(wv_ref[...], ckvn).astype(BF16)
    if with_q:
        q1 = _dot_nt(wq_ref[...], cqn)
        q2 = _dot_nt(wqr_ref[...], cqn)
        cq = cq_ref[...]
        sq = sq_ref[...]
        for hd in range(HEADS):
            sl = slice(hd * LANES, (hd + 1) * LANES)
            q_ref[0, sl, :] = (q1[sl] * cq + q2[sl] * sq).astype(BF16)


def _proj_call(x, shift, scale, per_batch_mod, norm_g, w_ext, wt_ext, g_ckv, wk, wv, ck, sk,
               q_side=None, tm=256):
    b, s, d = x.shape
    tm = min(tm, s)
    with_q = q_side is not None
    n_ext = w_ext.shape[1]
    n_t = wt_ext.shape[0]
    mod_map = (lambda i, j: (i, 0, 0)) if per_batch_mod else (lambda i, j: (0, 0, 0))
    const2 = lambda i, j: (0, 0)
    tok3 = lambda i, j: (i, j, 0)
    tab = lambda i, j: (j, 0)

    in_specs = [
        pl.BlockSpec((1, tm, d), tok3),
        pl.BlockSpec((1, 1, d), mod_map),
        pl.BlockSpec((1, 1, d), mod_map),
        pl.BlockSpec((1, d), const2),
        pl.BlockSpec((d, n_ext), const2),
        pl.BlockSpec((n_t, d), const2),
        pl.BlockSpec((1, KV_LORA), const2),
        pl.BlockSpec((KV_LORA, HEADS * LANES), const2),
        pl.BlockSpec((HEADS * MLA_V, KV_LORA), const2),
        pl.BlockSpec((tm, LANES), tab),
        pl.BlockSpec((tm, LANES), tab),
    ]
    args = [x, shift, scale, norm_g, w_ext, wt_ext, g_ckv, wk, wv, ck, sk]
    tok_major = lambda w: (pl.BlockSpec((1, tm, w), tok3), jax.ShapeDtypeStruct((b, s, w), BF16))
    feat_major = lambda w: (pl.BlockSpec((1, w, tm), lambda i, j: (i, 0, j)),
                            jax.ShapeDtypeStruct((b, w, s), BF16))
    outs = [tok_major(HEADS * LANES), feat_major(HEADS * MLA_V), tok_major(HEADS * NA_DIM),
            feat_major(HEADS * NA_DIM)]
    if with_q:
        g_cq, wq, wqr, cq, sq = q_side
        in_specs += [
            pl.BlockSpec((1, Q_LORA), const2),
            pl.BlockSpec((HEADS * LANES, Q_LORA), const2),
            pl.BlockSpec((HEADS * LANES, Q_LORA), const2),
            pl.BlockSpec((LANES, tm), lambda i, j: (0, j)),
            pl.BlockSpec((LANES, tm), lambda i, j: (0, j)),
        ]
        args += [g_cq, wq, wqr, cq, sq]
        outs += [feat_major(HEADS * LANES), feat_major(HEADS * NA_DIM), tok_major(HEADS * MLA_V),
                 tok_major(HEADS * NA_DIM), tok_major(d), tok_major(d)]
    out_specs = [o[0] for o in outs]
    out_shape = [o[1] for o in outs]
    return pl.pallas_call(
        functools.partial(_proj_kernel, with_q),
        grid=(b, s // tm),
        in_specs=in_specs,
        out_specs=out_specs,
        out_shape=out_shape,
        compiler_params=pltpu.CompilerParams(vmem_limit_bytes=VMEM_LIMIT),
        name="in_proj_q" if with_q else "in_proj_ctx",
    )(*args)


BF16_SUBLANES = 16
MLA_TQ = 512
MLA_CHUNK = 512
MLA_QK_CHUNKS = 1
MLA_LOOKAHEAD = 2


def _mla_kernel(qt_ref, k_ref, kc_ref, vt_ref, vct_ref, o_ref):
    s_len = k_ref.shape[1]
    qts = [qt_ref[0, hh * LANES:(hh + 1) * LANES, :] for hh in range(2)]
    qk_keys = MLA_QK_CHUNKS * MLA_CHUNK
    groups = [(hh, k_ref, vt_ref, c * qk_keys, qk_keys) for c in range(s_len // qk_keys) for hh in range(2)]
    groups += [(hh, kc_ref, vct_ref, 0, kc_ref.shape[1]) for hh in range(2)]
    items = [(hh, vr, off + i, min(MLA_CHUNK, n - i))
             for hh, _, vr, off, n in groups for i in range(0, n, MLA_CHUNK)]

    def group_scores(g):
        hh, kr, _, off, n = g
        big = _dot(kr[0, off:off + n, hh * LANES:(hh + 1) * LANES], qts[hh]).astype(BF16)
        return [big[i:i + MLA_CHUNK] for i in range(0, n, MLA_CHUNK)]

    m = [None, None]
    acc = [None, None]
    pending = []
    gi = 0
    for hh, vr, off, n in items:
        while len(pending) <= MLA_LOOKAHEAD and gi < len(groups):
            pending.extend(group_scores(groups[gi]))
            gi += 1
        s = pending.pop(0)
        mc = jnp.max(s, axis=0, keepdims=True)
        m_new = mc if m[hh] is None else jnp.maximum(m[hh], mc)
        p = jnp.exp2(s - m_new)
        v_ones = jnp.concatenate([vr[0, hh * MLA_V:(hh + 1) * MLA_V, off:off + n],
                                  jnp.ones((BF16_SUBLANES, n), BF16)], axis=0)
        pv = _dot(v_ones, p)
        if acc[hh] is None:
            acc[hh] = pv
        else:
            acc[hh] = acc[hh] * jnp.exp2(m[hh].astype(F32) - m_new.astype(F32)) + pv
        m[hh] = m_new
    outs = [a[:MLA_V] / a[MLA_V:MLA_V + 1] for a in acc]
    o_ref[0] = jnp.concatenate(outs, axis=0).T.astype(BF16)


def _mla_call(qt, k, kc, vt, vct):
    b, _, s = qt.shape
    t = kc.shape[1]
    return pl.pallas_call(
        _mla_kernel,
        grid=(b, HEAD_PAIRS, s // MLA_TQ),
        in_specs=[
            pl.BlockSpec((1, 2 * LANES, MLA_TQ), lambda i, p, j: (i, p, j)),
            pl.BlockSpec((1, s, 2 * LANES), lambda i, p, j: (i, 0, p)),
            pl.BlockSpec((1, t, 2 * LANES), lambda i, p, j: (i, 0, p)),
            pl.BlockSpec((1, 2 * MLA_V, s), lambda i, p, j: (i, p, 0)),
            pl.BlockSpec((1, 2 * MLA_V, t), lambda i, p, j: (i, p, 0)),
        ],
        out_specs=pl.BlockSpec((1, MLA_TQ, LANES), lambda i, p, j: (i, j, p)),
        out_shape=jax.ShapeDtypeStruct((b, s, HEADS * MLA_V), BF16),
        compiler_params=pltpu.CompilerParams(vmem_limit_bytes=VMEM_LIMIT),
        name="mla_attention",
    )(qt, k, kc, vt, vct)


NA_HEADS_PER_STEP = 8
NA_LOOKAHEAD = 2


def _na_slab_start(blk):
    return jnp.clip(blk * NA_ROWS_PER_BLOCK - NA_KR // 2, 0, GRID_H - NA_SLAB_ROWS)


def _na_variant(blk):
    return jnp.where(blk == 0, 0, jnp.where(blk == NA_BLOCKS - 1, 2, 1))


def _with_ones(v):
    return jnp.concatenate([v, jnp.ones((BF16_SUBLANES, v.shape[1]), v.dtype)], axis=0)


def _na_kernel(qt_ref, k_ref, vt_ref, kc_ref, vct_ref, bias_ref, o_ref):
    blk = pl.program_id(2)
    start = pl.multiple_of(_na_slab_start(blk) * GRID_W, NA_TQ)
    row_half = lax.broadcasted_iota(jnp.int32, (LANES, NA_TQ), 0) // NA_DIM

    def scores(h):
        pair, hh = divmod(h, 2)
        cols = slice(pair * LANES, (pair + 1) * LANES)
        qt = qt_ref[0, cols, :]
        qm = jnp.where(row_half == hh, qt, jnp.zeros_like(qt))
        s1 = _dot(k_ref[0, pl.ds(start, NA_SLAB), cols], qm) + bias_ref[0, h]
        s2 = _dot(kc_ref[0, :, cols], qm)
        return s1, s2

    outs = []
    ahead = [scores(h) for h in range(NA_LOOKAHEAD)]
    for h in range(NA_HEADS_PER_STEP):
        if h + NA_LOOKAHEAD < NA_HEADS_PER_STEP:
            ahead.append(scores(h + NA_LOOKAHEAD))
        s1, s2 = ahead.pop(0)
        m = jnp.maximum(jnp.max(s1, axis=0, keepdims=True), jnp.max(s2, axis=0, keepdims=True))
        p1 = jnp.exp((s1 - m).astype(BF16))
        p2 = jnp.exp((s2 - m).astype(BF16))
        rows = slice(h * NA_DIM, (h + 1) * NA_DIM)
        pv = (_dot(_with_ones(vt_ref[0, rows, pl.ds(start, NA_SLAB)]), p1)
              + _dot(_with_ones(vct_ref[0, rows, :]), p2))
        outs.append(pv[:NA_DIM] / pv[NA_DIM:NA_DIM + 1])
    o_ref[0] = jnp.concatenate(outs, axis=0).T.astype(BF16)


def _na_call(qbt, kb, vbt, kcb, vcbt, bias):
    b, s, _ = kb.shape
    t = kcb.shape[1]
    wide = NA_HEADS_PER_STEP * NA_DIM
    return pl.pallas_call(
        _na_kernel,
        grid=(HEADS // NA_HEADS_PER_STEP, b, NA_BLOCKS),
        in_specs=[
            pl.BlockSpec((1, wide, NA_TQ), lambda g, i, j: (i, g, j)),
            pl.BlockSpec((1, s, wide), lambda g, i, j: (i, 0, g)),
            pl.BlockSpec((1, wide, s), lambda g, i, j: (i, g, 0)),
            pl.BlockSpec((1, t, wide), lambda g, i, j: (i, 0, g)),
            pl.BlockSpec((1, wide, t), lambda g, i, j: (i, g, 0)),
            pl.BlockSpec((1, NA_HEADS_PER_STEP, NA_SLAB, NA_TQ),
                         lambda g, i, j: (_na_variant(j), g, 0, 0)),
        ],
        out_specs=pl.BlockSpec((1, NA_TQ, wide), lambda g, i, j: (i, j, g)),
        out_shape=jax.ShapeDtypeStruct((b, s, HEADS * NA_DIM), BF16),
        compiler_params=pltpu.CompilerParams(vmem_limit_bytes=VMEM_LIMIT),
        name="na_attention",
    )(qbt, kb, vbt, kcb, vcbt, bias)


def _na_bias_kernel(rpb_ref, o_ref):
    rp = pltpu.roll(rpb_ref[0], LANES - (NA_KC - 1), axis=1)
    shape = (GRID_W, LANES)
    wk = lax.broadcasted_iota(jnp.int32, shape, 0)
    lane = lax.broadcasted_iota(jnp.int32, shape, 1)
    c_start = jnp.clip(lane - NA_KC // 2, 0, GRID_W - NA_KC)
    left = lane < GRID_W
    mask_l = jnp.where(left, NA_MASK, 0.0).astype(F32)
    mask_r = jnp.where(left, 0.0, NA_MASK).astype(F32)
    tiles_l, tiles_r = [], []
    for d in range(2 * NA_KR - 1):
        row = jnp.broadcast_to(rp[d:d + 1, :], shape)
        t = pltpu.roll(row, 0, axis=1, stride=1, stride_axis=0)
        t = jnp.where(wk >= c_start, jnp.where(wk < c_start + NA_KC, t, NA_MASK), NA_MASK)
        t = jnp.where(left, t, 0.0)
        tiles_l.append(t)
        tiles_r.append(pltpu.roll(t, GRID_W, axis=1))
    for n, blk in enumerate((0, 1, NA_BLOCKS - 1)):
        sb = min(max(blk * NA_ROWS_PER_BLOCK - NA_KR // 2, 0), GRID_H - NA_SLAB_ROWS)
        for j in range(NA_SLAB_ROWS):
            kr = sb + j

            def pick(i, tiles, masked):
                r = blk * NA_ROWS_PER_BLOCK + i
                r_start = min(max(r - NA_KR // 2, 0), GRID_H - NA_KR)
                if r_start <= kr < r_start + NA_KR:
                    return tiles[kr - r + NA_KR - 1]
                return masked

            for ii in range(NA_ROWS_PER_BLOCK // 2):
                o_ref[n, 0, j * GRID_W:(j + 1) * GRID_W, ii * LANES:(ii + 1) * LANES] = (
                    pick(2 * ii, tiles_l, mask_l) + pick(2 * ii + 1, tiles_r, mask_r))


def _na_bias_tables(rpb):
    rpb_p = jnp.pad(rpb[:, :, ::-1], ((0, 0), (0, 1), (0, LANES - (2 * NA_KC - 1))))
    return pl.pallas_call(
        _na_bias_kernel,
        grid=(HEADS,),
        in_specs=[pl.BlockSpec((1, 2 * NA_KR, LANES), lambda h: (h, 0, 0))],
        out_specs=pl.BlockSpec((3, 1, NA_SLAB, NA_TQ), lambda h: (0, h, 0, 0)),
        out_shape=jax.ShapeDtypeStruct((3, HEADS, NA_SLAB, NA_TQ), F32),
        name="na_bias_table",
    )(rpb_p)


def _out_kernel(oa_ref, ob_ref, sza_ref, szb_ref, sga_ref, sgb_ref, x_ref, gate_ref,
                woa_ref, wob_ref, wout_ref, fg_ref, o_ref):
    subs = [slice(i, i + OUT_SUB) for i in range(0, o_ref.shape[1], OUT_SUB)]

    def branches(sl):
        ya = _dot(oa_ref[0, sl] * sza_ref[0, sl], woa_ref[...])
        yb = _dot(ob_ref[0, sl] * szb_ref[0, sl], wob_ref[...])
        return (sga_ref[0, sl].astype(F32) * ya + sgb_ref[0, sl].astype(F32) * yb).astype(BF16)

    mixes = [branches(subs[0])]
    for i, sl in enumerate(subs):
        if i + 1 < len(subs):
            mixes.append(branches(subs[i + 1]))
        y = _dot(mixes[i], wout_ref[...])
        r = x_ref[0, sl] + gate_ref[0] * y
        o_ref[0, sl] = _rms(r, fg_ref[...])


OUT_SUB = 256


def _out_call(oa, ob, sza, szb, sga, sgb, x, gate, w_oa, w_ob, w_out, final_g, tm=1024):
    b, s, d = x.shape
    tok3 = lambda i, j: (i, j, 0)
    const2 = lambda i, j: (0, 0)
    half = HEADS * MLA_V
    return pl.pallas_call(
        _out_kernel,
        grid=(b, s // tm),
        in_specs=[
            pl.BlockSpec((1, tm, half), tok3),
            pl.BlockSpec((1, tm, half), tok3),
            pl.BlockSpec((1, tm, half), tok3),
            pl.BlockSpec((1, tm, half), tok3),
            pl.BlockSpec((1, tm, d), tok3),
            pl.BlockSpec((1, tm, d), tok3),
            pl.BlockSpec((1, tm, d), tok3),
            pl.BlockSpec((1, 1, d), lambda i, j: (i, 0, 0)),
            pl.BlockSpec((half, d), const2),
            pl.BlockSpec((half, d), const2),
            pl.BlockSpec((d, d), const2),
            pl.BlockSpec((1, d), const2),
        ],
        out_specs=pl.BlockSpec((1, tm, d), tok3),
        out_shape=jax.ShapeDtypeStruct((b, s, d), F32),
        compiler_params=pltpu.CompilerParams(vmem_limit_bytes=VMEM_LIMIT),
        name="gated_out",
    )(oa, ob, sza, szb, sga, sgb, x, gate, w_oa, w_ob, w_out, final_g)


def _rope_partner():
    j = np.arange(MLA_ROPE)
    first = (j % 16) < 8
    src = np.where(first, j + 8, j - 8)
    sign = np.where(first, -1.0, 1.0).astype(np.float32)
    return src, sign


def _rope_tables():
    t = np.arange(SEQ)
    half = MLA_ROPE // 4
    freqs = (ROPE_THETA ** (-np.arange(half, dtype=np.float32) / half)).astype(np.float32)
    ang_r = (t // GRID_W).astype(np.float32)[:, None] * freqs[None, :]
    ang_c = (t % GRID_W).astype(np.float32)[:, None] * freqs[None, :]
    cos = np.concatenate([np.cos(ang_r)] * 2 + [np.cos(ang_c)] * 2, axis=-1).astype(np.float32)
    sin = np.concatenate([np.sin(ang_r)] * 2 + [np.sin(ang_c)] * 2, axis=-1).astype(np.float32)
    return cos, sin


def _np_lane_group(nope, rope):
    pad = np.zeros((rope.shape[0], LANES - MLA_NOPE - MLA_ROPE), np.float32)
    return np.concatenate([nope, rope, pad], axis=-1).astype(np.float32)


def _lane_group(nope, rope, n):
    pad = jnp.zeros((n, LANES - MLA_NOPE - MLA_ROPE), F32)
    return jnp.concatenate([nope, rope, pad], axis=-1)


def kernel(x, c, ctx, c_ctx, w_mod, b_mod, norm_g, w_in, g_cq, w_uq, g_ckv, w_ukv, rpb,
           w_oa, w_ob, w_out, final_g):
    b, s, d = x.shape
    depth = w_mod.shape[0]
    assert depth == 1 and s == SEQ and d == D_MODEL
    mla_scale = (MLA_NOPE + MLA_ROPE) ** -0.5 * LOG2_E
    na_scale = NA_DIM ** -0.5
    src, sign = _rope_partner()

    w = w_in[0]
    o = np.cumsum([0, KV_LORA, MLA_ROPE, HEADS * NA_DIM, HEADS * NA_DIM, Q_LORA, HEADS * MLA_V,
                   HEADS * NA_DIM, HEADS * NA_DIM, d, d])
    seg = [w[:, o[n]:o[n + 1]] for n in range(10)]
    w_ckv, w_kr, w_kb, w_vb, w_cq, w_za, w_qb, w_zb, w_ga, w_gb = seg
    z64 = jnp.zeros((d, MLA_NOPE), F32)
    w_ext = jnp.concatenate([
        w_ckv,
        _lane_group(z64, w_kr, d),
        _lane_group(z64, w_kr[:, src] * sign, d),
        w_kb, w_cq, w_za, w_zb, w_ga, w_gb], axis=-1).astype(BF16)
    wt_ext = jnp.concatenate([w_vb, w_qb * na_scale], axis=-1).T.astype(BF16)

    wkv3 = w_ukv[0].reshape(KV_LORA, HEADS, MLA_NOPE + MLA_V)
    wk = jnp.concatenate([wkv3[:, :, :MLA_NOPE], jnp.zeros((KV_LORA, HEADS, LANES - MLA_NOPE), F32)],
                         axis=-1).reshape(KV_LORA, HEADS * LANES).astype(BF16)
    wv = wkv3[:, :, MLA_NOPE:].reshape(KV_LORA, HEADS * MLA_V).T.astype(BF16)
    wq3 = w_uq[0].reshape(Q_LORA, HEADS, MLA_NOPE + MLA_ROPE)
    zq = jnp.zeros((Q_LORA, HEADS, LANES - MLA_NOPE - MLA_ROPE), F32)
    wq = jnp.concatenate([wq3, zq], axis=-1).reshape(Q_LORA, HEADS * LANES).T.astype(BF16)
    wqr = jnp.concatenate([jnp.zeros((Q_LORA, HEADS, MLA_NOPE), F32),
                           wq3[:, :, MLA_NOPE:][:, :, src] * sign, zq],
                          axis=-1).reshape(Q_LORA, HEADS * LANES).T.astype(BF16)

    cos, sin = _rope_tables()
    ones_s = np.ones((s, MLA_NOPE), np.float32)
    zeros_s = np.zeros((s, MLA_NOPE), np.float32)
    cq_tab = jnp.asarray(np.ascontiguousarray((_np_lane_group(ones_s, cos) * np.float32(mla_scale)).T))
    sq_tab = jnp.asarray(np.ascontiguousarray((_np_lane_group(zeros_s, sin) * np.float32(mla_scale)).T))
    ck_tab = jnp.asarray(_np_lane_group(zeros_s, cos))
    sk_tab = jnp.asarray(_np_lane_group(zeros_s, sin))
    zeros_c = np.zeros((CTX_LEN, MLA_NOPE), np.float32)
    ck_ctx = jnp.asarray(_np_lane_group(zeros_c, np.ones((CTX_LEN, MLA_ROPE), np.float32)))
    sk_ctx = jnp.zeros((CTX_LEN, LANES), F32)

    cs = jnp.concatenate([c, c_ctx[None, :], jnp.zeros((8 - b - 1, d), F32)], axis=0)
    mod = _mod_call(cs, w_mod[0], b_mod[0][None, :])
    shift, scale, gate = (mod[:b, None, n * d:(n + 1) * d] for n in range(3))
    shift_c, scale_c = (mod[b:b + 1, None, n * d:(n + 1) * d] for n in range(2))

    ng = norm_g[0][None, :]
    gkv = g_ckv[0][None, :]
    k, v, kb, vb, q, qb, sza, szb, sga, sgb = _proj_call(
        x, shift, scale, True, ng, w_ext, wt_ext, gkv, wk, wv, ck_tab, sk_tab,
        q_side=(g_cq[0][None, :], wq, wqr, cq_tab, sq_tab))
    kc, vc, kcb, vcb = _proj_call(
        ctx, shift_c, scale_c, False, ng, w_ext[:, :KV_EXT], wt_ext[:_R_VB[1]], gkv, wk, wv,
        ck_ctx, sk_ctx)

    oa = _mla_call(q, k, kc, v, vc)
    ob = _na_call(qb, kb, vb, kcb, vcb, _na_bias_tables(rpb[0]))

    return _out_call(oa, ob, sza, szb, sga, sgb, x, gate,
                     w_oa[0].astype(BF16), w_ob[0].astype(BF16), w_out[0].astype(BF16),
                     final_g[None, :])
```

```python
import functools

import jax
import jax.numpy as jnp
import numpy as np
from jax import lax
from jax.experimental import pallas as pl
from jax.experimental.pallas import tpu as pltpu

D_MODEL = 1024
SEQ = 4096
GRID_W = 64
GRID_H = SEQ // GRID_W
CTX_LEN = 256
HEADS = 8
MLA_NOPE = 64
MLA_ROPE = 32
MLA_V = 64
Q_LORA = 256
KV_LORA = 128
NA_DIM = 64
NA_KR = 8
NA_KC = 16
ROPE_THETA = 10000.0
EPS = 1e-6

LANES = 128
HEAD_PAIRS = HEADS // 2
VMEM_LIMIT = 56 * 1024 * 1024

_C_CKV = (0, 128)
_C_KR = (128, 256)
_C_KRR = (256, 384)
_C_KB = (384, 896)
KV_EXT = 896
_C_CQ = (896, 1152)
_C_ZA = (1152, 1664)
_C_ZB = (1664, 2176)
_C_GA = (2176, 3200)
_C_GB = (3200, 4224)
_R_VB = (0, 512)
_R_QB = (512, 1024)

NA_ROWS_PER_BLOCK = 4
NA_BLOCKS = GRID_H // NA_ROWS_PER_BLOCK
NA_TQ = NA_ROWS_PER_BLOCK * GRID_W
NA_SLAB_ROWS = 12
NA_SLAB = NA_SLAB_ROWS * GRID_W
NA_MASK = -1e30
LOG2_E = 1.4426950408889634

F32 = jnp.float32
BF16 = jnp.bfloat16


def _dot(a, b):
    return jnp.dot(a, b, preferred_element_type=F32)


def _dot_nt(a, b):
    return lax.dot_general(a, b, (((1,), (1,)), ((), ())), preferred_element_type=F32)


def _rms(x, g):
    return x * lax.rsqrt(jnp.mean(x * x, axis=-1, keepdims=True) + EPS) * g


def _mod_kernel(c_ref, w_ref, b_ref, o_ref):
    c = c_ref[...]
    sc = c * jax.nn.sigmoid(c)
    o_ref[...] = _dot(sc.astype(BF16), w_ref[...].astype(BF16)) + b_ref[...]


def _mod_call(cs, w_mod, b_mod):
    n = w_mod.shape[1]
    tn = 1024
    return pl.pallas_call(
        _mod_kernel,
        grid=(n // tn,),
        in_specs=[
            pl.BlockSpec((8, D_MODEL), lambda j: (0, 0)),
            pl.BlockSpec((D_MODEL, tn), lambda j: (0, j)),
            pl.BlockSpec((1, tn), lambda j: (0, j)),
        ],
        out_specs=pl.BlockSpec((8, tn), lambda j: (0, j)),
        out_shape=jax.ShapeDtypeStruct((8, n), F32),
        name="adaln_mod",
    )(cs, w_mod, b_mod)


def _proj_kernel(with_q, x_ref, shift_ref, scale_ref, ng_ref, w_ref, wt_ref, gkv_ref, wk_ref, wv_ref,
                 ck_ref, sk_ref, *rest):
    if with_q:
        (gq_ref, wq_ref, wqr_ref, cq_ref, sq_ref,
         k_ref, v_ref, kb_ref, vb_ref, q_ref, qb_ref, sza_ref, szb_ref, sga_ref, sgb_ref) = rest
    else:
        k_ref, v_ref, kb_ref, vb_ref = rest

    x = x_ref[0]
    h = _rms(x, ng_ref[...]) * (1.0 + scale_ref[0]) + shift_ref[0]
    hb = h.astype(BF16)

    def proj(c):
        return _dot(hb, w_ref[:, c[0]:c[1]])

    def proj_t(r):
        return _dot_nt(wt_ref[r[0]:r[1], :], hb)

    ckvn = _rms(proj(_C_CKV), gkv_ref[...]).astype(BF16)
    if with_q:
        cqn = _rms(proj(_C_CQ), gq_ref[...]).astype(BF16)
    krope = proj(_C_KR) * ck_ref[...] + proj(_C_KRR) * sk_ref[...]
    kb_ref[0] = proj(_C_KB).astype(BF16)
    vb_ref[0] = proj_t(_R_VB).astype(BF16)
    if with_q:
        qb_ref[0] = proj_t(_R_QB).astype(BF16)
        za = proj(_C_ZA)
        sza_ref[0] = (za * jax.nn.sigmoid(za)).astype(BF16)
        zb = proj(_C_ZB)
        szb_ref[0] = (zb * jax.nn.sigmoid(zb)).astype(BF16)
        sga_ref[0] = jax.nn.sigmoid(proj(_C_GA)).astype(BF16)
        sgb_ref[0] = jax.nn.sigmoid(proj(_C_GB)).astype(BF16)

    knope = _dot(ckvn, wk_ref[...])
    for hd in range(HEADS):
        sl = slice(hd * LANES, (hd + 1) * LANES)
        k_ref[0, :, sl] = (knope[:, sl] + krope).astype(BF16)
    v_ref[0] = _dot_nt(wv_ref[...], ckvn).astype(BF16)
    if with_q:
        q1 = _dot_nt(wq_ref[...], cqn)
        q2 = _dot_nt(wqr_ref[...], cqn)
        cq = cq_ref[...]
        sq = sq_ref[...]
        for hd in range(HEADS):
            sl = slice(hd * LANES, (hd + 1) * LANES)
            q_ref[0, sl, :] = (q1[sl] * cq + q2[sl] * sq).astype(BF16)


def _proj_call(x, shift, scale, per_batch_mod, norm_g, w_ext, wt_ext, g_ckv, wk, wv, ck, sk,
               q_side=None, tm=256):
    b, s, d = x.shape
    tm = min(tm, s)
    with_q = q_side is not None
    n_ext = w_ext.shape[1]
    n_t = wt_ext.shape[0]
    mod_map = (lambda i, j: (i, 0, 0)) if per_batch_mod else (lambda i, j: (0, 0, 0))
    const2 = lambda i, j: (0, 0)
    tok3 = lambda i, j: (i, j, 0)
    tab = lambda i, j: (j, 0)

    in_specs = [
        pl.BlockSpec((1, tm, d), tok3),
        pl.BlockSpec((1, 1, d), mod_map),
        pl.BlockSpec((1, 1, d), mod_map),
        pl.BlockSpec((1, d), const2),
        pl.BlockSpec((d, n_ext), const2),
        pl.BlockSpec((n_t, d), const2),
        pl.BlockSpec((1, KV_LORA), const2),
        pl.BlockSpec((KV_LORA, HEADS * LANES), const2),
        pl.BlockSpec((HEADS * MLA_V, KV_LORA), const2),
        pl.BlockSpec((tm, LANES), tab),
        pl.BlockSpec((tm, LANES), tab),
    ]
    args = [x, shift, scale, norm_g, w_ext, wt_ext, g_ckv, wk, wv, ck, sk]
    tok_major = lambda w: (pl.BlockSpec((1, tm, w), tok3), jax.ShapeDtypeStruct((b, s, w), BF16))
    feat_major = lambda w: (pl.BlockSpec((1, w, tm), lambda i, j: (i, 0, j)),
                            jax.ShapeDtypeStruct((b, w, s), BF16))
    outs = [tok_major(HEADS * LANES), feat_major(HEADS * MLA_V), tok_major(HEADS * NA_DIM),
            feat_major(HEADS * NA_DIM)]
    if with_q:
        g_cq, wq, wqr, cq, sq = q_side
        in_specs += [
            pl.BlockSpec((1, Q_LORA), const2),
            pl.BlockSpec((HEADS * LANES, Q_LORA), const2),
            pl.BlockSpec((HEADS * LANES, Q_LORA), const2),
            pl.BlockSpec((LANES, tm), lambda i, j: (0, j)),
            pl.BlockSpec((LANES, tm), lambda i, j: (0, j)),
        ]
        args += [g_cq, wq, wqr, cq, sq]
        outs += [feat_major(HEADS * LANES), feat_major(HEADS * NA_DIM), tok_major(HEADS * MLA_V),
                 tok_major(HEADS * NA_DIM), tok_major(d), tok_major(d)]
    out_specs = [o[0] for o in outs]
    out_shape = [o[1] for o in outs]
    return pl.pallas_call(
        functools.partial(_proj_kernel, with_q),
        grid=(b, s // tm),
        in_specs=in_specs,
        out_specs=out_specs,
        out_shape=out_shape,
        compiler_params=pltpu.CompilerParams(vmem_limit_bytes=VMEM_LIMIT),
        name="in_proj_q" if with_q else "in_proj_ctx",
    )(*args)


BF16_SUBLANES = 16
MLA_TQ = 512
MLA_CHUNK = 512
MLA_QK_CHUNKS = 1
MLA_LOOKAHEAD = 2


def _mla_kernel(qt_ref, k_ref, kc_ref, vt_ref, vct_ref, o_ref):
    s_len = k_ref.shape[1]
    qts = [qt_ref[0, hh * LANES:(hh + 1) * LANES, :] for hh in range(2)]
    qk_keys = MLA_QK_CHUNKS * MLA_CHUNK
    groups = [(hh, k_ref, vt_ref, c * qk_keys, qk_keys) for c in range(s_len // qk_keys) for hh in range(2)]
    groups += [(hh, kc_ref, vct_ref, 0, kc_ref.shape[1]) for hh in range(2)]
    items = [(hh, vr, off + i, min(MLA_CHUNK, n - i))
             for hh, _, vr, off, n in groups for i in range(0, n, MLA_CHUNK)]

    def group_scores(g):
        hh, kr, _, off, n = g
        big = _dot(kr[0, off:off + n, hh * LANES:(hh + 1) * LANES], qts[hh]).astype(BF16)
        return [big[i:i + MLA_CHUNK] for i in range(0, n, MLA_CHUNK)]

    m = [None, None]
    acc = [None, None]
    pending = []
    gi = 0
    for hh, vr, off, n in items:
        while len(pending) <= MLA_LOOKAHEAD and gi < len(groups):
            pending.extend(group_scores(groups[gi]))
            gi += 1
        s = pending.pop(0)
        mc = jnp.max(s, axis=0, keepdims=True)
        m_new = mc if m[hh] is None else jnp.maximum(m[hh], mc)
        p = jnp.exp2(s - m_new)
        v_ones = jnp.concatenate([vr[0, hh * MLA_V:(hh + 1) * MLA_V, off:off + n],
                                  jnp.ones((BF16_SUBLANES, n), BF16)], axis=0)
        pv = _dot(v_ones, p)
        if acc[hh] is None:
            acc[hh] = pv
        else:
            acc[hh] = acc[hh] * jnp.exp2(m[hh].astype(F32) - m_new.astype(F32)) + pv
        m[hh] = m_new
    outs = [a[:MLA_V] / a[MLA_V:MLA_V + 1] for a in acc]
    o_ref[0] = jnp.concatenate(outs, axis=0).T.astype(BF16)


def _mla_call(qt, k, kc, vt, vct):
    b, _, s = qt.shape
    t = kc.shape[1]
    return pl.pallas_call(
        _mla_kernel,
        grid=(b, HEAD_PAIRS, s // MLA_TQ),
        in_specs=[
            pl.BlockSpec((1, 2 * LANES, MLA_TQ), lambda i, p, j: (i, p, j)),
            pl.BlockSpec((1, s, 2 * LANES), lambda i, p, j: (i, 0, p)),
            pl.BlockSpec((1, t, 2 * LANES), lambda i, p, j: (i, 0, p)),
            pl.BlockSpec((1, 2 * MLA_V, s), lambda i, p, j: (i, p, 0)),
            pl.BlockSpec((1, 2 * MLA_V, t), lambda i, p, j: (i, p, 0)),
        ],
        out_specs=pl.BlockSpec((1, MLA_TQ, LANES), lambda i, p, j: (i, j, p)),
        out_shape=jax.ShapeDtypeStruct((b, s, HEADS * MLA_V), BF16),
        compiler_params=pltpu.CompilerParams(vmem_limit_bytes=VMEM_LIMIT),
        name="mla_attention",
    )(qt, k, kc, vt, vct)


NA_HEADS_PER_STEP = 8
NA_LOOKAHEAD = 3


def _na_slab_start(blk):
    return jnp.clip(blk * NA_ROWS_PER_BLOCK - NA_KR // 2, 0, GRID_H - NA_SLAB_ROWS)


def _na_variant(blk):
    return jnp.where(blk == 0, 0, jnp.where(blk == NA_BLOCKS - 1, 2, 1))


def _with_ones(v):
    return jnp.concatenate([v, jnp.ones((BF16_SUBLANES, v.shape[1]), v.dtype)], axis=0)


def _na_kernel(qt_ref, k_ref, vt_ref, kc_ref, vct_ref, bias_ref, o_ref):
    blk = pl.program_id(2)
    start = pl.multiple_of(_na_slab_start(blk) * GRID_W, NA_TQ)
    row_half = lax.broadcasted_iota(jnp.int32, (LANES, NA_TQ), 0) // NA_DIM

    k_all = [jnp.concatenate([k_ref[0, pl.ds(start, NA_SLAB), pair * LANES:(pair + 1) * LANES],
                              kc_ref[0, :, pair * LANES:(pair + 1) * LANES]], axis=0)
             for pair in range(NA_HEADS_PER_STEP // 2)]

    def scores(h):
        pair, hh = divmod(h, 2)
        qt = qt_ref[0, pair * LANES:(pair + 1) * LANES, :]
        qm = jnp.where(row_half == hh, qt, jnp.zeros_like(qt))
        s = _dot(k_all[pair], qm).astype(BF16)
        return jnp.concatenate([s[:NA_SLAB] + bias_ref[0, h], s[NA_SLAB:]], axis=0)

    outs = []
    ahead = [scores(h) for h in range(NA_LOOKAHEAD)]
    for h in range(NA_HEADS_PER_STEP):
        if h + NA_LOOKAHEAD < NA_HEADS_PER_STEP:
            ahead.append(scores(h + NA_LOOKAHEAD))
        s = ahead.pop(0)
        p = jnp.exp2(s - jnp.max(s, axis=0, keepdims=True))
        rows = slice(h * NA_DIM, (h + 1) * NA_DIM)
        v_all = jnp.concatenate([vt_ref[0, rows, pl.ds(start, NA_SLAB)], vct_ref[0, rows, :]], axis=1)
        pv = _dot(_with_ones(v_all), p)
        outs.append(pv[:NA_DIM] / pv[NA_DIM:NA_DIM + 1])
    o_ref[0] = jnp.concatenate(outs, axis=0).T.astype(BF16)


def _na_call(qbt, kb, vbt, kcb, vcbt, bias):
    b, s, _ = kb.shape
    t = kcb.shape[1]
    wide = NA_HEADS_PER_STEP * NA_DIM
    return pl.pallas_call(
        _na_kernel,
        grid=(HEADS // NA_HEADS_PER_STEP, b, NA_BLOCKS),
        in_specs=[
            pl.BlockSpec((1, wide, NA_TQ), lambda g, i, j: (i, g, j)),
            pl.BlockSpec((1, s, wide), lambda g, i, j: (i, 0, g)),
            pl.BlockSpec((1, wide, s), lambda g, i, j: (i, g, 0)),
            pl.BlockSpec((1, t, wide), lambda g, i, j: (i, 0, g)),
            pl.BlockSpec((1, wide, t), lambda g, i, j: (i, g, 0)),
            pl.BlockSpec((1, NA_HEADS_PER_STEP, NA_SLAB, NA_TQ),
                         lambda g, i, j: (_na_variant(j), g, 0, 0)),
        ],
        out_specs=pl.BlockSpec((1, NA_TQ, wide), lambda g, i, j: (i, j, g)),
        out_shape=jax.ShapeDtypeStruct((b, s, HEADS * NA_DIM), BF16),
        compiler_params=pltpu.CompilerParams(vmem_limit_bytes=VMEM_LIMIT),
        name="na_attention",
    )(qbt, kb, vbt, kcb, vcbt, bias)


def _na_bias_kernel(rpb_ref, o_ref):
    rp = pltpu.roll(rpb_ref[0] * LOG2_E, LANES - (NA_KC - 1), axis=1)
    shape = (GRID_W, LANES)
    wk = lax.broadcasted_iota(jnp.int32, shape, 0)
    lane = lax.broadcasted_iota(jnp.int32, shape, 1)
    c_start = jnp.clip(lane - NA_KC // 2, 0, GRID_W - NA_KC)
    left = lane < GRID_W
    mask_l = jnp.where(left, NA_MASK, 0.0).astype(F32)
    mask_r = jnp.where(left, 0.0, NA_MASK).astype(F32)
    tiles_l, tiles_r = [], []
    for d in range(2 * NA_KR - 1):
        row = jnp.broadcast_to(rp[d:d + 1, :], shape)
        t = pltpu.roll(row, 0, axis=1, stride=1, stride_axis=0)
        t = jnp.where(wk >= c_start, jnp.where(wk < c_start + NA_KC, t, NA_MASK), NA_MASK)
        t = jnp.where(left, t, 0.0)
        tiles_l.append(t)
        tiles_r.append(pltpu.roll(t, GRID_W, axis=1))
    for n, blk in enumerate((0, 1, NA_BLOCKS - 1)):
        sb = min(max(blk * NA_ROWS_PER_BLOCK - NA_KR // 2, 0), GRID_H - NA_SLAB_ROWS)
        for j in range(NA_SLAB_ROWS):
            kr = sb + j

            def pick(i, tiles, masked):
                r = blk * NA_ROWS_PER_BLOCK + i
                r_start = min(max(r - NA_KR // 2, 0), GRID_H - NA_KR)
                if r_start <= kr < r_start + NA_KR:
                    return tiles[kr - r + NA_KR - 1]
                return masked

            for ii in range(NA_ROWS_PER_BLOCK // 2):
                o_ref[n, 0, j * GRID_W:(j + 1) * GRID_W, ii * LANES:(ii + 1) * LANES] = (
                    pick(2 * ii, tiles_l, mask_l) + pick(2 * ii + 1, tiles_r, mask_r)).astype(BF16)


def _na_bias_tables(rpb):
    rpb_p = jnp.pad(rpb[:, :, ::-1], ((0, 0), (0, 1), (0, LANES - (2 * NA_KC - 1))))
    return pl.pallas_call(
        _na_bias_kernel,
        grid=(HEADS,),
        in_specs=[pl.BlockSpec((1, 2 * NA_KR, LANES), lambda h: (h, 0, 0))],
        out_specs=pl.BlockSpec((3, 1, NA_SLAB, NA_TQ), lambda h: (0, h, 0, 0)),
        out_shape=jax.ShapeDtypeStruct((3, HEADS, NA_SLAB, NA_TQ), BF16),
        name="na_bias_table",
    )(rpb_p)


def _out_kernel(oa_ref, ob_ref, sza_ref, szb_ref, sga_ref, sgb_ref, x_ref, gate_ref,
                woa_ref, wob_ref, wout_ref, fg_ref, o_ref):
    subs = [slice(i, i + OUT_SUB) for i in range(0, o_ref.shape[1], OUT_SUB)]

    def branches(sl):
        ya = _dot(oa_ref[0, sl] * sza_ref[0, sl], woa_ref[...])
        yb = _dot(ob_ref[0, sl] * szb_ref[0, sl], wob_ref[...])
        return (sga_ref[0, sl].astype(F32) * ya + sgb_ref[0, sl].astype(F32) * yb).astype(BF16)

    mixes = [branches(subs[0])]
    for i, sl in enumerate(subs):
        if i + 1 < len(subs):
            mixes.append(branches(subs[i + 1]))
        y = _dot(mixes[i], wout_ref[...])
        r = x_ref[0, sl] + gate_ref[0] * y
        o_ref[0, sl] = _rms(r, fg_ref[...])


OUT_SUB = 256


def _out_call(oa, ob, sza, szb, sga, sgb, x, gate, w_oa, w_ob, w_out, final_g, tm=1024):
    b, s, d = x.shape
    tok3 = lambda i, j: (i, j, 0)
    const2 = lambda i, j: (0, 0)
    half = HEADS * MLA_V
    return pl.pallas_call(
        _out_kernel,
        grid=(b, s // tm),
        in_specs=[
            pl.BlockSpec((1, tm, half), tok3),
            pl.BlockSpec((1, tm, half), tok3),
            pl.BlockSpec((1, tm, half), tok3),
            pl.BlockSpec((1, tm, half), tok3),
            pl.BlockSpec((1, tm, d), tok3),
            pl.BlockSpec((1, tm, d), tok3),
            pl.BlockSpec((1, tm, d), tok3),
            pl.BlockSpec((1, 1, d), lambda i, j: (i, 0, 0)),
            pl.BlockSpec((half, d), const2),
            pl.BlockSpec((half, d), const2),
            pl.BlockSpec((d, d), const2),
            pl.BlockSpec((1, d), const2),
        ],
        out_specs=pl.BlockSpec((1, tm, d), tok3),
        out_shape=jax.ShapeDtypeStruct((b, s, d), F32),
        compiler_params=pltpu.CompilerParams(vmem_limit_bytes=VMEM_LIMIT),
        name="gated_out",
    )(oa, ob, sza, szb, sga, sgb, x, gate, w_oa, w_ob, w_out, final_g)


def _rope_partner():
    j = np.arange(MLA_ROPE)
    first = (j % 16) < 8
    src = np.where(first, j + 8, j - 8)
    sign = np.where(first, -1.0, 1.0).astype(np.float32)
    return src, sign


def _rope_tables():
    t = np.arange(SEQ)
    half = MLA_ROPE // 4
    freqs = (ROPE_THETA ** (-np.arange(half, dtype=np.float32) / half)).astype(np.float32)
    ang_r = (t // GRID_W).astype(np.float32)[:, None] * freqs[None, :]
    ang_c = (t % GRID_W).astype(np.float32)[:, None] * freqs[None, :]
    cos = np.concatenate([np.cos(ang_r)] * 2 + [np.cos(ang_c)] * 2, axis=-1).astype(np.float32)
    sin = np.concatenate([np.sin(ang_r)] * 2 + [np.sin(ang_c)] * 2, axis=-1).astype(np.float32)
    return cos, sin


def _np_lane_group(nope, rope):
    pad = np.zeros((rope.shape[0], LANES - MLA_NOPE - MLA_ROPE), np.float32)
    return np.concatenate([nope, rope, pad], axis=-1).astype(np.float32)


def _lane_group(nope, rope, n):
    pad = jnp.zeros((n, LANES - MLA_NOPE - MLA_ROPE), F32)
    return jnp.concatenate([nope, rope, pad], axis=-1)


def kernel(x, c, ctx, c_ctx, w_mod, b_mod, norm_g, w_in, g_cq, w_uq, g_ckv, w_ukv, rpb,
           w_oa, w_ob, w_out, final_g):
    b, s, d = x.shape
    depth = w_mod.shape[0]
    assert depth == 1 and s == SEQ and d == D_MODEL
    mla_scale = (MLA_NOPE + MLA_ROPE) ** -0.5 * LOG2_E
    na_scale = NA_DIM ** -0.5 * LOG2_E
    src, sign = _rope_partner()

    w = w_in[0]
    o = np.cumsum([0, KV_LORA, MLA_ROPE, HEADS * NA_DIM, HEADS * NA_DIM, Q_LORA, HEADS * MLA_V,
                   HEADS * NA_DIM, HEADS * NA_DIM, d, d])
    seg = [w[:, o[n]:o[n + 1]] for n in range(10)]
    w_ckv, w_kr, w_kb, w_vb, w_cq, w_za, w_qb, w_zb, w_ga, w_gb = seg
    z64 = jnp.zeros((d, MLA_NOPE), F32)
    w_ext = jnp.concatenate([
        w_ckv,
        _lane_group(z64, w_kr, d),
        _lane_group(z64, w_kr[:, src] * sign, d),
        w_kb, w_cq, w_za, w_zb, w_ga, w_gb], axis=-1).astype(BF16)
    wt_ext = jnp.concatenate([w_vb, w_qb * na_scale], axis=-1).T.astype(BF16)

    wkv3 = w_ukv[0].reshape(KV_LORA, HEADS, MLA_NOPE + MLA_V)
    wk = jnp.concatenate([wkv3[:, :, :MLA_NOPE], jnp.zeros((KV_LORA, HEADS, LANES - MLA_NOPE), F32)],
                         axis=-1).reshape(KV_LORA, HEADS * LANES).astype(BF16)
    wv = wkv3[:, :, MLA_NOPE:].reshape(KV_LORA, HEADS * MLA_V).T.astype(BF16)
    wq3 = w_uq[0].reshape(Q_LORA, HEADS, MLA_NOPE + MLA_ROPE)
    zq = jnp.zeros((Q_LORA, HEADS, LANES - MLA_NOPE - MLA_ROPE), F32)
    wq = jnp.concatenate([wq3, zq], axis=-1).reshape(Q_LORA, HEADS * LANES).T.astype(BF16)
    wqr = jnp.concatenate([jnp.zeros((Q_LORA, HEADS, MLA_NOPE), F32),
                           wq3[:, :, MLA_NOPE:][:, :, src] * sign, zq],
                          axis=-1).reshape(Q_LORA, HEADS * LANES).T.astype(BF16)

    cos, sin = _rope_tables()
    ones_s = np.ones((s, MLA_NOPE), np.float32)
    zeros_s = np.zeros((s, MLA_NOPE), np.float32)
    cq_tab = jnp.asarray(np.ascontiguousarray((_np_lane_group(ones_s, cos) * np.float32(mla_scale)).T))
    sq_tab = jnp.asarray(np.ascontiguousarray((_np_lane_group(zeros_s, sin) * np.float32(mla_scale)).T))
    ck_tab = jnp.asarray(_np_lane_group(zeros_s, cos))
    sk_tab = jnp.asarray(_np_lane_group(zeros_s, sin))
    zeros_c = np.zeros((CTX_LEN, MLA_NOPE), np.float32)
    ck_ctx = jnp.asarray(_np_lane_group(zeros_c, np.ones((CTX_LEN, MLA_ROPE), np.float32)))
    sk_ctx = jnp.zeros((CTX_LEN, LANES), F32)

    cs = jnp.concatenate([c, c_ctx[None, :], jnp.zeros((8 - b - 1, d), F32)], axis=0)
    mod = _mod_call(cs, w_mod[0], b_mod[0][None, :])
    shift, scale, gate = (mod[:b, None, n * d:(n + 1) * d] for n in range(3))
    shift_c, scale_c = (mod[b:b + 1, None, n * d:(n + 1) * d] for n in range(2))

    ng = norm_g[0][None, :]
    gkv = g_ckv[0][None, :]
    k, v, kb, vb, q, qb, sza, szb, sga, sgb = _proj_call(
        x, shift, scale, True, ng, w_ext, wt_ext, gkv, wk, wv, ck_tab, sk_tab,
        q_side=(g_cq[0][None, :], wq, wqr, cq_tab, sq_tab))
    kc, vc, kcb, vcb = _proj_call(
        ctx, shift_c, scale_c, False, ng, w_ext[:, :KV_EXT], wt_ext[:_R_VB[1]], gkv, wk, wv,
        ck_ctx, sk_ctx)

    oa = _mla_call(q, k, kc, v, vc)
    ob = _na_call(qb, kb, vb, kcb, vcb, _na_bias_tables(rpb[0]))

    return _out_call(oa, ob, sza, szb, sga, sgb, x, gate,
                     w_oa[0].astype(BF16), w_ob[0].astype(BF16), w_out[0].astype(BF16),
                     final_g[None, :])
```

```python
import functools

import jax
import jax.numpy as jnp
import numpy as np
from jax import lax
from jax.experimental import pallas as pl
from jax.experimental.pallas import tpu as pltpu

D_MODEL = 1024
SEQ = 4096
GRID_W = 64
GRID_H = SEQ // GRID_W
CTX_LEN = 256
HEADS = 8
MLA_NOPE = 64
MLA_ROPE = 32
MLA_V = 64
Q_LORA = 256
KV_LORA = 128
NA_DIM = 64
NA_KR = 8
NA_KC = 16
ROPE_THETA = 10000.0
EPS = 1e-6

LANES = 128
HEAD_PAIRS = HEADS // 2
VMEM_LIMIT = 56 * 1024 * 1024

_C_CKV = (0, 128)
_C_KR = (128, 256)
_C_KRR = (256, 384)
_C_KB = (384, 896)
KV_EXT = 896
_C_CQ = (896, 1152)
_G_ZA = (0, 512)
_G_ZB = (512, 1024)
_G_GA = (1024, 2048)
_G_GB = (2048, 3072)
_R_VB = (0, 512)
_R_QB = (512, 1024)

NA_ROWS_PER_BLOCK = 4
NA_BLOCKS = GRID_H // NA_ROWS_PER_BLOCK
NA_TQ = NA_ROWS_PER_BLOCK * GRID_W
NA_SLAB_ROWS = 12
NA_SLAB = NA_SLAB_ROWS * GRID_W
NA_MASK = -1e30
LOG2_E = 1.4426950408889634

F32 = jnp.float32
BF16 = jnp.bfloat16


def _dot(a, b):
    return jnp.dot(a, b, preferred_element_type=F32)


def _dot_nt(a, b):
    return lax.dot_general(a, b, (((1,), (1,)), ((), ())), preferred_element_type=F32)


def _rms(x, g):
    return x * lax.rsqrt(jnp.mean(x * x, axis=-1, keepdims=True) + EPS) * g


def _mod_kernel(c_ref, w_ref, b_ref, o_ref):
    c = c_ref[...]
    sc = c * jax.nn.sigmoid(c)
    o_ref[...] = _dot(sc.astype(BF16), w_ref[...].astype(BF16)) + b_ref[...]


def _mod_call(cs, w_mod, b_mod):
    n = w_mod.shape[1]
    tn = 1024
    return pl.pallas_call(
        _mod_kernel,
        grid=(n // tn,),
        in_specs=[
            pl.BlockSpec((8, D_MODEL), lambda j: (0, 0)),
            pl.BlockSpec((D_MODEL, tn), lambda j: (0, j)),
            pl.BlockSpec((1, tn), lambda j: (0, j)),
        ],
        out_specs=pl.BlockSpec((8, tn), lambda j: (0, j)),
        out_shape=jax.ShapeDtypeStruct((8, n), F32),
        name="adaln_mod",
    )(cs, w_mod, b_mod)


def _proj_kernel(with_q, x_ref, shift_ref, scale_ref, ng_ref, w_ref, wt_ref, gkv_ref, wk_ref, wv_ref,
                 ck_ref, sk_ref, *rest):
    if with_q:
        (gq_ref, wq_ref, wqr_ref, cq_ref, sq_ref,
         k_ref, v_ref, kb_ref, vb_ref, q_ref, qb_ref) = rest
    else:
        k_ref, v_ref, kb_ref, vb_ref = rest

    x = x_ref[0]
    h = _rms(x, ng_ref[...]) * (1.0 + scale_ref[0]) + shift_ref[0]
    hb = h.astype(BF16)

    def proj(c):
        return _dot(hb, w_ref[:, c[0]:c[1]])

    def proj_t(r):
        return _dot_nt(wt_ref[r[0]:r[1], :], hb)

    ckvn = _rms(proj(_C_CKV), gkv_ref[...]).astype(BF16)
    if with_q:
        cqn = _rms(proj(_C_CQ), gq_ref[...]).astype(BF16)
    krope = proj(_C_KR) * ck_ref[...] + proj(_C_KRR) * sk_ref[...]
    kb_ref[0] = proj(_C_KB).astype(BF16)
    vb_ref[0] = proj_t(_R_VB).astype(BF16)
    if with_q:
        qb_ref[0] = proj_t(_R_QB).astype(BF16)

    knope = _dot(ckvn, wk_ref[...])
    for hd in range(HEADS):
        sl = slice(hd * LANES, (hd + 1) * LANES)
        k_ref[0, :, sl] = (knope[:, sl] + krope).astype(BF16)
    v_ref[0] = _dot_nt(wv_ref[...], ckvn).astype(BF16)
    if with_q:
        q1 = _dot_nt(wq_ref[...], cqn)
        q2 = _dot_nt(wqr_ref[...], cqn)
        cq = cq_ref[...]
        sq = sq_ref[...]
        for hd in range(HEADS):
            sl = slice(hd * LANES, (hd + 1) * LANES)
            q_ref[0, sl, :] = (q1[sl] * cq + q2[sl] * sq).astype(BF16)


def _proj_call(x, shift, scale, per_batch_mod, norm_g, w_ext, wt_ext, g_ckv, wk, wv, ck, sk,
               q_side=None, tm=256):
    b, s, d = x.shape
    tm = min(tm, s)
    with_q = q_side is not None
    n_ext = w_ext.shape[1]
    n_t = wt_ext.shape[0]
    mod_map = (lambda i, j: (i, 0, 0)) if per_batch_mod else (lambda i, j: (0, 0, 0))
    const2 = lambda i, j: (0, 0)
    tok3 = lambda i, j: (i, j, 0)
    tab = lambda i, j: (j, 0)

    in_specs = [
        pl.BlockSpec((1, tm, d), tok3),
        pl.BlockSpec((1, 1, d), mod_map),
        pl.BlockSpec((1, 1, d), mod_map),
        pl.BlockSpec((1, d), const2),
        pl.BlockSpec((d, n_ext), const2),
        pl.BlockSpec((n_t, d), const2),
        pl.BlockSpec((1, KV_LORA), const2),
        pl.BlockSpec((KV_LORA, HEADS * LANES), const2),
        pl.BlockSpec((HEADS * MLA_V, KV_LORA), const2),
        pl.BlockSpec((tm, LANES), tab),
        pl.BlockSpec((tm, LANES), tab),
    ]
    args = [x, shift, scale, norm_g, w_ext, wt_ext, g_ckv, wk, wv, ck, sk]
    tok_major = lambda w: (pl.BlockSpec((1, tm, w), tok3), jax.ShapeDtypeStruct((b, s, w), BF16))
    feat_major = lambda w: (pl.BlockSpec((1, w, tm), lambda i, j: (i, 0, j)),
                            jax.ShapeDtypeStruct((b, w, s), BF16))
    outs = [tok_major(HEADS * LANES), feat_major(HEADS * MLA_V), tok_major(HEADS * NA_DIM),
            feat_major(HEADS * NA_DIM)]
    if with_q:
        g_cq, wq, wqr, cq, sq = q_side
        in_specs += [
            pl.BlockSpec((1, Q_LORA), const2),
            pl.BlockSpec((HEADS * LANES, Q_LORA), const2),
            pl.BlockSpec((HEADS * LANES, Q_LORA), const2),
            pl.BlockSpec((LANES, tm), lambda i, j: (0, j)),
            pl.BlockSpec((LANES, tm), lambda i, j: (0, j)),
        ]
        args += [g_cq, wq, wqr, cq, sq]
        outs += [feat_major(HEADS * LANES), feat_major(HEADS * NA_DIM)]
    out_specs = [o[0] for o in outs]
    out_shape = [o[1] for o in outs]
    return pl.pallas_call(
        functools.partial(_proj_kernel, with_q),
        grid=(b, s // tm),
        in_specs=in_specs,
        out_specs=out_specs,
        out_shape=out_shape,
        compiler_params=pltpu.CompilerParams(vmem_limit_bytes=VMEM_LIMIT),
        name="in_proj_q" if with_q else "in_proj_ctx",
    )(*args)


BF16_SUBLANES = 16
MLA_TQ = 512
MLA_CHUNK = 512
MLA_QK_CHUNKS = 1
MLA_LOOKAHEAD = 2


def _mla_kernel(qt_ref, k_ref, kc_ref, vt_ref, vct_ref, o_ref):
    s_len = k_ref.shape[1]
    qts = [qt_ref[0, hh * LANES:(hh + 1) * LANES, :] for hh in range(2)]
    qk_keys = MLA_QK_CHUNKS * MLA_CHUNK
    groups = [(hh, k_ref, vt_ref, c * qk_keys, qk_keys) for c in range(s_len // qk_keys) for hh in range(2)]
    groups += [(hh, kc_ref, vct_ref, 0, kc_ref.shape[1]) for hh in range(2)]
    items = [(hh, vr, off + i, min(MLA_CHUNK, n - i))
             for hh, _, vr, off, n in groups for i in range(0, n, MLA_CHUNK)]

    def group_scores(g):
        hh, kr, _, off, n = g
        big = _dot(kr[0, off:off + n, hh * LANES:(hh + 1) * LANES], qts[hh]).astype(BF16)
        return [big[i:i + MLA_CHUNK] for i in range(0, n, MLA_CHUNK)]

    m = [None, None]
    acc = [None, None]
    pending = []
    gi = 0
    for hh, vr, off, n in items:
        while len(pending) <= MLA_LOOKAHEAD and gi < len(groups):
            pending.extend(group_scores(groups[gi]))
            gi += 1
        s = pending.pop(0)
        mc = jnp.max(s, axis=0, keepdims=True)
        m_new = mc if m[hh] is None else jnp.maximum(m[hh], mc)
        p = jnp.exp2(s - m_new)
        v_ones = jnp.concatenate([vr[0, hh * MLA_V:(hh + 1) * MLA_V, off:off + n],
                                  jnp.ones((BF16_SUBLANES, n), BF16)], axis=0)
        pv = _dot(v_ones, p)
        if acc[hh] is None:
            acc[hh] = pv
        else:
            acc[hh] = acc[hh] * jnp.exp2(m[hh].astype(F32) - m_new.astype(F32)) + pv
        m[hh] = m_new
    outs = [a[:MLA_V] / a[MLA_V:MLA_V + 1] for a in acc]
    o_ref[0] = jnp.concatenate(outs, axis=0).T.astype(BF16)


def _mla_call(qt, k, kc, vt, vct):
    b, _, s = qt.shape
    t = kc.shape[1]
    return pl.pallas_call(
        _mla_kernel,
        grid=(b, HEAD_PAIRS, s // MLA_TQ),
        in_specs=[
            pl.BlockSpec((1, 2 * LANES, MLA_TQ), lambda i, p, j: (i, p, j)),
            pl.BlockSpec((1, s, 2 * LANES), lambda i, p, j: (i, 0, p)),
            pl.BlockSpec((1, t, 2 * LANES), lambda i, p, j: (i, 0, p)),
            pl.BlockSpec((1, 2 * MLA_V, s), lambda i, p, j: (i, p, 0)),
            pl.BlockSpec((1, 2 * MLA_V, t), lambda i, p, j: (i, p, 0)),
        ],
        out_specs=pl.BlockSpec((1, MLA_TQ, LANES), lambda i, p, j: (i, j, p)),
        out_shape=jax.ShapeDtypeStruct((b, s, HEADS * MLA_V), BF16),
        compiler_params=pltpu.CompilerParams(vmem_limit_bytes=VMEM_LIMIT),
        name="mla_attention",
    )(qt, k, kc, vt, vct)


NA_HEADS_PER_STEP = 8
NA_LOOKAHEAD = 3


def _na_slab_start(blk):
    return jnp.clip(blk * NA_ROWS_PER_BLOCK - NA_KR // 2, 0, GRID_H - NA_SLAB_ROWS)


def _na_variant(blk):
    return jnp.where(blk == 0, 0, jnp.where(blk == NA_BLOCKS - 1, 2, 1))


def _with_ones(v):
    return jnp.concatenate([v, jnp.ones((BF16_SUBLANES, v.shape[1]), v.dtype)], axis=0)


def _na_kernel(qt_ref, k_ref, vt_ref, kc_ref, vct_ref, bias_ref, o_ref):
    blk = pl.program_id(2)
    start = pl.multiple_of(_na_slab_start(blk) * GRID_W, NA_TQ)
    row_half = lax.broadcasted_iota(jnp.int32, (LANES, NA_TQ), 0) // NA_DIM

    k_all = [jnp.concatenate([k_ref[0, pl.ds(start, NA_SLAB), pair * LANES:(pair + 1) * LANES],
                              kc_ref[0, :, pair * LANES:(pair + 1) * LANES]], axis=0)
             for pair in range(NA_HEADS_PER_STEP // 2)]

    def scores(h):
        pair, hh = divmod(h, 2)
        qt = qt_ref[0, pair * LANES:(pair + 1) * LANES, :]
        qm = jnp.where(row_half == hh, qt, jnp.zeros_like(qt))
        s = _dot(k_all[pair], qm).astype(BF16)
        return jnp.concatenate([s[:NA_SLAB] + bias_ref[0, h], s[NA_SLAB:]], axis=0)

    outs = []
    ahead = [scores(h) for h in range(NA_LOOKAHEAD)]
    for h in range(NA_HEADS_PER_STEP):
        if h + NA_LOOKAHEAD < NA_HEADS_PER_STEP:
            ahead.append(scores(h + NA_LOOKAHEAD))
        s = ahead.pop(0)
        p = jnp.exp2(s - jnp.max(s, axis=0, keepdims=True))
        rows = slice(h * NA_DIM, (h + 1) * NA_DIM)
        v_all = jnp.concatenate([vt_ref[0, rows, pl.ds(start, NA_SLAB)], vct_ref[0, rows, :]], axis=1)
        pv = _dot(_with_ones(v_all), p)
        outs.append(pv[:NA_DIM] / pv[NA_DIM:NA_DIM + 1])
    o_ref[0] = jnp.concatenate(outs, axis=0).T.astype(BF16)


def _na_call(qbt, kb, vbt, kcb, vcbt, bias):
    b, s, _ = kb.shape
    t = kcb.shape[1]
    wide = NA_HEADS_PER_STEP * NA_DIM
    return pl.pallas_call(
        _na_kernel,
        grid=(HEADS // NA_HEADS_PER_STEP, b, NA_BLOCKS),
        in_specs=[
            pl.BlockSpec((1, wide, NA_TQ), lambda g, i, j: (i, g, j)),
            pl.BlockSpec((1, s, wide), lambda g, i, j: (i, 0, g)),
            pl.BlockSpec((1, wide, s), lambda g, i, j: (i, g, 0)),
            pl.BlockSpec((1, t, wide), lambda g, i, j: (i, 0, g)),
            pl.BlockSpec((1, wide, t), lambda g, i, j: (i, g, 0)),
            pl.BlockSpec((1, NA_HEADS_PER_STEP, NA_SLAB, NA_TQ),
                         lambda g, i, j: (_na_variant(j), g, 0, 0)),
        ],
        out_specs=pl.BlockSpec((1, NA_TQ, wide), lambda g, i, j: (i, j, g)),
        out_shape=jax.ShapeDtypeStruct((b, s, HEADS * NA_DIM), BF16),
        compiler_params=pltpu.CompilerParams(vmem_limit_bytes=VMEM_LIMIT),
        name="na_attention",
    )(qbt, kb, vbt, kcb, vcbt, bias)


def _na_bias_kernel(rpb_ref, o_ref):
    rp = pltpu.roll(rpb_ref[0] * LOG2_E, LANES - (NA_KC - 1), axis=1)
    shape = (GRID_W, LANES)
    wk = lax.broadcasted_iota(jnp.int32, shape, 0)
    lane = lax.broadcasted_iota(jnp.int32, shape, 1)
    c_start = jnp.clip(lane - NA_KC // 2, 0, GRID_W - NA_KC)
    left = lane < GRID_W
    mask_l = jnp.where(left, NA_MASK, 0.0).astype(F32)
    mask_r = jnp.where(left, 0.0, NA_MASK).astype(F32)
    tiles_l, tiles_r = [], []
    for d in range(2 * NA_KR - 1):
        row = jnp.broadcast_to(rp[d:d + 1, :], shape)
        t = pltpu.roll(row, 0, axis=1, stride=1, stride_axis=0)
        t = jnp.where(wk >= c_start, jnp.where(wk < c_start + NA_KC, t, NA_MASK), NA_MASK)
        t = jnp.where(left, t, 0.0)
        tiles_l.append(t)
        tiles_r.append(pltpu.roll(t, GRID_W, axis=1))
    for n, blk in enumerate((0, 1, NA_BLOCKS - 1)):
        sb = min(max(blk * NA_ROWS_PER_BLOCK - NA_KR // 2, 0), GRID_H - NA_SLAB_ROWS)
        for j in range(NA_SLAB_ROWS):
            kr = sb + j

            def pick(i, tiles, masked):
                r = blk * NA_ROWS_PER_BLOCK + i
                r_start = min(max(r - NA_KR // 2, 0), GRID_H - NA_KR)
                if r_start <= kr < r_start + NA_KR:
                    return tiles[kr - r + NA_KR - 1]
                return masked

            for ii in range(NA_ROWS_PER_BLOCK // 2):
                o_ref[n, 0, j * GRID_W:(j + 1) * GRID_W, ii * LANES:(ii + 1) * LANES] = (
                    pick(2 * ii, tiles_l, mask_l) + pick(2 * ii + 1, tiles_r, mask_r)).astype(BF16)


def _na_bias_tables(rpb):
    rpb_p = jnp.pad(rpb[:, :, ::-1], ((0, 0), (0, 1), (0, LANES - (2 * NA_KC - 1))))
    return pl.pallas_call(
        _na_bias_kernel,
        grid=(HEADS,),
        in_specs=[pl.BlockSpec((1, 2 * NA_KR, LANES), lambda h: (h, 0, 0))],
        out_specs=pl.BlockSpec((3, 1, NA_SLAB, NA_TQ), lambda h: (0, h, 0, 0)),
        out_shape=jax.ShapeDtypeStruct((3, HEADS, NA_SLAB, NA_TQ), BF16),
        name="na_bias_table",
    )(rpb_p)


def _out_kernel(oa_ref, ob_ref, x_ref, shift_ref, scale_ref, gate_ref, ng_ref, wg_ref,
                woa_ref, wob_ref, wout_ref, fg_ref, o_ref):
    subs = [slice(i, i + OUT_SUB) for i in range(0, o_ref.shape[1], OUT_SUB)]

    def gates(sl):
        h = _rms(x_ref[0, sl], ng_ref[...]) * (1.0 + scale_ref[0]) + shift_ref[0]
        hb = h.astype(BF16)
        proj = lambda c: _dot(hb, wg_ref[:, c[0]:c[1]])
        za = proj(_G_ZA)
        zb = proj(_G_ZB)
        return ((za * jax.nn.sigmoid(za)).astype(BF16), (zb * jax.nn.sigmoid(zb)).astype(BF16),
                jax.nn.sigmoid(proj(_G_GA)).astype(BF16), jax.nn.sigmoid(proj(_G_GB)).astype(BF16))

    def branches(sl, g):
        sza, szb, sga, sgb = g
        ya = _dot(oa_ref[0, sl] * sza, woa_ref[...])
        yb = _dot(ob_ref[0, sl] * szb, wob_ref[...])
        return (sga.astype(F32) * ya + sgb.astype(F32) * yb).astype(BF16)

    def finish(sl, mix):
        y = _dot(mix, wout_ref[...])
        r = x_ref[0, sl] + gate_ref[0] * y
        o_ref[0, sl] = _rms(r, fg_ref[...])

    order = sorted([(3 * i + off, stage, i) for i in range(len(subs))
                    for stage, off in (("gates", 0), ("branches", 4), ("finish", 8))])
    g, mix = {}, {}
    for _, stage, i in order:
        if stage == "gates":
            g[i] = gates(subs[i])
        elif stage == "branches":
            mix[i] = branches(subs[i], g.pop(i))
        else:
            finish(subs[i], mix.pop(i))


OUT_SUB = 256


def _out_call(oa, ob, x, shift, scale, gate, norm_g, w_gate, w_oa, w_ob, w_out, final_g, tm=1024):
    b, s, d = x.shape
    tok3 = lambda i, j: (i, j, 0)
    mod3 = lambda i, j: (i, 0, 0)
    half = HEADS * MLA_V

    def const(shape):
        return pl.BlockSpec(shape, lambda i, j: (0, 0), pipeline_mode=pl.Buffered(1))

    return pl.pallas_call(
        _out_kernel,
        grid=(b, s // tm),
        in_specs=[
            pl.BlockSpec((1, tm, half), tok3),
            pl.BlockSpec((1, tm, half), tok3),
            pl.BlockSpec((1, tm, d), tok3),
            pl.BlockSpec((1, 1, d), mod3),
            pl.BlockSpec((1, 1, d), mod3),
            pl.BlockSpec((1, 1, d), mod3),
            const((1, d)),
            const((d, w_gate.shape[1])),
            const((half, d)),
            const((half, d)),
            const((d, d)),
            const((1, d)),
        ],
        out_specs=pl.BlockSpec((1, tm, d), tok3),
        out_shape=jax.ShapeDtypeStruct((b, s, d), F32),
        compiler_params=pltpu.CompilerParams(vmem_limit_bytes=VMEM_LIMIT),
        name="gated_out",
    )(oa, ob, x, shift, scale, gate, norm_g, w_gate, w_oa, w_ob, w_out, final_g)


def _rope_partner():
    j = np.arange(MLA_ROPE)
    first = (j % 16) < 8
    src = np.where(first, j + 8, j - 8)
    sign = np.where(first, -1.0, 1.0).astype(np.float32)
    return src, sign


def _rope_tables():
    t = np.arange(SEQ)
    half = MLA_ROPE // 4
    freqs = (ROPE_THETA ** (-np.arange(half, dtype=np.float32) / half)).astype(np.float32)
    ang_r = (t // GRID_W).astype(np.float32)[:, None] * freqs[None, :]
    ang_c = (t % GRID_W).astype(np.float32)[:, None] * freqs[None, :]
    cos = np.concatenate([np.cos(ang_r)] * 2 + [np.cos(ang_c)] * 2, axis=-1).astype(np.float32)
    sin = np.concatenate([np.sin(ang_r)] * 2 + [np.sin(ang_c)] * 2, axis=-1).astype(np.float32)
    return cos, sin


def _np_lane_group(nope, rope):
    pad = np.zeros((rope.shape[0], LANES - MLA_NOPE - MLA_ROPE), np.float32)
    return np.concatenate([nope, rope, pad], axis=-1).astype(np.float32)


def _lane_group(nope, rope, n):
    pad = jnp.zeros((n, LANES - MLA_NOPE - MLA_ROPE), F32)
    return jnp.concatenate([nope, rope, pad], axis=-1)


def kernel(x, c, ctx, c_ctx, w_mod, b_mod, norm_g, w_in, g_cq, w_uq, g_ckv, w_ukv, rpb,
           w_oa, w_ob, w_out, final_g):
    b, s, d = x.shape
    depth = w_mod.shape[0]
    assert depth == 1 and s == SEQ and d == D_MODEL
    mla_scale = (MLA_NOPE + MLA_ROPE) ** -0.5 * LOG2_E
    na_scale = NA_DIM ** -0.5 * LOG2_E
    src, sign = _rope_partner()

    w = w_in[0]
    o = np.cumsum([0, KV_LORA, MLA_ROPE, HEADS * NA_DIM, HEADS * NA_DIM, Q_LORA, HEADS * MLA_V,
                   HEADS * NA_DIM, HEADS * NA_DIM, d, d])
    seg = [w[:, o[n]:o[n + 1]] for n in range(10)]
    w_ckv, w_kr, w_kb, w_vb, w_cq, w_za, w_qb, w_zb, w_ga, w_gb = seg
    z64 = jnp.zeros((d, MLA_NOPE), F32)
    w_ext = jnp.concatenate([
        w_ckv,
        _lane_group(z64, w_kr, d),
        _lane_group(z64, w_kr[:, src] * sign, d),
        w_kb, w_cq], axis=-1).astype(BF16)
    wt_ext = jnp.concatenate([w_vb, w_qb * na_scale], axis=-1).T.astype(BF16)
    w_gate = jnp.concatenate([w_za, w_zb, w_ga, w_gb], axis=-1).astype(BF16)

    wkv3 = w_ukv[0].reshape(KV_LORA, HEADS, MLA_NOPE + MLA_V)
    wk = jnp.concatenate([wkv3[:, :, :MLA_NOPE], jnp.zeros((KV_LORA, HEADS, LANES - MLA_NOPE), F32)],
                         axis=-1).reshape(KV_LORA, HEADS * LANES).astype(BF16)
    wv = wkv3[:, :, MLA_NOPE:].reshape(KV_LORA, HEADS * MLA_V).T.astype(BF16)
    wq3 = w_uq[0].reshape(Q_LORA, HEADS, MLA_NOPE + MLA_ROPE)
    zq = jnp.zeros((Q_LORA, HEADS, LANES - MLA_NOPE - MLA_ROPE), F32)
    wq = jnp.concatenate([wq3, zq], axis=-1).reshape(Q_LORA, HEADS * LANES).T.astype(BF16)
    wqr = jnp.concatenate([jnp.zeros((Q_LORA, HEADS, MLA_NOPE), F32),
                           wq3[:, :, MLA_NOPE:][:, :, src] * sign, zq],
                          axis=-1).reshape(Q_LORA, HEADS * LANES).T.astype(BF16)

    cos, sin = _rope_tables()
    ones_s = np.ones((s, MLA_NOPE), np.float32)
    zeros_s = np.zeros((s, MLA_NOPE), np.float32)
    cq_tab = jnp.asarray(np.ascontiguousarray((_np_lane_group(ones_s, cos) * np.float32(mla_scale)).T))
    sq_tab = jnp.asarray(np.ascontiguousarray((_np_lane_group(zeros_s, sin) * np.float32(mla_scale)).T))
    ck_tab = jnp.asarray(_np_lane_group(zeros_s, cos))
    sk_tab = jnp.asarray(_np_lane_group(zeros_s, sin))
    zeros_c = np.zeros((CTX_LEN, MLA_NOPE), np.float32)
    ck_ctx = jnp.asarray(_np_lane_group(zeros_c, np.ones((CTX_LEN, MLA_ROPE), np.float32)))
    sk_ctx = jnp.zeros((CTX_LEN, LANES), F32)

    cs = jnp.concatenate([c, c_ctx[None, :], jnp.zeros((8 - b - 1, d), F32)], axis=0)
    mod = _mod_call(cs, w_mod[0], b_mod[0][None, :])
    shift, scale, gate = (mod[:b, None, n * d:(n + 1) * d] for n in range(3))
    shift_c, scale_c = (mod[b:b + 1, None, n * d:(n + 1) * d] for n in range(2))

    ng = norm_g[0][None, :]
    gkv = g_ckv[0][None, :]
    k, v, kb, vb, q, qb = _proj_call(
        x, shift, scale, True, ng, w_ext, wt_ext, gkv, wk, wv, ck_tab, sk_tab,
        q_side=(g_cq[0][None, :], wq, wqr, cq_tab, sq_tab))
    kc, vc, kcb, vcb = _proj_call(
        ctx, shift_c, scale_c, False, ng, w_ext[:, :KV_EXT], wt_ext[:_R_VB[1]], gkv, wk, wv,
        ck_ctx, sk_ctx)

    oa = _mla_call(q, k, kc, v, vc)
    ob = _na_call(qb, kb, vb, kcb, vcb, _na_bias_tables(rpb[0]))

    return _out_call(oa, ob, x, shift, scale, gate, ng, w_gate,
                     w_oa[0].astype(BF16), w_ob[0].astype(BF16), w_out[0].astype(BF16),
                     final_g[None, :])
```

```python
import functools

import jax
import jax.numpy as jnp
import numpy as np
from jax import lax
from jax.experimental import pallas as pl
from jax.experimental.pallas import tpu as pltpu

D_MODEL = 1024
SEQ = 4096
GRID_W = 64
GRID_H = SEQ // GRID_W
CTX_LEN = 256
HEADS = 8
MLA_NOPE = 64
MLA_ROPE = 32
MLA_V = 64
Q_LORA = 256
KV_LORA = 128
NA_DIM = 64
NA_KR = 8
NA_KC = 16
ROPE_THETA = 10000.0
EPS = 1e-6

LANES = 128
HEAD_PAIRS = HEADS // 2
VMEM_LIMIT = 56 * 1024 * 1024

_C_CKV = (0, 128)
_C_KR = (128, 256)
_C_KRR = (256, 384)
_C_KB = (384, 896)
KV_EXT = 896
_C_CQ = (896, 1152)
_G_ZA = (0, 512)
_G_ZB = (512, 1024)
_G_GA = (1024, 2048)
_G_GB = (2048, 3072)
_R_VB = (0, 512)
_R_QB = (512, 1024)

NA_ROWS_PER_BLOCK = 4
NA_BLOCKS = GRID_H // NA_ROWS_PER_BLOCK
NA_TQ = NA_ROWS_PER_BLOCK * GRID_W
NA_SLAB_ROWS = 12
NA_SLAB = NA_SLAB_ROWS * GRID_W
NA_MASK = -1e30
LOG2_E = 1.4426950408889634

F32 = jnp.float32
BF16 = jnp.bfloat16


def _dot(a, b):
    return jnp.dot(a, b, preferred_element_type=F32)


def _dot_nt(a, b):
    return lax.dot_general(a, b, (((1,), (1,)), ((), ())), preferred_element_type=F32)


def _rms(x, g):
    return x * lax.rsqrt(jnp.mean(x * x, axis=-1, keepdims=True) + EPS) * g


def _mod_kernel(c_ref, w_ref, b_ref, o_ref):
    c = c_ref[...]
    sc = c * jax.nn.sigmoid(c)
    o_ref[0, :, 0, :] = _dot(sc.astype(BF16), w_ref[...].astype(BF16)) + b_ref[...]


def _mod_call(cs, w_mod, b_mod):
    n = w_mod.shape[1]
    tn = 1024
    return pl.pallas_call(
        _mod_kernel,
        grid=(n // tn,),
        in_specs=[
            pl.BlockSpec((8, D_MODEL), lambda j: (0, 0)),
            pl.BlockSpec((D_MODEL, tn), lambda j: (0, j)),
            pl.BlockSpec((1, tn), lambda j: (0, j)),
        ],
        out_specs=pl.BlockSpec((1, 8, 1, tn), lambda j: (j, 0, 0, 0)),
        out_shape=jax.ShapeDtypeStruct((n // tn, 8, 1, tn), F32),
        name="adaln_mod",
    )(cs, w_mod, b_mod)


def _proj_kernel(with_q, x_ref, shift_ref, scale_ref, ng_ref, w_ref, wt_ref, gkv_ref, wk_ref, wv_ref,
                 ck_ref, sk_ref, *rest):
    if with_q:
        (gq_ref, wq_ref, wqr_ref, cq_ref, sq_ref,
         k_ref, v_ref, kb_ref, vb_ref, q_ref, qb_ref) = rest
    else:
        k_ref, v_ref, kb_ref, vb_ref = rest

    x = x_ref[0]
    h = _rms(x, ng_ref[...]) * (1.0 + scale_ref[0, 0]) + shift_ref[0, 0]
    hb = h.astype(BF16)

    def proj(c):
        return _dot(hb, w_ref[:, c[0]:c[1]])

    def proj_t(r):
        return _dot_nt(wt_ref[r[0]:r[1], :], hb)

    ckvn = _rms(proj(_C_CKV), gkv_ref[...]).astype(BF16)
    if with_q:
        cqn = _rms(proj(_C_CQ), gq_ref[...]).astype(BF16)
    krope = proj(_C_KR) * ck_ref[...] + proj(_C_KRR) * sk_ref[...]
    kb_ref[0] = proj(_C_KB).astype(BF16)
    vb_ref[0] = proj_t(_R_VB).astype(BF16)
    if with_q:
        qb_ref[0] = proj_t(_R_QB).astype(BF16)

    knope = _dot(ckvn, wk_ref[...])
    for hd in range(HEADS):
        sl = slice(hd * LANES, (hd + 1) * LANES)
        k_ref[0, :, sl] = (knope[:, sl] + krope).astype(BF16)
    v_ref[0] = _dot_nt(wv_ref[...], ckvn).astype(BF16)
    if with_q:
        q1 = _dot_nt(wq_ref[...], cqn)
        q2 = _dot_nt(wqr_ref[...], cqn)
        cq = cq_ref[...]
        sq = sq_ref[...]
        for hd in range(HEADS):
            sl = slice(hd * LANES, (hd + 1) * LANES)
            q_ref[0, sl, :] = (q1[sl] * cq + q2[sl] * sq).astype(BF16)


def _mod_spec(which, row=None):
    d = D_MODEL
    if row is None:
        return pl.BlockSpec((1, 1, 1, d), lambda i, j: (which, i, 0, 0))
    return pl.BlockSpec((1, 1, 1, d), lambda i, j: (which, row, 0, 0))


MOD_SHIFT, MOD_SCALE, MOD_GATE = 0, 1, 2


def _proj_call(x, mod, mod_row, norm_g, w_ext, wt_ext, g_ckv, wk, wv, ck, sk,
               q_side=None, tm=256):
    b, s, d = x.shape
    tm = min(tm, s)
    with_q = q_side is not None
    n_ext = w_ext.shape[1]
    n_t = wt_ext.shape[0]
    const2 = lambda i, j: (0, 0)
    tok3 = lambda i, j: (i, j, 0)
    tab = lambda i, j: (j, 0)

    in_specs = [
        pl.BlockSpec((1, tm, d), tok3),
        _mod_spec(MOD_SHIFT, mod_row),
        _mod_spec(MOD_SCALE, mod_row),
        pl.BlockSpec((1, d), const2),
        pl.BlockSpec((d, n_ext), const2),
        pl.BlockSpec((n_t, d), const2),
        pl.BlockSpec((1, KV_LORA), const2),
        pl.BlockSpec((KV_LORA, HEADS * LANES), const2),
        pl.BlockSpec((HEADS * MLA_V, KV_LORA), const2),
        pl.BlockSpec((tm, LANES), tab),
        pl.BlockSpec((tm, LANES), tab),
    ]
    args = [x, mod, mod, norm_g, w_ext, wt_ext, g_ckv, wk, wv, ck, sk]
    tok_major = lambda w: (pl.BlockSpec((1, tm, w), tok3), jax.ShapeDtypeStruct((b, s, w), BF16))
    feat_major = lambda w: (pl.BlockSpec((1, w, tm), lambda i, j: (i, 0, j)),
                            jax.ShapeDtypeStruct((b, w, s), BF16))
    outs = [tok_major(HEADS * LANES), feat_major(HEADS * MLA_V), tok_major(HEADS * NA_DIM),
            feat_major(HEADS * NA_DIM)]
    if with_q:
        g_cq, wq, wqr, cq, sq = q_side
        in_specs += [
            pl.BlockSpec((1, Q_LORA), const2),
            pl.BlockSpec((HEADS * LANES, Q_LORA), const2),
            pl.BlockSpec((HEADS * LANES, Q_LORA), const2),
            pl.BlockSpec((LANES, tm), lambda i, j: (0, j)),
            pl.BlockSpec((LANES, tm), lambda i, j: (0, j)),
        ]
        args += [g_cq, wq, wqr, cq, sq]
        outs += [feat_major(HEADS * LANES), feat_major(HEADS * NA_DIM)]
    out_specs = [o[0] for o in outs]
    out_shape = [o[1] for o in outs]
    return pl.pallas_call(
        functools.partial(_proj_kernel, with_q),
        grid=(b, s // tm),
        in_specs=in_specs,
        out_specs=out_specs,
        out_shape=out_shape,
        compiler_params=pltpu.CompilerParams(vmem_limit_bytes=VMEM_LIMIT),
        name="in_proj_q" if with_q else "in_proj_ctx",
    )(*args)


BF16_SUBLANES = 16
MLA_TQ = 512
MLA_CHUNK = 512
MLA_QK_CHUNKS = 1
MLA_LOOKAHEAD = 2


def _mla_kernel(qt_ref, k_ref, kc_ref, vt_ref, vct_ref, o_ref):
    s_len = k_ref.shape[1]
    qts = [qt_ref[0, hh * LANES:(hh + 1) * LANES, :] for hh in range(2)]
    qk_keys = MLA_QK_CHUNKS * MLA_CHUNK
    groups = [(hh, k_ref, vt_ref, c * qk_keys, qk_keys) for c in range(s_len // qk_keys) for hh in range(2)]
    groups += [(hh, kc_ref, vct_ref, 0, kc_ref.shape[1]) for hh in range(2)]
    items = [(hh, vr, off + i, min(MLA_CHUNK, n - i))
             for hh, _, vr, off, n in groups for i in range(0, n, MLA_CHUNK)]

    def group_scores(g):
        hh, kr, _, off, n = g
        big = _dot(kr[0, off:off + n, hh * LANES:(hh + 1) * LANES], qts[hh]).astype(BF16)
        return [big[i:i + MLA_CHUNK] for i in range(0, n, MLA_CHUNK)]

    m = [None, None]
    acc = [None, None]
    pending = []
    gi = 0
    for hh, vr, off, n in items:
        while len(pending) <= MLA_LOOKAHEAD and gi < len(groups):
            pending.extend(group_scores(groups[gi]))
            gi += 1
        s = pending.pop(0)
        mc = jnp.max(s, axis=0, keepdims=True)
        m_new = mc if m[hh] is None else jnp.maximum(m[hh], mc)
        p = jnp.exp2(s - m_new)
        v_ones = jnp.concatenate([vr[0, hh * MLA_V:(hh + 1) * MLA_V, off:off + n],
                                  jnp.ones((BF16_SUBLANES, n), BF16)], axis=0)
        pv = _dot(v_ones, p)
        if acc[hh] is None:
            acc[hh] = pv
        else:
            acc[hh] = acc[hh] * jnp.exp2(m[hh].astype(F32) - m_new.astype(F32)) + pv
        m[hh] = m_new
    outs = [a[:MLA_V] / a[MLA_V:MLA_V + 1] for a in acc]
    o_ref[0] = jnp.concatenate(outs, axis=0).T.astype(BF16)


def _mla_call(qt, k, kc, vt, vct):
    b, _, s = qt.shape
    t = kc.shape[1]
    return pl.pallas_call(
        _mla_kernel,
        grid=(b, HEAD_PAIRS, s // MLA_TQ),
        in_specs=[
            pl.BlockSpec((1, 2 * LANES, MLA_TQ), lambda i, p, j: (i, p, j)),
            pl.BlockSpec((1, s, 2 * LANES), lambda i, p, j: (i, 0, p)),
            pl.BlockSpec((1, t, 2 * LANES), lambda i, p, j: (i, 0, p)),
            pl.BlockSpec((1, 2 * MLA_V, s), lambda i, p, j: (i, p, 0)),
            pl.BlockSpec((1, 2 * MLA_V, t), lambda i, p, j: (i, p, 0)),
        ],
        out_specs=pl.BlockSpec((1, MLA_TQ, LANES), lambda i, p, j: (i, j, p)),
        out_shape=jax.ShapeDtypeStruct((b, s, HEADS * MLA_V), BF16),
        compiler_params=pltpu.CompilerParams(vmem_limit_bytes=VMEM_LIMIT),
        name="mla_attention",
    )(qt, k, kc, vt, vct)


NA_HEADS_PER_STEP = 8
NA_LOOKAHEAD = 3


def _na_slab_start(blk):
    return jnp.clip(blk * NA_ROWS_PER_BLOCK - NA_KR // 2, 0, GRID_H - NA_SLAB_ROWS)


def _na_variant(blk):
    return jnp.where(blk == 0, 0, jnp.where(blk == NA_BLOCKS - 1, 2, 1))


def _with_ones(v):
    return jnp.concatenate([v, jnp.ones((BF16_SUBLANES, v.shape[1]), v.dtype)], axis=0)


def _na_kernel(qt_ref, k_ref, vt_ref, kc_ref, vct_ref, bias_ref, o_ref):
    blk = pl.program_id(2)
    start = pl.multiple_of(_na_slab_start(blk) * GRID_W, NA_TQ)
    row_half = lax.broadcasted_iota(jnp.int32, (LANES, NA_TQ), 0) // NA_DIM

    k_all = [jnp.concatenate([k_ref[0, pl.ds(start, NA_SLAB), pair * LANES:(pair + 1) * LANES],
                              kc_ref[0, :, pair * LANES:(pair + 1) * LANES]], axis=0)
             for pair in range(NA_HEADS_PER_STEP // 2)]

    def scores(h):
        pair, hh = divmod(h, 2)
        qt = qt_ref[0, pair * LANES:(pair + 1) * LANES, :]
        qm = jnp.where(row_half == hh, qt, jnp.zeros_like(qt))
        s = _dot(k_all[pair], qm).astype(BF16)
        return jnp.concatenate([s[:NA_SLAB] + bias_ref[0, h], s[NA_SLAB:]], axis=0)

    outs = []
    ahead = [scores(h) for h in range(NA_LOOKAHEAD)]
    for h in range(NA_HEADS_PER_STEP):
        if h + NA_LOOKAHEAD < NA_HEADS_PER_STEP:
            ahead.append(scores(h + NA_LOOKAHEAD))
        s = ahead.pop(0)
        p = jnp.exp2(s - jnp.max(s, axis=0, keepdims=True))
        rows = slice(h * NA_DIM, (h + 1) * NA_DIM)
        v_all = jnp.concatenate([vt_ref[0, rows, pl.ds(start, NA_SLAB)], vct_ref[0, rows, :]], axis=1)
        pv = _dot(_with_ones(v_all), p)
        outs.append(pv[:NA_DIM] / pv[NA_DIM:NA_DIM + 1])
    o_ref[0] = jnp.concatenate(outs, axis=0).T.astype(BF16)


def _na_call(qbt, kb, vbt, kcb, vcbt, bias):
    b, s, _ = kb.shape
    t = kcb.shape[1]
    wide = NA_HEADS_PER_STEP * NA_DIM
    return pl.pallas_call(
        _na_kernel,
        grid=(HEADS // NA_HEADS_PER_STEP, b, NA_BLOCKS),
        in_specs=[
            pl.BlockSpec((1, wide, NA_TQ), lambda g, i, j: (i, g, j)),
            pl.BlockSpec((1, s, wide), lambda g, i, j: (i, 0, g)),
            pl.BlockSpec((1, wide, s), lambda g, i, j: (i, g, 0)),
            pl.BlockSpec((1, t, wide), lambda g, i, j: (i, 0, g)),
            pl.BlockSpec((1, wide, t), lambda g, i, j: (i, g, 0)),
            pl.BlockSpec((1, NA_HEADS_PER_STEP, NA_SLAB, NA_TQ),
                         lambda g, i, j: (_na_variant(j), g, 0, 0)),
        ],
        out_specs=pl.BlockSpec((1, NA_TQ, wide), lambda g, i, j: (i, j, g)),
        out_shape=jax.ShapeDtypeStruct((b, s, HEADS * NA_DIM), BF16),
        compiler_params=pltpu.CompilerParams(vmem_limit_bytes=VMEM_LIMIT),
        name="na_attention",
    )(qbt, kb, vbt, kcb, vcbt, bias)


def _na_bias_kernel(rpb_ref, o_ref):
    rp = pltpu.roll(rpb_ref[0] * LOG2_E, LANES - (NA_KC - 1), axis=1)
    shape = (GRID_W, LANES)
    wk = lax.broadcasted_iota(jnp.int32, shape, 0)
    lane = lax.broadcasted_iota(jnp.int32, shape, 1)
    c_start = jnp.clip(lane - NA_KC // 2, 0, GRID_W - NA_KC)
    left = lane < GRID_W
    mask_l = jnp.where(left, NA_MASK, 0.0).astype(F32)
    mask_r = jnp.where(left, 0.0, NA_MASK).astype(F32)
    tiles_l, tiles_r = [], []
    for d in range(2 * NA_KR - 1):
        row = jnp.broadcast_to(rp[d:d + 1, :], shape)
        t = pltpu.roll(row, 0, axis=1, stride=1, stride_axis=0)
        t = jnp.where(wk >= c_start, jnp.where(wk < c_start + NA_KC, t, NA_MASK), NA_MASK)
        t = jnp.where(left, t, 0.0)
        tiles_l.append(t)
        tiles_r.append(pltpu.roll(t, GRID_W, axis=1))
    for n, blk in enumerate((0, 1, NA_BLOCKS - 1)):
        sb = min(max(blk * NA_ROWS_PER_BLOCK - NA_KR // 2, 0), GRID_H - NA_SLAB_ROWS)
        for j in range(NA_SLAB_ROWS):
            kr = sb + j

            def pick(i, tiles, masked):
                r = blk * NA_ROWS_PER_BLOCK + i
                r_start = min(max(r - NA_KR // 2, 0), GRID_H - NA_KR)
                if r_start <= kr < r_start + NA_KR:
                    return tiles[kr - r + NA_KR - 1]
                return masked

            for ii in range(NA_ROWS_PER_BLOCK // 2):
                o_ref[n, 0, j * GRID_W:(j + 1) * GRID_W, ii * LANES:(ii + 1) * LANES] = (
                    pick(2 * ii, tiles_l, mask_l) + pick(2 * ii + 1, tiles_r, mask_r)).astype(BF16)


def _na_bias_tables(rpb):
    rpb_p = jnp.pad(rpb[:, :, ::-1], ((0, 0), (0, 1), (0, LANES - (2 * NA_KC - 1))))
    return pl.pallas_call(
        _na_bias_kernel,
        grid=(HEADS,),
        in_specs=[pl.BlockSpec((1, 2 * NA_KR, LANES), lambda h: (h, 0, 0))],
        out_specs=pl.BlockSpec((3, 1, NA_SLAB, NA_TQ), lambda h: (0, h, 0, 0)),
        out_shape=jax.ShapeDtypeStruct((3, HEADS, NA_SLAB, NA_TQ), BF16),
        name="na_bias_table",
    )(rpb_p)


def _out_kernel(oa_ref, ob_ref, x_ref, shift_ref, scale_ref, gate_ref, ng_ref, wg_ref,
                woa_ref, wob_ref, wout_ref, fg_ref, o_ref):
    subs = [slice(i, i + OUT_SUB) for i in range(0, o_ref.shape[1], OUT_SUB)]

    def gates(sl):
        h = _rms(x_ref[0, sl], ng_ref[...]) * (1.0 + scale_ref[0, 0]) + shift_ref[0, 0]
        hb = h.astype(BF16)
        proj = lambda c: _dot(hb, wg_ref[:, c[0]:c[1]])
        za = proj(_G_ZA)
        zb = proj(_G_ZB)
        return ((za * jax.nn.sigmoid(za)).astype(BF16), (zb * jax.nn.sigmoid(zb)).astype(BF16),
                jax.nn.sigmoid(proj(_G_GA)).astype(BF16), jax.nn.sigmoid(proj(_G_GB)).astype(BF16))

    def branches(sl, g):
        sza, szb, sga, sgb = g
        ya = _dot(oa_ref[0, sl] * sza, woa_ref[...])
        yb = _dot(ob_ref[0, sl] * szb, wob_ref[...])
        return (sga.astype(F32) * ya + sgb.astype(F32) * yb).astype(BF16)

    def finish(sl, mix):
        y = _dot(mix, wout_ref[...])
        r = x_ref[0, sl] + gate_ref[0, 0] * y
        o_ref[0, sl] = _rms(r, fg_ref[...])

    order = sorted([(3 * i + off, stage, i) for i in range(len(subs))
                    for stage, off in (("gates", 0), ("branches", 4), ("finish", 8))])
    g, mix = {}, {}
    for _, stage, i in order:
        if stage == "gates":
            g[i] = gates(subs[i])
        elif stage == "branches":
            mix[i] = branches(subs[i], g.pop(i))
        else:
            finish(subs[i], mix.pop(i))


OUT_SUB = 256


def _out_call(oa, ob, x, mod, norm_g, w_gate, w_oa, w_ob, w_out, final_g, tm=1024):
    b, s, d = x.shape
    tok3 = lambda i, j: (i, j, 0)
    half = HEADS * MLA_V

    def const(shape):
        return pl.BlockSpec(shape, lambda i, j: (0, 0), pipeline_mode=pl.Buffered(1))

    return pl.pallas_call(
        _out_kernel,
        grid=(b, s // tm),
        in_specs=[
            pl.BlockSpec((1, tm, half), tok3),
            pl.BlockSpec((1, tm, half), tok3),
            pl.BlockSpec((1, tm, d), tok3),
            _mod_spec(MOD_SHIFT),
            _mod_spec(MOD_SCALE),
            _mod_spec(MOD_GATE),
            const((1, d)),
            const((d, w_gate.shape[1])),
            const((half, d)),
            const((half, d)),
            const((d, d)),
            const((1, d)),
        ],
        out_specs=pl.BlockSpec((1, tm, d), tok3),
        out_shape=jax.ShapeDtypeStruct((b, s, d), F32),
        compiler_params=pltpu.CompilerParams(vmem_limit_bytes=VMEM_LIMIT),
        name="gated_out",
    )(oa, ob, x, mod, mod, mod, norm_g, w_gate, w_oa, w_ob, w_out, final_g)


def _rope_partner():
    j = np.arange(MLA_ROPE)
    first = (j % 16) < 8
    src = np.where(first, j + 8, j - 8)
    sign = np.where(first, -1.0, 1.0).astype(np.float32)
    return src, sign


def _rope_tables():
    t = np.arange(SEQ)
    half = MLA_ROPE // 4
    freqs = (ROPE_THETA ** (-np.arange(half, dtype=np.float32) / half)).astype(np.float32)
    ang_r = (t // GRID_W).astype(np.float32)[:, None] * freqs[None, :]
    ang_c = (t % GRID_W).astype(np.float32)[:, None] * freqs[None, :]
    cos = np.concatenate([np.cos(ang_r)] * 2 + [np.cos(ang_c)] * 2, axis=-1).astype(np.float32)
    sin = np.concatenate([np.sin(ang_r)] * 2 + [np.sin(ang_c)] * 2, axis=-1).astype(np.float32)
    return cos, sin


def _np_lane_group(nope, rope):
    pad = np.zeros((rope.shape[0], LANES - MLA_NOPE - MLA_ROPE), np.float32)
    return np.concatenate([nope, rope, pad], axis=-1).astype(np.float32)


def _lane_group(nope, rope, n):
    pad = jnp.zeros((n, LANES - MLA_NOPE - MLA_ROPE), F32)
    return jnp.concatenate([nope, rope, pad], axis=-1)


def kernel(x, c, ctx, c_ctx, w_mod, b_mod, norm_g, w_in, g_cq, w_uq, g_ckv, w_ukv, rpb,
           w_oa, w_ob, w_out, final_g):
    b, s, d = x.shape
    depth = w_mod.shape[0]
    assert depth == 1 and s == SEQ and d == D_MODEL
    mla_scale = (MLA_NOPE + MLA_ROPE) ** -0.5 * LOG2_E
    na_scale = NA_DIM ** -0.5 * LOG2_E
    src, sign = _rope_partner()

    w = w_in[0]
    o = np.cumsum([0, KV_LORA, MLA_ROPE, HEADS * NA_DIM, HEADS * NA_DIM, Q_LORA, HEADS * MLA_V,
                   HEADS * NA_DIM, HEADS * NA_DIM, d, d])
    seg = [w[:, o[n]:o[n + 1]] for n in range(10)]
    w_ckv, w_kr, w_kb, w_vb, w_cq, w_za, w_qb, w_zb, w_ga, w_gb = seg
    z64 = jnp.zeros((d, MLA_NOPE), F32)
    w_ext = jnp.concatenate([
        w_ckv,
        _lane_group(z64, w_kr, d),
        _lane_group(z64, w_kr[:, src] * sign, d),
        w_kb, w_cq], axis=-1).astype(BF16)
    wt_ext = jnp.concatenate([w_vb, w_qb * na_scale], axis=-1).astype(BF16).T
    w_gate = jnp.concatenate([w_za, w_zb, w_ga, w_gb], axis=-1).astype(BF16)

    wkv3 = w_ukv[0].reshape(KV_LORA, HEADS, MLA_NOPE + MLA_V)
    wk = jnp.concatenate([wkv3[:, :, :MLA_NOPE], jnp.zeros((KV_LORA, HEADS, LANES - MLA_NOPE), F32)],
                         axis=-1).reshape(KV_LORA, HEADS * LANES).astype(BF16)
    wv = wkv3[:, :, MLA_NOPE:].reshape(KV_LORA, HEADS * MLA_V).astype(BF16).T
    wq3 = w_uq[0].reshape(Q_LORA, HEADS, MLA_NOPE + MLA_ROPE)
    zq = jnp.zeros((Q_LORA, HEADS, LANES - MLA_NOPE - MLA_ROPE), F32)
    wq = jnp.concatenate([wq3, zq], axis=-1).reshape(Q_LORA, HEADS * LANES).astype(BF16).T
    wqr = jnp.concatenate([jnp.zeros((Q_LORA, HEADS, MLA_NOPE), F32),
                           wq3[:, :, MLA_NOPE:][:, :, src] * sign, zq],
                          axis=-1).reshape(Q_LORA, HEADS * LANES).astype(BF16).T

    cos, sin = _rope_tables()
    ones_s = np.ones((s, MLA_NOPE), np.float32)
    zeros_s = np.zeros((s, MLA_NOPE), np.float32)
    cq_tab = jnp.asarray(np.ascontiguousarray((_np_lane_group(ones_s, cos) * np.float32(mla_scale)).T))
    sq_tab = jnp.asarray(np.ascontiguousarray((_np_lane_group(zeros_s, sin) * np.float32(mla_scale)).T))
    ck_tab = jnp.asarray(_np_lane_group(zeros_s, cos))
    sk_tab = jnp.asarray(_np_lane_group(zeros_s, sin))
    zeros_c = np.zeros((CTX_LEN, MLA_NOPE), np.float32)
    ck_ctx = jnp.asarray(_np_lane_group(zeros_c, np.ones((CTX_LEN, MLA_ROPE), np.float32)))
    sk_ctx = jnp.zeros((CTX_LEN, LANES), F32)

    cs = jnp.concatenate([c, c_ctx[None, :], jnp.zeros((8 - b - 1, d), F32)], axis=0)
    mod = _mod_call(cs, w_mod[0], b_mod[0][None, :])

    ng = norm_g[0][None, :]
    gkv = g_ckv[0][None, :]
    k, v, kb, vb, q, qb = _proj_call(
        x, mod, None, ng, w_ext, wt_ext, gkv, wk, wv, ck_tab, sk_tab,
        q_side=(g_cq[0][None, :], wq, wqr, cq_tab, sq_tab))
    kc, vc, kcb, vcb = _proj_call(
        ctx, mod, b, ng, w_ext[:, :KV_EXT], wt_ext[:_R_VB[1]], gkv, wk, wv,
        ck_ctx, sk_ctx)

    oa = _mla_call(q, k, kc, v, vc)
    ob = _na_call(qb, kb, vb, kcb, vcb, _na_bias_tables(rpb[0]))

    return _out_call(oa, ob, x, mod, ng, w_gate,
                     w_oa[0].astype(BF16), w_ob[0].astype(BF16), w_out[0].astype(BF16),
                     final_g[None, :])
```

```python
import functools

import jax
import jax.numpy as jnp
import numpy as np
from jax import lax
from jax.experimental import pallas as pl
from jax.experimental.pallas import tpu as pltpu

D_MODEL = 1024
SEQ = 4096
GRID_W = 64
GRID_H = SEQ // GRID_W
CTX_LEN = 256
HEADS = 8
MLA_NOPE = 64
MLA_ROPE = 32
MLA_V = 64
Q_LORA = 256
KV_LORA = 128
NA_DIM = 64
NA_KR = 8
NA_KC = 16
ROPE_THETA = 10000.0
EPS = 1e-6

LANES = 128
HEAD_PAIRS = HEADS // 2
VMEM_LIMIT = 56 * 1024 * 1024

_C_CKV = (0, 128)
_C_KR = (128, 256)
_C_KRR = (256, 384)
_C_KB = (384, 896)
KV_EXT = 896
_C_CQ = (896, 1152)
_G_ZA = (0, 512)
_G_ZB = (512, 1024)
_G_GA = (1024, 2048)
_G_GB = (2048, 3072)
_R_VB = (0, 512)
_R_QB = (512, 1024)

NA_ROWS_PER_BLOCK = 4
NA_BLOCKS = GRID_H // NA_ROWS_PER_BLOCK
NA_TQ = NA_ROWS_PER_BLOCK * GRID_W
NA_SLAB_ROWS = 12
NA_SLAB = NA_SLAB_ROWS * GRID_W
NA_MASK = -1e30
LOG2_E = 1.4426950408889634

F32 = jnp.float32
BF16 = jnp.bfloat16


def _dot(a, b):
    return jnp.dot(a, b, preferred_element_type=F32)


def _dot_nt(a, b):
    return lax.dot_general(a, b, (((1,), (1,)), ((), ())), preferred_element_type=F32)


def _rms(x, g):
    return x * lax.rsqrt(jnp.mean(x * x, axis=-1, keepdims=True) + EPS) * g


def _mod_kernel(c_ref, w_ref, b_ref, o_ref):
    c = c_ref[...]
    sc = c * jax.nn.sigmoid(c)
    o_ref[0, :, 0, :] = _dot(sc.astype(BF16), w_ref[...].astype(BF16)) + b_ref[...]


def _mod_call(cs, w_mod, b_mod):
    n = w_mod.shape[1]
    tn = 1024
    return pl.pallas_call(
        _mod_kernel,
        grid=(n // tn,),
        in_specs=[
            pl.BlockSpec((8, D_MODEL), lambda j: (0, 0)),
            pl.BlockSpec((D_MODEL, tn), lambda j: (0, j)),
            pl.BlockSpec((1, tn), lambda j: (0, j)),
        ],
        out_specs=pl.BlockSpec((1, 8, 1, tn), lambda j: (j, 0, 0, 0)),
        out_shape=jax.ShapeDtypeStruct((n // tn, 8, 1, tn), F32),
        name="adaln_mod",
    )(cs, w_mod, b_mod)


def _proj_kernel(with_q, x_ref, shift_ref, scale_ref, ng_ref, w_ref, wt_ref, gkv_ref, wk_ref, wv_ref,
                 ck_ref, sk_ref, *rest):
    if with_q:
        (gq_ref, wq_ref, wqr_ref, cq_ref, sq_ref,
         k_ref, v_ref, kb_ref, vb_ref, q_ref, qb_ref) = rest
    else:
        k_ref, v_ref, kb_ref, vb_ref = rest

    x = x_ref[0]
    h = _rms(x, ng_ref[...]) * (1.0 + scale_ref[0, 0]) + shift_ref[0, 0]
    hb = h.astype(BF16)

    def proj(c):
        return _dot(hb, w_ref[:, c[0]:c[1]])

    def proj_t(r):
        return _dot(hb, wt_ref[:, r[0]:r[1]]).T

    ckvn = _rms(proj(_C_CKV), gkv_ref[...]).astype(BF16)
    if with_q:
        cqn = _rms(proj(_C_CQ), gq_ref[...]).astype(BF16)
    krope = proj(_C_KR) * ck_ref[...] + proj(_C_KRR) * sk_ref[...]
    kb_ref[0] = proj(_C_KB).astype(BF16)
    vb_ref[0] = proj_t(_R_VB).astype(BF16)
    if with_q:
        qb_ref[0] = proj_t(_R_QB).astype(BF16)

    knope = _dot(ckvn, wk_ref[...])
    for hd in range(HEADS):
        sl = slice(hd * LANES, (hd + 1) * LANES)
        k_ref[0, :, sl] = (knope[:, sl] + krope).astype(BF16)
    v_ref[0] = _dot(ckvn, wv_ref[...]).T.astype(BF16)
    if with_q:
        q1 = _dot(cqn, wq_ref[...])
        q2 = _dot(cqn, wqr_ref[...])
        cq = cq_ref[...]
        sq = sq_ref[...]
        for hd in range(HEADS):
            sl = slice(hd * LANES, (hd + 1) * LANES)
            q_ref[0, sl, :] = (q1[:, sl] * cq + q2[:, sl] * sq).T.astype(BF16)


def _mod_spec(which, row=None):
    d = D_MODEL
    if row is None:
        return pl.BlockSpec((1, 1, 1, d), lambda i, j: (which, i, 0, 0))
    return pl.BlockSpec((1, 1, 1, d), lambda i, j: (which, row, 0, 0))


MOD_SHIFT, MOD_SCALE, MOD_GATE = 0, 1, 2


def _proj_call(x, mod, mod_row, norm_g, w_ext, wt_ext, g_ckv, wk, wv, ck, sk,
               q_side=None, tm=256):
    b, s, d = x.shape
    tm = min(tm, s)
    with_q = q_side is not None
    n_ext = w_ext.shape[1]
    n_t = wt_ext.shape[1]
    const2 = lambda i, j: (0, 0)
    tok3 = lambda i, j: (i, j, 0)
    tab = lambda i, j: (j, 0)

    in_specs = [
        pl.BlockSpec((1, tm, d), tok3),
        _mod_spec(MOD_SHIFT, mod_row),
        _mod_spec(MOD_SCALE, mod_row),
        pl.BlockSpec((1, d), const2),
        pl.BlockSpec((d, n_ext), const2),
        pl.BlockSpec((d, n_t), const2),
        pl.BlockSpec((1, KV_LORA), const2),
        pl.BlockSpec((KV_LORA, HEADS * LANES), const2),
        pl.BlockSpec((KV_LORA, HEADS * MLA_V), const2),
        pl.BlockSpec((tm, LANES), tab),
        pl.BlockSpec((tm, LANES), tab),
    ]
    args = [x, mod, mod, norm_g, w_ext, wt_ext, g_ckv, wk, wv, ck, sk]
    tok_major = lambda w: (pl.BlockSpec((1, tm, w), tok3), jax.ShapeDtypeStruct((b, s, w), BF16))
    feat_major = lambda w: (pl.BlockSpec((1, w, tm), lambda i, j: (i, 0, j)),
                            jax.ShapeDtypeStruct((b, w, s), BF16))
    outs = [tok_major(HEADS * LANES), feat_major(HEADS * MLA_V), tok_major(HEADS * NA_DIM),
            feat_major(HEADS * NA_DIM)]
    if with_q:
        g_cq, wq, wqr, cq, sq = q_side
        in_specs += [
            pl.BlockSpec((1, Q_LORA), const2),
            pl.BlockSpec((Q_LORA, HEADS * LANES), const2),
            pl.BlockSpec((Q_LORA, HEADS * LANES), const2),
            pl.BlockSpec((tm, LANES), tab),
            pl.BlockSpec((tm, LANES), tab),
        ]
        args += [g_cq, wq, wqr, cq, sq]
        outs += [feat_major(HEADS * LANES), feat_major(HEADS * NA_DIM)]
    out_specs = [o[0] for o in outs]
    out_shape = [o[1] for o in outs]
    return pl.pallas_call(
        functools.partial(_proj_kernel, with_q),
        grid=(b, s // tm),
        in_specs=in_specs,
        out_specs=out_specs,
        out_shape=out_shape,
        compiler_params=pltpu.CompilerParams(vmem_limit_bytes=VMEM_LIMIT),
        name="in_proj_q" if with_q else "in_proj_ctx",
    )(*args)


BF16_SUBLANES = 16
MLA_TQ = 512
MLA_CHUNK = 512
MLA_QK_CHUNKS = 1
MLA_LOOKAHEAD = 2


def _mla_kernel(qt_ref, k_ref, kc_ref, vt_ref, vct_ref, o_ref):
    s_len = k_ref.shape[1]
    qts = [qt_ref[0, hh * LANES:(hh + 1) * LANES, :] for hh in range(2)]
    qk_keys = MLA_QK_CHUNKS * MLA_CHUNK
    groups = [(hh, k_ref, vt_ref, c * qk_keys, qk_keys) for c in range(s_len // qk_keys) for hh in range(2)]
    groups += [(hh, kc_ref, vct_ref, 0, kc_ref.shape[1]) for hh in range(2)]
    items = [(hh, vr, off + i, min(MLA_CHUNK, n - i))
             for hh, _, vr, off, n in groups for i in range(0, n, MLA_CHUNK)]

    def group_scores(g):
        hh, kr, _, off, n = g
        big = _dot(kr[0, off:off + n, hh * LANES:(hh + 1) * LANES], qts[hh]).astype(BF16)
        return [big[i:i + MLA_CHUNK] for i in range(0, n, MLA_CHUNK)]

    m = [None, None]
    acc = [None, None]
    pending = []
    gi = 0
    for hh, vr, off, n in items:
        while len(pending) <= MLA_LOOKAHEAD and gi < len(groups):
            pending.extend(group_scores(groups[gi]))
            gi += 1
        s = pending.pop(0)
        mc = jnp.max(s, axis=0, keepdims=True)
        m_new = mc if m[hh] is None else jnp.maximum(m[hh], mc)
        p = jnp.exp2(s - m_new)
        v_ones = jnp.concatenate([vr[0, hh * MLA_V:(hh + 1) * MLA_V, off:off + n],
                                  jnp.ones((BF16_SUBLANES, n), BF16)], axis=0)
        pv = _dot(v_ones, p)
        if acc[hh] is None:
            acc[hh] = pv
        else:
            acc[hh] = acc[hh] * jnp.exp2(m[hh].astype(F32) - m_new.astype(F32)) + pv
        m[hh] = m_new
    outs = [a[:MLA_V] / a[MLA_V:MLA_V + 1] for a in acc]
    o_ref[0] = jnp.concatenate(outs, axis=0).T.astype(BF16)


def _mla_call(qt, k, kc, vt, vct):
    b, _, s = qt.shape
    t = kc.shape[1]
    return pl.pallas_call(
        _mla_kernel,
        grid=(b, HEAD_PAIRS, s // MLA_TQ),
        in_specs=[
            pl.BlockSpec((1, 2 * LANES, MLA_TQ), lambda i, p, j: (i, p, j)),
            pl.BlockSpec((1, s, 2 * LANES), lambda i, p, j: (i, 0, p)),
            pl.BlockSpec((1, t, 2 * LANES), lambda i, p, j: (i, 0, p)),
            pl.BlockSpec((1, 2 * MLA_V, s), lambda i, p, j: (i, p, 0)),
            pl.BlockSpec((1, 2 * MLA_V, t), lambda i, p, j: (i, p, 0)),
        ],
        out_specs=pl.BlockSpec((1, MLA_TQ, LANES), lambda i, p, j: (i, j, p)),
        out_shape=jax.ShapeDtypeStruct((b, s, HEADS * MLA_V), BF16),
        compiler_params=pltpu.CompilerParams(vmem_limit_bytes=VMEM_LIMIT),
        name="mla_attention",
    )(qt, k, kc, vt, vct)


NA_HEADS_PER_STEP = 8
NA_LOOKAHEAD = 3


def _na_slab_start(blk):
    return jnp.clip(blk * NA_ROWS_PER_BLOCK - NA_KR // 2, 0, GRID_H - NA_SLAB_ROWS)


def _na_variant(blk):
    return jnp.where(blk == 0, 0, jnp.where(blk == NA_BLOCKS - 1, 2, 1))


def _with_ones(v):
    return jnp.concatenate([v, jnp.ones((BF16_SUBLANES, v.shape[1]), v.dtype)], axis=0)


def _na_kernel(qt_ref, k_ref, vt_ref, kc_ref, vct_ref, bias_ref, o_ref):
    blk = pl.program_id(2)
    start = pl.multiple_of(_na_slab_start(blk) * GRID_W, NA_TQ)
    row_half = lax.broadcasted_iota(jnp.int32, (LANES, NA_TQ), 0) // NA_DIM

    k_all = [jnp.concatenate([k_ref[0, pl.ds(start, NA_SLAB), pair * LANES:(pair + 1) * LANES],
                              kc_ref[0, :, pair * LANES:(pair + 1) * LANES]], axis=0)
             for pair in range(NA_HEADS_PER_STEP // 2)]

    def scores(h):
        pair, hh = divmod(h, 2)
        qt = qt_ref[0, pair * LANES:(pair + 1) * LANES, :]
        qm = jnp.where(row_half == hh, qt, jnp.zeros_like(qt))
        s = _dot(k_all[pair], qm).astype(BF16)
        return jnp.concatenate([s[:NA_SLAB] + bias_ref[0, h], s[NA_SLAB:]], axis=0)

    outs = []
    ahead = [scores(h) for h in range(NA_LOOKAHEAD)]
    for h in range(NA_HEADS_PER_STEP):
        if h + NA_LOOKAHEAD < NA_HEADS_PER_STEP:
            ahead.append(scores(h + NA_LOOKAHEAD))
        s = ahead.pop(0)
        p = jnp.exp2(s - jnp.max(s, axis=0, keepdims=True))
        rows = slice(h * NA_DIM, (h + 1) * NA_DIM)
        v_all = jnp.concatenate([vt_ref[0, rows, pl.ds(start, NA_SLAB)], vct_ref[0, rows, :]], axis=1)
        pv = _dot(_with_ones(v_all), p)
        outs.append(pv[:NA_DIM] / pv[NA_DIM:NA_DIM + 1])
    o_ref[0] = jnp.concatenate(outs, axis=0).T.astype(BF16)


def _na_call(qbt, kb, vbt, kcb, vcbt, bias):
    b, s, _ = kb.shape
    t = kcb.shape[1]
    wide = NA_HEADS_PER_STEP * NA_DIM
    return pl.pallas_call(
        _na_kernel,
        grid=(HEADS // NA_HEADS_PER_STEP, b, NA_BLOCKS),
        in_specs=[
            pl.BlockSpec((1, wide, NA_TQ), lambda g, i, j: (i, g, j)),
            pl.BlockSpec((1, s, wide), lambda g, i, j: (i, 0, g)),
            pl.BlockSpec((1, wide, s), lambda g, i, j: (i, g, 0)),
            pl.BlockSpec((1, t, wide), lambda g, i, j: (i, 0, g)),
            pl.BlockSpec((1, wide, t), lambda g, i, j: (i, g, 0)),
            pl.BlockSpec((1, NA_HEADS_PER_STEP, NA_SLAB, NA_TQ),
                         lambda g, i, j: (_na_variant(j), g, 0, 0)),
        ],
        out_specs=pl.BlockSpec((1, NA_TQ, wide), lambda g, i, j: (i, j, g)),
        out_shape=jax.ShapeDtypeStruct((b, s, HEADS * NA_DIM), BF16),
        compiler_params=pltpu.CompilerParams(vmem_limit_bytes=VMEM_LIMIT),
        name="na_attention",
    )(qbt, kb, vbt, kcb, vcbt, bias)


def _na_bias_kernel(rpb_ref, o_ref):
    rp = pltpu.roll(rpb_ref[0] * LOG2_E, LANES - (NA_KC - 1), axis=1)
    shape = (GRID_W, LANES)
    wk = lax.broadcasted_iota(jnp.int32, shape, 0)
    lane = lax.broadcasted_iota(jnp.int32, shape, 1)
    c_start = jnp.clip(lane - NA_KC // 2, 0, GRID_W - NA_KC)
    left = lane < GRID_W
    mask_l = jnp.where(left, NA_MASK, 0.0).astype(F32)
    mask_r = jnp.where(left, 0.0, NA_MASK).astype(F32)
    tiles_l, tiles_r = [], []
    for d in range(2 * NA_KR - 1):
        row = jnp.broadcast_to(rp[d:d + 1, :], shape)
        t = pltpu.roll(row, 0, axis=1, stride=1, stride_axis=0)
        t = jnp.where(wk >= c_start, jnp.where(wk < c_start + NA_KC, t, NA_MASK), NA_MASK)
        t = jnp.where(left, t, 0.0)
        tiles_l.append(t)
        tiles_r.append(pltpu.roll(t, GRID_W, axis=1))
    for n, blk in enumerate((0, 1, NA_BLOCKS - 1)):
        sb = min(max(blk * NA_ROWS_PER_BLOCK - NA_KR // 2, 0), GRID_H - NA_SLAB_ROWS)
        for j in range(NA_SLAB_ROWS):
            kr = sb + j

            def pick(i, tiles, masked):
                r = blk * NA_ROWS_PER_BLOCK + i
                r_start = min(max(r - NA_KR // 2, 0), GRID_H - NA_KR)
                if r_start <= kr < r_start + NA_KR:
                    return tiles[kr - r + NA_KR - 1]
                return masked

            for ii in range(NA_ROWS_PER_BLOCK // 2):
                o_ref[n, 0, j * GRID_W:(j + 1) * GRID_W, ii * LANES:(ii + 1) * LANES] = (
                    pick(2 * ii, tiles_l, mask_l) + pick(2 * ii + 1, tiles_r, mask_r)).astype(BF16)


def _na_bias_tables(rpb):
    rpb_p = jnp.pad(rpb[:, :, ::-1], ((0, 0), (0, 1), (0, LANES - (2 * NA_KC - 1))))
    return pl.pallas_call(
        _na_bias_kernel,
        grid=(HEADS,),
        in_specs=[pl.BlockSpec((1, 2 * NA_KR, LANES), lambda h: (h, 0, 0))],
        out_specs=pl.BlockSpec((3, 1, NA_SLAB, NA_TQ), lambda h: (0, h, 0, 0)),
        out_shape=jax.ShapeDtypeStruct((3, HEADS, NA_SLAB, NA_TQ), BF16),
        name="na_bias_table",
    )(rpb_p)


def _out_kernel(oa_ref, ob_ref, x_ref, shift_ref, scale_ref, gate_ref, ng_ref, wg_ref,
                woa_ref, wob_ref, wout_ref, fg_ref, o_ref):
    subs = [slice(i, i + OUT_SUB) for i in range(0, o_ref.shape[1], OUT_SUB)]

    def gates(sl):
        h = _rms(x_ref[0, sl], ng_ref[...]) * (1.0 + scale_ref[0, 0]) + shift_ref[0, 0]
        hb = h.astype(BF16)
        proj = lambda c: _dot(hb, wg_ref[:, c[0]:c[1]])
        za = proj(_G_ZA)
        zb = proj(_G_ZB)
        return ((za * jax.nn.sigmoid(za)).astype(BF16), (zb * jax.nn.sigmoid(zb)).astype(BF16),
                jax.nn.sigmoid(proj(_G_GA)).astype(BF16), jax.nn.sigmoid(proj(_G_GB)).astype(BF16))

    def branches(sl, g):
        sza, szb, sga, sgb = g
        ya = _dot(oa_ref[0, sl] * sza, woa_ref[...])
        yb = _dot(ob_ref[0, sl] * szb, wob_ref[...])
        return (sga.astype(F32) * ya + sgb.astype(F32) * yb).astype(BF16)

    def finish(sl, mix):
        y = _dot(mix, wout_ref[...])
        r = x_ref[0, sl] + gate_ref[0, 0] * y
        o_ref[0, sl] = _rms(r, fg_ref[...])

    order = sorted([(3 * i + off, stage, i) for i in range(len(subs))
                    for stage, off in (("gates", 0), ("branches", 4), ("finish", 8))])
    g, mix = {}, {}
    for _, stage, i in order:
        if stage == "gates":
            g[i] = gates(subs[i])
        elif stage == "branches":
            mix[i] = branches(subs[i], g.pop(i))
        else:
            finish(subs[i], mix.pop(i))


OUT_SUB = 256


def _out_call(oa, ob, x, mod, norm_g, w_gate, w_oa, w_ob, w_out, final_g, tm=1024):
    b, s, d = x.shape
    tok3 = lambda i, j: (i, j, 0)
    half = HEADS * MLA_V

    def const(shape):
        return pl.BlockSpec(shape, lambda i, j: (0, 0), pipeline_mode=pl.Buffered(1))

    return pl.pallas_call(
        _out_kernel,
        grid=(b, s // tm),
        in_specs=[
            pl.BlockSpec((1, tm, half), tok3),
            pl.BlockSpec((1, tm, half), tok3),
            pl.BlockSpec((1, tm, d), tok3),
            _mod_spec(MOD_SHIFT),
            _mod_spec(MOD_SCALE),
            _mod_spec(MOD_GATE),
            const((1, d)),
            const((d, w_gate.shape[1])),
            const((half, d)),
            const((half, d)),
            const((d, d)),
            const((1, d)),
        ],
        out_specs=pl.BlockSpec((1, tm, d), tok3),
        out_shape=jax.ShapeDtypeStruct((b, s, d), F32),
        compiler_params=pltpu.CompilerParams(vmem_limit_bytes=VMEM_LIMIT),
        name="gated_out",
    )(oa, ob, x, mod, mod, mod, norm_g, w_gate, w_oa, w_ob, w_out, final_g)


def _rope_partner():
    j = np.arange(MLA_ROPE)
    first = (j % 16) < 8
    src = np.where(first, j + 8, j - 8)
    sign = np.where(first, -1.0, 1.0).astype(np.float32)
    return src, sign


def _rope_tables():
    t = np.arange(SEQ)
    half = MLA_ROPE // 4
    freqs = (ROPE_THETA ** (-np.arange(half, dtype=np.float32) / half)).astype(np.float32)
    ang_r = (t // GRID_W).astype(np.float32)[:, None] * freqs[None, :]
    ang_c = (t % GRID_W).astype(np.float32)[:, None] * freqs[None, :]
    cos = np.concatenate([np.cos(ang_r)] * 2 + [np.cos(ang_c)] * 2, axis=-1).astype(np.float32)
    sin = np.concatenate([np.sin(ang_r)] * 2 + [np.sin(ang_c)] * 2, axis=-1).astype(np.float32)
    return cos, sin


def _np_lane_group(nope, rope):
    pad = np.zeros((rope.shape[0], LANES - MLA_NOPE - MLA_ROPE), np.float32)
    return np.concatenate([nope, rope, pad], axis=-1).astype(np.float32)


def _lane_group(nope, rope, n):
    pad = jnp.zeros((n, LANES - MLA_NOPE - MLA_ROPE), F32)
    return jnp.concatenate([nope, rope, pad], axis=-1)


def kernel(x, c, ctx, c_ctx, w_mod, b_mod, norm_g, w_in, g_cq, w_uq, g_ckv, w_ukv, rpb,
           w_oa, w_ob, w_out, final_g):
    b, s, d = x.shape
    depth = w_mod.shape[0]
    assert depth == 1 and s == SEQ and d == D_MODEL
    mla_scale = (MLA_NOPE + MLA_ROPE) ** -0.5 * LOG2_E
    na_scale = NA_DIM ** -0.5 * LOG2_E
    src, sign = _rope_partner()

    w = w_in[0]
    o = np.cumsum([0, KV_LORA, MLA_ROPE, HEADS * NA_DIM, HEADS * NA_DIM, Q_LORA, HEADS * MLA_V,
                   HEADS * NA_DIM, HEADS * NA_DIM, d, d])
    seg = [w[:, o[n]:o[n + 1]] for n in range(10)]
    w_ckv, w_kr, w_kb, w_vb, w_cq, w_za, w_qb, w_zb, w_ga, w_gb = seg
    z64 = jnp.zeros((d, MLA_NOPE), F32)
    w_ext = jnp.concatenate([
        w_ckv,
        _lane_group(z64, w_kr, d),
        _lane_group(z64, w_kr[:, src] * sign, d),
        w_kb, w_cq], axis=-1).astype(BF16)
    wt_ext = jnp.concatenate([w_vb, w_qb * na_scale], axis=-1).astype(BF16)
    w_gate = jnp.concatenate([w_za, w_zb, w_ga, w_gb], axis=-1).astype(BF16)

    wkv3 = w_ukv[0].reshape(KV_LORA, HEADS, MLA_NOPE + MLA_V)
    wk = jnp.concatenate([wkv3[:, :, :MLA_NOPE], jnp.zeros((KV_LORA, HEADS, LANES - MLA_NOPE), F32)],
                         axis=-1).reshape(KV_LORA, HEADS * LANES).astype(BF16)
    wv = wkv3[:, :, MLA_NOPE:].reshape(KV_LORA, HEADS * MLA_V).astype(BF16)
    wq3 = w_uq[0].reshape(Q_LORA, HEADS, MLA_NOPE + MLA_ROPE)
    zq = jnp.zeros((Q_LORA, HEADS, LANES - MLA_NOPE - MLA_ROPE), F32)
    wq = jnp.concatenate([wq3, zq], axis=-1).reshape(Q_LORA, HEADS * LANES).astype(BF16)
    wqr = jnp.concatenate([jnp.zeros((Q_LORA, HEADS, MLA_NOPE), F32),
                           wq3[:, :, MLA_NOPE:][:, :, src] * sign, zq],
                          axis=-1).reshape(Q_LORA, HEADS * LANES).astype(BF16)

    cos, sin = _rope_tables()
    ones_s = np.ones((s, MLA_NOPE), np.float32)
    zeros_s = np.zeros((s, MLA_NOPE), np.float32)
    cq_tab = jnp.asarray(_np_lane_group(ones_s, cos) * np.float32(mla_scale))
    sq_tab = jnp.asarray(_np_lane_group(zeros_s, sin) * np.float32(mla_scale))
    ck_tab = jnp.asarray(_np_lane_group(zeros_s, cos))
    sk_tab = jnp.asarray(_np_lane_group(zeros_s, sin))
    zeros_c = np.zeros((CTX_LEN, MLA_NOPE), np.float32)
    ck_ctx = jnp.asarray(_np_lane_group(zeros_c, np.ones((CTX_LEN, MLA_ROPE), np.float32)))
    sk_ctx = jnp.zeros((CTX_LEN, LANES), F32)

    cs = jnp.concatenate([c, c_ctx[None, :], jnp.zeros((8 - b - 1, d), F32)], axis=0)
    mod = _mod_call(cs, w_mod[0], b_mod[0][None, :])

    ng = norm_g[0][None, :]
    gkv = g_ckv[0][None, :]
    k, v, kb, vb, q, qb = _proj_call(
        x, mod, None, ng, w_ext, wt_ext, gkv, wk, wv, ck_tab, sk_tab,
        q_side=(g_cq[0][None, :], wq, wqr, cq_tab, sq_tab))
    kc, vc, kcb, vcb = _proj_call(
        ctx, mod, b, ng, w_ext[:, :KV_EXT], wt_ext[:, :_R_VB[1]], gkv, wk, wv,
        ck_ctx, sk_ctx)

    oa = _mla_call(q, k, kc, v, vc)
    ob = _na_call(qb, kb, vb, kcb, vcb, _na_bias_tables(rpb[0]))

    return _out_call(oa, ob, x, mod, ng, w_gate,
                     w_oa[0].astype(BF16), w_ob[0].astype(BF16), w_out[0].astype(BF16),
                     final_g[None, :])
```

```python
import functools

import jax
import jax.numpy as jnp
import numpy as np
from jax import lax
from jax.experimental import pallas as pl
from jax.experimental.pallas import tpu as pltpu

D_MODEL = 1024
SEQ = 4096
GRID_W = 64
GRID_H = SEQ // GRID_W
CTX_LEN = 256
HEADS = 8
MLA_NOPE = 64
MLA_ROPE = 32
MLA_V = 64
Q_LORA = 256
KV_LORA = 128
NA_DIM = 64
NA_KR = 8
NA_KC = 16
ROPE_THETA = 10000.0
EPS = 1e-6

LANES = 128
HEAD_PAIRS = HEADS // 2
VMEM_LIMIT = 56 * 1024 * 1024

_C_CKV = (0, 128)
_C_KR = (128, 256)
_C_KRR = (256, 384)
_C_KB = (384, 896)
KV_EXT = 896
_C_CQ = (896, 1152)
_G_ZA = (0, 512)
_G_ZB = (512, 1024)
_G_GA = (1024, 2048)
_G_GB = (2048, 3072)
_R_VB = (0, 512)
_R_QB = (512, 1024)

NA_ROWS_PER_BLOCK = 4
NA_BLOCKS = GRID_H // NA_ROWS_PER_BLOCK
NA_TQ = NA_ROWS_PER_BLOCK * GRID_W
NA_SLAB_ROWS = 12
NA_SLAB = NA_SLAB_ROWS * GRID_W
NA_MASK = -1e30
LOG2_E = 1.4426950408889634

F32 = jnp.float32
BF16 = jnp.bfloat16


def _dot(a, b):
    return jnp.dot(a, b, preferred_element_type=F32)


def _dot_nt(a, b):
    return lax.dot_general(a, b, (((1,), (1,)), ((), ())), preferred_element_type=F32)


def _rms(x, g):
    return x * lax.rsqrt(jnp.mean(x * x, axis=-1, keepdims=True) + EPS) * g


def _mod_kernel(c_ref, w_ref, b_ref, o_ref):
    c = c_ref[...]
    sc = c * jax.nn.sigmoid(c)
    o_ref[0, :, 0, :] = _dot(sc.astype(BF16), w_ref[...].astype(BF16)) + b_ref[...]


def _mod_call(cs, w_mod, b_mod):
    n = w_mod.shape[1]
    tn = 1024
    return pl.pallas_call(
        _mod_kernel,
        grid=(n // tn,),
        in_specs=[
            pl.BlockSpec((8, D_MODEL), lambda j: (0, 0)),
            pl.BlockSpec((D_MODEL, tn), lambda j: (0, j)),
            pl.BlockSpec((1, tn), lambda j: (0, j)),
        ],
        out_specs=pl.BlockSpec((1, 8, 1, tn), lambda j: (j, 0, 0, 0)),
        out_shape=jax.ShapeDtypeStruct((n // tn, 8, 1, tn), F32),
        name="adaln_mod",
    )(cs, w_mod, b_mod)


def _proj_kernel(with_q, x_ref, shift_ref, scale_ref, ng_ref, w_ref, wt_ref, gkv_ref, wk_ref, wv_ref,
                 ck_ref, sk_ref, *rest):
    if with_q:
        (gq_ref, wq_ref, wqr_ref, cq_ref, sq_ref,
         k_ref, v_ref, kb_ref, vb_ref, q_ref, qb_ref) = rest
    else:
        k_ref, v_ref, kb_ref, vb_ref = rest

    x = x_ref[0]
    h = _rms(x, ng_ref[...]) * (1.0 + scale_ref[0, 0]) + shift_ref[0, 0]
    hb = h.astype(BF16)

    def proj(c):
        return _dot_nt(hb, w_ref[c[0]:c[1], :])

    def proj_t(r):
        return _dot_nt(wt_ref[r[0]:r[1], :], hb)

    ckvn = _rms(proj(_C_CKV), gkv_ref[...]).astype(BF16)
    if with_q:
        cqn = _rms(proj(_C_CQ), gq_ref[...]).astype(BF16)
    krope = proj(_C_KR) * ck_ref[...] + proj(_C_KRR) * sk_ref[...]
    kb_ref[0] = proj(_C_KB).astype(BF16)
    vb_ref[0] = proj_t(_R_VB).astype(BF16)
    if with_q:
        qb_ref[0] = proj_t(_R_QB).astype(BF16)

    knope = _dot(ckvn, wk_ref[...])
    for hd in range(HEADS):
        sl = slice(hd * LANES, (hd + 1) * LANES)
        k_ref[0, :, sl] = (knope[:, sl] + krope).astype(BF16)
    v_ref[0] = _dot_nt(wv_ref[...], ckvn).astype(BF16)
    if with_q:
        q1 = _dot_nt(wq_ref[...], cqn)
        q2 = _dot_nt(wqr_ref[...], cqn)
        cq = cq_ref[...]
        sq = sq_ref[...]
        for hd in range(HEADS):
            sl = slice(hd * LANES, (hd + 1) * LANES)
            q_ref[0, sl, :] = (q1[sl] * cq + q2[sl] * sq).astype(BF16)


def _mod_spec(which, row=None):
    d = D_MODEL
    if row is None:
        return pl.BlockSpec((1, 1, 1, d), lambda i, j: (which, i, 0, 0))
    return pl.BlockSpec((1, 1, 1, d), lambda i, j: (which, row, 0, 0))


MOD_SHIFT, MOD_SCALE, MOD_GATE = 0, 1, 2


def _proj_call(x, mod, mod_row, norm_g, w_ext, wt_ext, g_ckv, wk, wv, ck, sk,
               q_side=None, tm=256):
    b, s, d = x.shape
    tm = min(tm, s)
    with_q = q_side is not None
    n_ext = w_ext.shape[0]
    n_t = wt_ext.shape[0]
    const2 = lambda i, j: (0, 0)
    tok3 = lambda i, j: (i, j, 0)
    tab = lambda i, j: (j, 0)

    in_specs = [
        pl.BlockSpec((1, tm, d), tok3),
        _mod_spec(MOD_SHIFT, mod_row),
        _mod_spec(MOD_SCALE, mod_row),
        pl.BlockSpec((1, d), const2),
        pl.BlockSpec((n_ext, d), const2),
        pl.BlockSpec((n_t, d), const2),
        pl.BlockSpec((1, KV_LORA), const2),
        pl.BlockSpec((KV_LORA, HEADS * LANES), const2),
        pl.BlockSpec((HEADS * MLA_V, KV_LORA), const2),
        pl.BlockSpec((tm, LANES), tab),
        pl.BlockSpec((tm, LANES), tab),
    ]
    args = [x, mod, mod, norm_g, w_ext, wt_ext, g_ckv, wk, wv, ck, sk]
    tok_major = lambda w: (pl.BlockSpec((1, tm, w), tok3), jax.ShapeDtypeStruct((b, s, w), BF16))
    feat_major = lambda w: (pl.BlockSpec((1, w, tm), lambda i, j: (i, 0, j)),
                            jax.ShapeDtypeStruct((b, w, s), BF16))
    outs = [tok_major(HEADS * LANES), feat_major(HEADS * MLA_V), tok_major(HEADS * NA_DIM),
            feat_major(HEADS * NA_DIM)]
    if with_q:
        g_cq, wq, wqr, cq, sq = q_side
        in_specs += [
            pl.BlockSpec((1, Q_LORA), const2),
            pl.BlockSpec((HEADS * LANES, Q_LORA), const2),
            pl.BlockSpec((HEADS * LANES, Q_LORA), const2),
            pl.BlockSpec((LANES, tm), lambda i, j: (0, j)),
            pl.BlockSpec((LANES, tm), lambda i, j: (0, j)),
        ]
        args += [g_cq, wq, wqr, cq, sq]
        outs += [feat_major(HEADS * LANES), feat_major(HEADS * NA_DIM)]
    out_specs = [o[0] for o in outs]
    out_shape = [o[1] for o in outs]
    return pl.pallas_call(
        functools.partial(_proj_kernel, with_q),
        grid=(b, s // tm),
        in_specs=in_specs,
        out_specs=out_specs,
        out_shape=out_shape,
        compiler_params=pltpu.CompilerParams(vmem_limit_bytes=VMEM_LIMIT),
        name="in_proj_q" if with_q else "in_proj_ctx",
    )(*args)


BF16_SUBLANES = 16
MLA_TQ = 512
MLA_CHUNK = 512
MLA_QK_CHUNKS = 1
MLA_LOOKAHEAD = 2


def _mla_kernel(qt_ref, k_ref, kc_ref, vt_ref, vct_ref, o_ref):
    s_len = k_ref.shape[1]
    qts = [qt_ref[0, hh * LANES:(hh + 1) * LANES, :] for hh in range(2)]
    qk_keys = MLA_QK_CHUNKS * MLA_CHUNK
    groups = [(hh, k_ref, vt_ref, c * qk_keys, qk_keys) for c in range(s_len // qk_keys) for hh in range(2)]
    groups += [(hh, kc_ref, vct_ref, 0, kc_ref.shape[1]) for hh in range(2)]
    items = [(hh, vr, off + i, min(MLA_CHUNK, n - i))
             for hh, _, vr, off, n in groups for i in range(0, n, MLA_CHUNK)]

    def group_scores(g):
        hh, kr, _, off, n = g
        big = _dot(kr[0, off:off + n, hh * LANES:(hh + 1) * LANES], qts[hh]).astype(BF16)
        return [big[i:i + MLA_CHUNK] for i in range(0, n, MLA_CHUNK)]

    m = [None, None]
    acc = [None, None]
    pending = []
    gi = 0
    for hh, vr, off, n in items:
        while len(pending) <= MLA_LOOKAHEAD and gi < len(groups):
            pending.extend(group_scores(groups[gi]))
            gi += 1
        s = pending.pop(0)
        mc = jnp.max(s, axis=0, keepdims=True)
        m_new = mc if m[hh] is None else jnp.maximum(m[hh], mc)
        p = jnp.exp2(s - m_new)
        v_ones = jnp.concatenate([vr[0, hh * MLA_V:(hh + 1) * MLA_V, off:off + n],
                                  jnp.ones((BF16_SUBLANES, n), BF16)], axis=0)
        pv = _dot(v_ones, p)
        if acc[hh] is None:
            acc[hh] = pv
        else:
            acc[hh] = acc[hh] * jnp.exp2(m[hh].astype(F32) - m_new.astype(F32)) + pv
        m[hh] = m_new
    outs = [a[:MLA_V] / a[MLA_V:MLA_V + 1] for a in acc]
    o_ref[0] = jnp.concatenate(outs, axis=0).T.astype(BF16)


def _mla_call(qt, k, kc, vt, vct):
    b, _, s = qt.shape
    t = kc.shape[1]
    return pl.pallas_call(
        _mla_kernel,
        grid=(b, HEAD_PAIRS, s // MLA_TQ),
        in_specs=[
            pl.BlockSpec((1, 2 * LANES, MLA_TQ), lambda i, p, j: (i, p, j)),
            pl.BlockSpec((1, s, 2 * LANES), lambda i, p, j: (i, 0, p)),
            pl.BlockSpec((1, t, 2 * LANES), lambda i, p, j: (i, 0, p)),
            pl.BlockSpec((1, 2 * MLA_V, s), lambda i, p, j: (i, p, 0)),
            pl.BlockSpec((1, 2 * MLA_V, t), lambda i, p, j: (i, p, 0)),
        ],
        out_specs=pl.BlockSpec((1, MLA_TQ, LANES), lambda i, p, j: (i, j, p)),
        out_shape=jax.ShapeDtypeStruct((b, s, HEADS * MLA_V), BF16),
        compiler_params=pltpu.CompilerParams(vmem_limit_bytes=VMEM_LIMIT),
        name="mla_attention",
    )(qt, k, kc, vt, vct)


NA_HEADS_PER_STEP = 8
NA_LOOKAHEAD = 3


def _na_slab_start(blk):
    return jnp.clip(blk * NA_ROWS_PER_BLOCK - NA_KR // 2, 0, GRID_H - NA_SLAB_ROWS)


def _na_variant(blk):
    return jnp.where(blk == 0, 0, jnp.where(blk == NA_BLOCKS - 1, 2, 1))


def _with_ones(v):
    return jnp.concatenate([v, jnp.ones((BF16_SUBLANES, v.shape[1]), v.dtype)], axis=0)


def _na_kernel(qt_ref, k_ref, vt_ref, kc_ref, vct_ref, bias_ref, o_ref):
    blk = pl.program_id(2)
    start = pl.multiple_of(_na_slab_start(blk) * GRID_W, NA_TQ)
    row_half = lax.broadcasted_iota(jnp.int32, (LANES, NA_TQ), 0) // NA_DIM

    k_all = [jnp.concatenate([k_ref[0, pl.ds(start, NA_SLAB), pair * LANES:(pair + 1) * LANES],
                              kc_ref[0, :, pair * LANES:(pair + 1) * LANES]], axis=0)
             for pair in range(NA_HEADS_PER_STEP // 2)]

    def scores(h):
        pair, hh = divmod(h, 2)
        qt = qt_ref[0, pair * LANES:(pair + 1) * LANES, :]
        qm = jnp.where(row_half == hh, qt, jnp.zeros_like(qt))
        s = _dot(k_all[pair], qm).astype(BF16)
        return jnp.concatenate([s[:NA_SLAB] + bias_ref[0, h], s[NA_SLAB:]], axis=0)

    outs = []
    ahead = [scores(h) for h in range(NA_LOOKAHEAD)]
    for h in range(NA_HEADS_PER_STEP):
        if h + NA_LOOKAHEAD < NA_HEADS_PER_STEP:
            ahead.append(scores(h + NA_LOOKAHEAD))
        s = ahead.pop(0)
        p = jnp.exp2(s - jnp.max(s, axis=0, keepdims=True))
        rows = slice(h * NA_DIM, (h + 1) * NA_DIM)
        v_all = jnp.concatenate([vt_ref[0, rows, pl.ds(start, NA_SLAB)], vct_ref[0, rows, :]], axis=1)
        pv = _dot(_with_ones(v_all), p)
        outs.append(pv[:NA_DIM] / pv[NA_DIM:NA_DIM + 1])
    o_ref[0] = jnp.concatenate(outs, axis=0).T.astype(BF16)


def _na_call(qbt, kb, vbt, kcb, vcbt, bias):
    b, s, _ = kb.shape
    t = kcb.shape[1]
    wide = NA_HEADS_PER_STEP * NA_DIM
    return pl.pallas_call(
        _na_kernel,
        grid=(HEADS // NA_HEADS_PER_STEP, b, NA_BLOCKS),
        in_specs=[
            pl.BlockSpec((1, wide, NA_TQ), lambda g, i, j: (i, g, j)),
            pl.BlockSpec((1, s, wide), lambda g, i, j: (i, 0, g)),
            pl.BlockSpec((1, wide, s), lambda g, i, j: (i, g, 0)),
            pl.BlockSpec((1, t, wide), lambda g, i, j: (i, 0, g)),
            pl.BlockSpec((1, wide, t), lambda g, i, j: (i, g, 0)),
            pl.BlockSpec((1, NA_HEADS_PER_STEP, NA_SLAB, NA_TQ),
                         lambda g, i, j: (_na_variant(j), g, 0, 0)),
        ],
        out_specs=pl.BlockSpec((1, NA_TQ, wide), lambda g, i, j: (i, j, g)),
        out_shape=jax.ShapeDtypeStruct((b, s, HEADS * NA_DIM), BF16),
        compiler_params=pltpu.CompilerParams(vmem_limit_bytes=VMEM_LIMIT),
        name="na_attention",
    )(qbt, kb, vbt, kcb, vcbt, bias)


def _na_bias_kernel(rpb_ref, o_ref):
    rp = pltpu.roll(rpb_ref[0] * LOG2_E, LANES - (NA_KC - 1), axis=1)
    shape = (GRID_W, LANES)
    wk = lax.broadcasted_iota(jnp.int32, shape, 0)
    lane = lax.broadcasted_iota(jnp.int32, shape, 1)
    c_start = jnp.clip(lane - NA_KC // 2, 0, GRID_W - NA_KC)
    left = lane < GRID_W
    mask_l = jnp.where(left, NA_MASK, 0.0).astype(F32)
    mask_r = jnp.where(left, 0.0, NA_MASK).astype(F32)
    tiles_l, tiles_r = [], []
    for d in range(2 * NA_KR - 1):
        row = jnp.broadcast_to(rp[d:d + 1, :], shape)
        t = pltpu.roll(row, 0, axis=1, stride=1, stride_axis=0)
        t = jnp.where(wk >= c_start, jnp.where(wk < c_start + NA_KC, t, NA_MASK), NA_MASK)
        t = jnp.where(left, t, 0.0)
        tiles_l.append(t)
        tiles_r.append(pltpu.roll(t, GRID_W, axis=1))
    for n, blk in enumerate((0, 1, NA_BLOCKS - 1)):
        sb = min(max(blk * NA_ROWS_PER_BLOCK - NA_KR // 2, 0), GRID_H - NA_SLAB_ROWS)
        for j in range(NA_SLAB_ROWS):
            kr = sb + j

            def pick(i, tiles, masked):
                r = blk * NA_ROWS_PER_BLOCK + i
                r_start = min(max(r - NA_KR // 2, 0), GRID_H - NA_KR)
                if r_start <= kr < r_start + NA_KR:
                    return tiles[kr - r + NA_KR - 1]
                return masked

            for ii in range(NA_ROWS_PER_BLOCK // 2):
                o_ref[n, 0, j * GRID_W:(j + 1) * GRID_W, ii * LANES:(ii + 1) * LANES] = (
                    pick(2 * ii, tiles_l, mask_l) + pick(2 * ii + 1, tiles_r, mask_r)).astype(BF16)


def _na_bias_tables(rpb):
    rpb_p = jnp.pad(rpb[:, :, ::-1], ((0, 0), (0, 1), (0, LANES - (2 * NA_KC - 1))))
    return pl.pallas_call(
        _na_bias_kernel,
        grid=(HEADS,),
        in_specs=[pl.BlockSpec((1, 2 * NA_KR, LANES), lambda h: (h, 0, 0))],
        out_specs=pl.BlockSpec((3, 1, NA_SLAB, NA_TQ), lambda h: (0, h, 0, 0)),
        out_shape=jax.ShapeDtypeStruct((3, HEADS, NA_SLAB, NA_TQ), BF16),
        name="na_bias_table",
    )(rpb_p)


def _out_kernel(oa_ref, ob_ref, x_ref, shift_ref, scale_ref, gate_ref, ng_ref, wg_ref,
                woa_ref, wob_ref, wout_ref, fg_ref, o_ref):
    subs = [slice(i, i + OUT_SUB) for i in range(0, o_ref.shape[1], OUT_SUB)]

    def gates(sl):
        h = _rms(x_ref[0, sl], ng_ref[...]) * (1.0 + scale_ref[0, 0]) + shift_ref[0, 0]
        hb = h.astype(BF16)
        proj = lambda c: _dot_nt(hb, wg_ref[c[0]:c[1], :])
        za = proj(_G_ZA)
        zb = proj(_G_ZB)
        return ((za * jax.nn.sigmoid(za)).astype(BF16), (zb * jax.nn.sigmoid(zb)).astype(BF16),
                jax.nn.sigmoid(proj(_G_GA)).astype(BF16), jax.nn.sigmoid(proj(_G_GB)).astype(BF16))

    def branches(sl, g):
        sza, szb, sga, sgb = g
        ya = _dot(oa_ref[0, sl] * sza, woa_ref[...])
        yb = _dot(ob_ref[0, sl] * szb, wob_ref[...])
        return (sga.astype(F32) * ya + sgb.astype(F32) * yb).astype(BF16)

    def finish(sl, mix):
        y = _dot(mix, wout_ref[...])
        r = x_ref[0, sl] + gate_ref[0, 0] * y
        o_ref[0, sl] = _rms(r, fg_ref[...])

    order = sorted([(3 * i + off, stage, i) for i in range(len(subs))
                    for stage, off in (("gates", 0), ("branches", 4), ("finish", 8))])
    g, mix = {}, {}
    for _, stage, i in order:
        if stage == "gates":
            g[i] = gates(subs[i])
        elif stage == "branches":
            mix[i] = branches(subs[i], g.pop(i))
        else:
            finish(subs[i], mix.pop(i))


OUT_SUB = 256


def _out_call(oa, ob, x, mod, norm_g, w_gate, w_oa, w_ob, w_out, final_g, tm=1024):
    b, s, d = x.shape
    tok3 = lambda i, j: (i, j, 0)
    half = HEADS * MLA_V

    def const(shape):
        return pl.BlockSpec(shape, lambda i, j: (0, 0), pipeline_mode=pl.Buffered(1))

    return pl.pallas_call(
        _out_kernel,
        grid=(b, s // tm),
        in_specs=[
            pl.BlockSpec((1, tm, half), tok3),
            pl.BlockSpec((1, tm, half), tok3),
            pl.BlockSpec((1, tm, d), tok3),
            _mod_spec(MOD_SHIFT),
            _mod_spec(MOD_SCALE),
            _mod_spec(MOD_GATE),
            const((1, d)),
            const((w_gate.shape[0], d)),
            const((half, d)),
            const((half, d)),
            const((d, d)),
            const((1, d)),
        ],
        out_specs=pl.BlockSpec((1, tm, d), tok3),
        out_shape=jax.ShapeDtypeStruct((b, s, d), F32),
        compiler_params=pltpu.CompilerParams(vmem_limit_bytes=VMEM_LIMIT),
        name="gated_out",
    )(oa, ob, x, mod, mod, mod, norm_g, w_gate, w_oa, w_ob, w_out, final_g)


def _rope_partner(w, axis):
    q = MLA_ROPE // 4
    parts = [lax.slice_in_dim(w, n * q, (n + 1) * q, axis=axis) for n in range(4)]
    return jnp.concatenate([-parts[1], parts[0], -parts[3], parts[2]], axis=axis)


def _rope_tables():
    t = np.arange(SEQ)
    half = MLA_ROPE // 4
    freqs = (ROPE_THETA ** (-np.arange(half, dtype=np.float32) / half)).astype(np.float32)
    ang_r = (t // GRID_W).astype(np.float32)[:, None] * freqs[None, :]
    ang_c = (t % GRID_W).astype(np.float32)[:, None] * freqs[None, :]
    cos = np.concatenate([np.cos(ang_r)] * 2 + [np.cos(ang_c)] * 2, axis=-1).astype(np.float32)
    sin = np.concatenate([np.sin(ang_r)] * 2 + [np.sin(ang_c)] * 2, axis=-1).astype(np.float32)
    return cos, sin


def _np_lane_group(nope, rope):
    pad = np.zeros((rope.shape[0], LANES - MLA_NOPE - MLA_ROPE), np.float32)
    return np.concatenate([nope, rope, pad], axis=-1).astype(np.float32)


def _rope_rows(r):
    d = r.shape[1]
    return jnp.concatenate([jnp.zeros((MLA_NOPE, d), F32), r,
                            jnp.zeros((LANES - MLA_NOPE - MLA_ROPE, d), F32)], axis=0)


def kernel(x, c, ctx, c_ctx, w_mod, b_mod, norm_g, w_in, g_cq, w_uq, g_ckv, w_ukv, rpb,
           w_oa, w_ob, w_out, final_g):
    b, s, d = x.shape
    depth = w_mod.shape[0]
    assert depth == 1 and s == SEQ and d == D_MODEL
    mla_scale = (MLA_NOPE + MLA_ROPE) ** -0.5 * LOG2_E
    na_scale = NA_DIM ** -0.5 * LOG2_E

    wt = jnp.transpose(w_in[0])
    o = np.cumsum([0, KV_LORA, MLA_ROPE, HEADS * NA_DIM, HEADS * NA_DIM, Q_LORA, HEADS * MLA_V,
                   HEADS * NA_DIM, HEADS * NA_DIM, d, d])
    seg = [wt[o[n]:o[n + 1]] for n in range(10)]
    w_ckv, w_kr, w_kb, w_vb, w_cq, w_za, w_qb, w_zb, w_ga, w_gb = seg
    w_ext = jnp.concatenate([w_ckv, _rope_rows(w_kr), _rope_rows(_rope_partner(w_kr, 0)), w_kb, w_cq],
                            axis=0).astype(BF16)
    wt_ext = jnp.concatenate([w_vb, w_qb * na_scale], axis=0).astype(BF16)
    w_gate = jnp.concatenate([w_za, w_zb, w_ga, w_gb], axis=0).astype(BF16)

    wkv3 = w_ukv[0].reshape(KV_LORA, HEADS, MLA_NOPE + MLA_V)
    wk = jnp.concatenate([wkv3[:, :, :MLA_NOPE], jnp.zeros((KV_LORA, HEADS, LANES - MLA_NOPE), F32)],
                         axis=-1).reshape(KV_LORA, HEADS * LANES).astype(BF16)
    wv = wkv3[:, :, MLA_NOPE:].reshape(KV_LORA, HEADS * MLA_V).astype(BF16).T
    wq3 = w_uq[0].reshape(Q_LORA, HEADS, MLA_NOPE + MLA_ROPE)
    zq = jnp.zeros((Q_LORA, HEADS, LANES - MLA_NOPE - MLA_ROPE), F32)
    wq = jnp.concatenate([wq3, zq], axis=-1).reshape(Q_LORA, HEADS * LANES).astype(BF16).T
    wqr = jnp.concatenate([jnp.zeros((Q_LORA, HEADS, MLA_NOPE), F32),
                           _rope_partner(wq3[:, :, MLA_NOPE:], 2), zq],
                          axis=-1).reshape(Q_LORA, HEADS * LANES).astype(BF16).T

    cos, sin = _rope_tables()
    ones_s = np.ones((s, MLA_NOPE), np.float32)
    zeros_s = np.zeros((s, MLA_NOPE), np.float32)
    cq_tab = jnp.asarray(np.ascontiguousarray((_np_lane_group(ones_s, cos) * np.float32(mla_scale)).T))
    sq_tab = jnp.asarray(np.ascontiguousarray((_np_lane_group(zeros_s, sin) * np.float32(mla_scale)).T))
    ck_tab = jnp.asarray(_np_lane_group(zeros_s, cos))
    sk_tab = jnp.asarray(_np_lane_group(zeros_s, sin))
    zeros_c = np.zeros((CTX_LEN, MLA_NOPE), np.float32)
    ck_ctx = jnp.asarray(_np_lane_group(zeros_c, np.ones((CTX_LEN, MLA_ROPE), np.float32)))
    sk_ctx = jnp.zeros((CTX_LEN, LANES), F32)

    cs = jnp.concatenate([c, c_ctx[None, :], jnp.zeros((8 - b - 1, d), F32)], axis=0)
    mod = _mod_call(cs, w_mod[0], b_mod[0][None, :])

    ng = norm_g[0][None, :]
    gkv = g_ckv[0][None, :]
    k, v, kb, vb, q, qb = _proj_call(
        x, mod, None, ng, w_ext, wt_ext, gkv, wk, wv, ck_tab, sk_tab,
        q_side=(g_cq[0][None, :], wq, wqr, cq_tab, sq_tab))
    kc, vc, kcb, vcb = _proj_call(
        ctx, mod, b, ng, w_ext[:KV_EXT], wt_ext[:_R_VB[1]], gkv, wk, wv,
        ck_ctx, sk_ctx)

    oa = _mla_call(q, k, kc, v, vc)
    ob = _na_call(qb, kb, vb, kcb, vcb, _na_bias_tables(rpb[0]))

    return _out_call(oa, ob, x, mod, ng, w_gate,
                     w_oa[0].astype(BF16), w_ob[0].astype(BF16), w_out[0].astype(BF16),
                     final_g[None, :])
```

```python
import functools

import jax
import jax.numpy as jnp
import numpy as np
from jax import lax
from jax.experimental import pallas as pl
from jax.experimental.pallas import tpu as pltpu

D_MODEL = 1024
SEQ = 4096
GRID_W = 64
GRID_H = SEQ // GRID_W
CTX_LEN = 256
HEADS = 8
MLA_NOPE = 64
MLA_ROPE = 32
MLA_V = 64
Q_LORA = 256
KV_LORA = 128
NA_DIM = 64
NA_KR = 8
NA_KC = 16
ROPE_THETA = 10000.0
EPS = 1e-6

LANES = 128
HEAD_PAIRS = HEADS // 2
VMEM_LIMIT = 56 * 1024 * 1024

_C_CKV = (0, 128)
_C_KR = (128, 256)
_C_KRR = (256, 384)
_C_KB = (384, 896)
KV_EXT = 896
_C_CQ = (896, 1152)
_G_ZA = (0, 512)
_G_ZB = (512, 1024)
_G_GA = (1024, 2048)
_G_GB = (2048, 3072)
_R_VB = (0, 512)
_R_QB = (512, 1024)

NA_ROWS_PER_BLOCK = 4
NA_BLOCKS = GRID_H // NA_ROWS_PER_BLOCK
NA_TQ = NA_ROWS_PER_BLOCK * GRID_W
NA_SLAB_ROWS = 12
NA_SLAB = NA_SLAB_ROWS * GRID_W
NA_MASK = -1e30
LOG2_E = 1.4426950408889634

F32 = jnp.float32
BF16 = jnp.bfloat16


def _dot(a, b):
    return jnp.dot(a, b, preferred_element_type=F32)


def _dot_nt(a, b):
    return lax.dot_general(a, b, (((1,), (1,)), ((), ())), preferred_element_type=F32)


def _rms(x, g):
    return x * lax.rsqrt(jnp.mean(x * x, axis=-1, keepdims=True) + EPS) * g


def _mod_kernel(c_ref, w_ref, b_ref, o_ref):
    c = c_ref[...]
    sc = c * jax.nn.sigmoid(c)
    o_ref[0, :, 0, :] = _dot(sc.astype(BF16), w_ref[...].astype(BF16)) + b_ref[...]


def _mod_call(cs, w_mod, b_mod):
    n = w_mod.shape[1]
    tn = 1024
    return pl.pallas_call(
        _mod_kernel,
        grid=(n // tn,),
        in_specs=[
            pl.BlockSpec((8, D_MODEL), lambda j: (0, 0)),
            pl.BlockSpec((D_MODEL, tn), lambda j: (0, j)),
            pl.BlockSpec((1, tn), lambda j: (0, j)),
        ],
        out_specs=pl.BlockSpec((1, 8, 1, tn), lambda j: (j, 0, 0, 0)),
        out_shape=jax.ShapeDtypeStruct((n // tn, 8, 1, tn), F32),
        name="adaln_mod",
    )(cs, w_mod, b_mod)


def _proj_kernel(with_q, x_ref, shift_ref, scale_ref, ng_ref, w_ref, wt_ref, gkv_ref, wk_ref, wv_ref,
                 ck_ref, sk_ref, *rest):
    if with_q:
        (gq_ref, wq_ref, wqr_ref, cq_ref, sq_ref,
         k_ref, v_ref, kb_ref, vb_ref, q_ref, qb_ref) = rest
    else:
        k_ref, v_ref, kb_ref, vb_ref = rest

    x = x_ref[0]
    h = _rms(x, ng_ref[...]) * (1.0 + scale_ref[0, 0]) + shift_ref[0, 0]
    hb = h.astype(BF16)

    def proj(c):
        return _dot_nt(hb, w_ref[c[0]:c[1], :])

    def proj_t(r):
        return _dot_nt(wt_ref[r[0]:r[1], :], hb)

    ckvn = _rms(proj(_C_CKV), gkv_ref[...]).astype(BF16)
    if with_q:
        cqn = _rms(proj(_C_CQ), gq_ref[...]).astype(BF16)
    krope = proj(_C_KR) * ck_ref[...] + proj(_C_KRR) * sk_ref[...]
    kb_ref[0] = proj(_C_KB).astype(BF16)
    vb_ref[0] = proj_t(_R_VB).astype(BF16)
    if with_q:
        qb_ref[0] = proj_t(_R_QB).astype(BF16)

    knope = _dot(ckvn, wk_ref[...])
    for hd in range(HEADS):
        sl = slice(hd * LANES, (hd + 1) * LANES)
        k_ref[0, :, sl] = (knope[:, sl] + krope).astype(BF16)
    v_ref[0] = _dot_nt(wv_ref[...], ckvn).astype(BF16)
    if with_q:
        q1 = _dot_nt(wq_ref[...], cqn)
        q2 = _dot_nt(wqr_ref[...], cqn)
        cq = cq_ref[...]
        sq = sq_ref[...]
        for hd in range(HEADS):
            sl = slice(hd * LANES, (hd + 1) * LANES)
            q_ref[0, sl, :] = (q1[sl] * cq + q2[sl] * sq).astype(BF16)


def _mod_spec(which, row=None):
    d = D_MODEL
    if row is None:
        return pl.BlockSpec((1, 1, 1, d), lambda i, j: (which, i, 0, 0))
    return pl.BlockSpec((1, 1, 1, d), lambda i, j: (which, row, 0, 0))


MOD_SHIFT, MOD_SCALE, MOD_GATE = 0, 1, 2


def _proj_call(x, mod, mod_row, norm_g, w_ext, wt_ext, g_ckv, wk, wv, ck, sk,
               q_side=None, tm=256):
    b, s, d = x.shape
    tm = min(tm, s)
    with_q = q_side is not None
    n_ext = w_ext.shape[0]
    n_t = wt_ext.shape[0]
    const2 = lambda i, j: (0, 0)
    tok3 = lambda i, j: (i, j, 0)
    tab = lambda i, j: (j, 0)

    in_specs = [
        pl.BlockSpec((1, tm, d), tok3),
        _mod_spec(MOD_SHIFT, mod_row),
        _mod_spec(MOD_SCALE, mod_row),
        pl.BlockSpec((1, d), const2),
        pl.BlockSpec((n_ext, d), const2),
        pl.BlockSpec((n_t, d), const2),
        pl.BlockSpec((1, KV_LORA), const2),
        pl.BlockSpec((KV_LORA, HEADS * LANES), const2),
        pl.BlockSpec((HEADS * MLA_V, KV_LORA), const2),
        pl.BlockSpec((tm, LANES), tab),
        pl.BlockSpec((tm, LANES), tab),
    ]
    args = [x, mod, mod, norm_g, w_ext, wt_ext, g_ckv, wk, wv, ck, sk]
    tok_major = lambda w: (pl.BlockSpec((1, tm, w), tok3), jax.ShapeDtypeStruct((b, s, w), BF16))
    feat_major = lambda w: (pl.BlockSpec((1, w, tm), lambda i, j: (i, 0, j)),
                            jax.ShapeDtypeStruct((b, w, s), BF16))
    outs = [tok_major(HEADS * LANES), feat_major(HEADS * MLA_V), tok_major(HEADS * NA_DIM),
            feat_major(HEADS * NA_DIM)]
    if with_q:
        g_cq, wq, wqr, cq, sq = q_side
        in_specs += [
            pl.BlockSpec((1, Q_LORA), const2),
            pl.BlockSpec((HEADS * LANES, Q_LORA), const2),
            pl.BlockSpec((HEADS * LANES, Q_LORA), const2),
            pl.BlockSpec((LANES, tm), lambda i, j: (0, j)),
            pl.BlockSpec((LANES, tm), lambda i, j: (0, j)),
        ]
        args += [g_cq, wq, wqr, cq, sq]
        outs += [feat_major(HEADS * LANES), feat_major(HEADS * NA_DIM)]
    out_specs = [o[0] for o in outs]
    out_shape = [o[1] for o in outs]
    return pl.pallas_call(
        functools.partial(_proj_kernel, with_q),
        grid=(b, s // tm),
        in_specs=in_specs,
        out_specs=out_specs,
        out_shape=out_shape,
        compiler_params=pltpu.CompilerParams(vmem_limit_bytes=VMEM_LIMIT),
        name="in_proj_q" if with_q else "in_proj_ctx",
    )(*args)


BF16_SUBLANES = 16
MLA_TQ = 512
MLA_TILES = 2
MLA_CHUNK = 512
MLA_QK_CHUNKS = 1
MLA_LOOKAHEAD = 2


def _mla_kernel(qt_ref, k_ref, kc_ref, vt_ref, vct_ref, o_ref):
    s_len = k_ref.shape[1]
    streams = [(t, hh) for t in range(MLA_TILES) for hh in range(2)]
    qts = {(t, hh): qt_ref[0, hh * LANES:(hh + 1) * LANES, t * MLA_TQ:(t + 1) * MLA_TQ] for t, hh in streams}
    qk_keys = MLA_QK_CHUNKS * MLA_CHUNK
    groups = []
    for t in range(MLA_TILES):
        groups += [((t, hh), k_ref, vt_ref, c * qk_keys, qk_keys)
                   for c in range(s_len // qk_keys) for hh in range(2)]
        groups += [((t, hh), kc_ref, vct_ref, 0, kc_ref.shape[1]) for hh in range(2)]
    items = [(st, vr, off + i, min(MLA_CHUNK, n - i))
             for st, _, vr, off, n in groups for i in range(0, n, MLA_CHUNK)]

    def group_scores(g):
        st, kr, _, off, n = g
        hh = st[1]
        big = _dot(kr[0, off:off + n, hh * LANES:(hh + 1) * LANES], qts[st]).astype(BF16)
        return [big[i:i + MLA_CHUNK] for i in range(0, n, MLA_CHUNK)]

    m = {st: None for st in streams}
    acc = {st: None for st in streams}
    pending = []
    gi = 0
    last_item = {st[0]: idx for idx, (st, _, _, _) in enumerate(items)}
    for idx, (st, vr, off, n) in enumerate(items):
        while len(pending) <= MLA_LOOKAHEAD and gi < len(groups):
            pending.extend(group_scores(groups[gi]))
            gi += 1
        s = pending.pop(0)
        hh = st[1]
        mc = jnp.max(s, axis=0, keepdims=True)
        m_new = mc if m[st] is None else jnp.maximum(m[st], mc)
        p = jnp.exp2(s - m_new)
        v_ones = jnp.concatenate([vr[0, hh * MLA_V:(hh + 1) * MLA_V, off:off + n],
                                  jnp.ones((BF16_SUBLANES, n), BF16)], axis=0)
        pv = _dot(v_ones, p)
        if acc[st] is None:
            acc[st] = pv
        else:
            acc[st] = acc[st] * jnp.exp2(m[st].astype(F32) - m_new.astype(F32)) + pv
        m[st] = m_new
        t = st[0]
        if idx == last_item[t]:
            outs = [acc[(t, h2)][:MLA_V] / acc[(t, h2)][MLA_V:MLA_V + 1] for h2 in range(2)]
            o_ref[0, t * MLA_TQ:(t + 1) * MLA_TQ, :] = jnp.concatenate(outs, axis=0).T.astype(BF16)


def _mla_call(qt, k, kc, vt, vct):
    b, _, s = qt.shape
    t = kc.shape[1]
    step_q = MLA_TILES * MLA_TQ
    return pl.pallas_call(
        _mla_kernel,
        grid=(b, HEAD_PAIRS, s // step_q),
        in_specs=[
            pl.BlockSpec((1, 2 * LANES, step_q), lambda i, p, j: (i, p, j)),
            pl.BlockSpec((1, s, 2 * LANES), lambda i, p, j: (i, 0, p)),
            pl.BlockSpec((1, t, 2 * LANES), lambda i, p, j: (i, 0, p)),
            pl.BlockSpec((1, 2 * MLA_V, s), lambda i, p, j: (i, p, 0)),
            pl.BlockSpec((1, 2 * MLA_V, t), lambda i, p, j: (i, p, 0)),
        ],
        out_specs=pl.BlockSpec((1, step_q, LANES), lambda i, p, j: (i, j, p)),
        out_shape=jax.ShapeDtypeStruct((b, s, HEADS * MLA_V), BF16),
        compiler_params=pltpu.CompilerParams(vmem_limit_bytes=VMEM_LIMIT),
        name="mla_attention",
    )(qt, k, kc, vt, vct)


NA_HEADS_PER_STEP = 8
NA_LOOKAHEAD = 3


def _na_slab_start(blk):
    return jnp.clip(blk * NA_ROWS_PER_BLOCK - NA_KR // 2, 0, GRID_H - NA_SLAB_ROWS)


def _na_variant(blk):
    return jnp.where(blk == 0, 0, jnp.where(blk == NA_BLOCKS - 1, 2, 1))


def _with_ones(v):
    return jnp.concatenate([v, jnp.ones((BF16_SUBLANES, v.shape[1]), v.dtype)], axis=0)


def _na_kernel(qt_ref, k_ref, vt_ref, kc_ref, vct_ref, bias_ref, o_ref):
    blk = pl.program_id(2)
    start = pl.multiple_of(_na_slab_start(blk) * GRID_W, NA_TQ)
    row_half = lax.broadcasted_iota(jnp.int32, (LANES, NA_TQ), 0) // NA_DIM

    k_all = [jnp.concatenate([k_ref[0, pl.ds(start, NA_SLAB), pair * LANES:(pair + 1) * LANES],
                              kc_ref[0, :, pair * LANES:(pair + 1) * LANES]], axis=0)
             for pair in range(NA_HEADS_PER_STEP // 2)]

    def scores(h):
        pair, hh = divmod(h, 2)
        qt = qt_ref[0, pair * LANES:(pair + 1) * LANES, :]
        qm = jnp.where(row_half == hh, qt, jnp.zeros_like(qt))
        s = _dot(k_all[pair], qm).astype(BF16)
        return jnp.concatenate([s[:NA_SLAB] + bias_ref[0, h], s[NA_SLAB:]], axis=0)

    outs = []
    ahead = [scores(h) for h in range(NA_LOOKAHEAD)]
    for h in range(NA_HEADS_PER_STEP):
        if h + NA_LOOKAHEAD < NA_HEADS_PER_STEP:
            ahead.append(scores(h + NA_LOOKAHEAD))
        s = ahead.pop(0)
        p = jnp.exp2(s - jnp.max(s, axis=0, keepdims=True))
        rows = slice(h * NA_DIM, (h + 1) * NA_DIM)
        v_all = jnp.concatenate([vt_ref[0, rows, pl.ds(start, NA_SLAB)], vct_ref[0, rows, :]], axis=1)
        pv = _dot(_with_ones(v_all), p)
        outs.append(pv[:NA_DIM] / pv[NA_DIM:NA_DIM + 1])
    o_ref[0] = jnp.concatenate(outs, axis=0).T.astype(BF16)


def _na_call(qbt, kb, vbt, kcb, vcbt, bias):
    b, s, _ = kb.shape
    t = kcb.shape[1]
    wide = NA_HEADS_PER_STEP * NA_DIM
    return pl.pallas_call(
        _na_kernel,
        grid=(HEADS // NA_HEADS_PER_STEP, b, NA_BLOCKS),
        in_specs=[
            pl.BlockSpec((1, wide, NA_TQ), lambda g, i, j: (i, g, j)),
            pl.BlockSpec((1, s, wide), lambda g, i, j: (i, 0, g)),
            pl.BlockSpec((1, wide, s), lambda g, i, j: (i, g, 0)),
            pl.BlockSpec((1, t, wide), lambda g, i, j: (i, 0, g)),
            pl.BlockSpec((1, wide, t), lambda g, i, j: (i, g, 0)),
            pl.BlockSpec((1, NA_HEADS_PER_STEP, NA_SLAB, NA_TQ),
                         lambda g, i, j: (_na_variant(j), g, 0, 0)),
        ],
        out_specs=pl.BlockSpec((1, NA_TQ, wide), lambda g, i, j: (i, j, g)),
        out_shape=jax.ShapeDtypeStruct((b, s, HEADS * NA_DIM), BF16),
        compiler_params=pltpu.CompilerParams(vmem_limit_bytes=VMEM_LIMIT),
        name="na_attention",
    )(qbt, kb, vbt, kcb, vcbt, bias)


def _na_bias_kernel(rpb_ref, o_ref):
    rp = pltpu.roll(rpb_ref[0] * LOG2_E, LANES - (NA_KC - 1), axis=1)
    shape = (GRID_W, LANES)
    wk = lax.broadcasted_iota(jnp.int32, shape, 0)
    lane = lax.broadcasted_iota(jnp.int32, shape, 1)
    c_start = jnp.clip(lane - NA_KC // 2, 0, GRID_W - NA_KC)
    left = lane < GRID_W
    mask_l = jnp.where(left, NA_MASK, 0.0).astype(F32)
    mask_r = jnp.where(left, 0.0, NA_MASK).astype(F32)
    tiles_l, tiles_r = [], []
    for d in range(2 * NA_KR - 1):
        row = jnp.broadcast_to(rp[d:d + 1, :], shape)
        t = pltpu.roll(row, 0, axis=1, stride=1, stride_axis=0)
        t = jnp.where(wk >= c_start, jnp.where(wk < c_start + NA_KC, t, NA_MASK), NA_MASK)
        t = jnp.where(left, t, 0.0)
        tiles_l.append(t)
        tiles_r.append(pltpu.roll(t, GRID_W, axis=1))
    for n, blk in enumerate((0, 1, NA_BLOCKS - 1)):
        sb = min(max(blk * NA_ROWS_PER_BLOCK - NA_KR // 2, 0), GRID_H - NA_SLAB_ROWS)
        for j in range(NA_SLAB_ROWS):
            kr = sb + j

            def pick(i, tiles, masked):
                r = blk * NA_ROWS_PER_BLOCK + i
                r_start = min(max(r - NA_KR // 2, 0), GRID_H - NA_KR)
                if r_start <= kr < r_start + NA_KR:
                    return tiles[kr - r + NA_KR - 1]
                return masked

            for ii in range(NA_ROWS_PER_BLOCK // 2):
                o_ref[n, 0, j * GRID_W:(j + 1) * GRID_W, ii * LANES:(ii + 1) * LANES] = (
                    pick(2 * ii, tiles_l, mask_l) + pick(2 * ii + 1, tiles_r, mask_r)).astype(BF16)


def _na_bias_tables(rpb):
    rpb_p = jnp.pad(rpb[:, :, ::-1], ((0, 0), (0, 1), (0, LANES - (2 * NA_KC - 1))))
    return pl.pallas_call(
        _na_bias_kernel,
        grid=(HEADS,),
        in_specs=[pl.BlockSpec((1, 2 * NA_KR, LANES), lambda h: (h, 0, 0))],
        out_specs=pl.BlockSpec((3, 1, NA_SLAB, NA_TQ), lambda h: (0, h, 0, 0)),
        out_shape=jax.ShapeDtypeStruct((3, HEADS, NA_SLAB, NA_TQ), BF16),
        name="na_bias_table",
    )(rpb_p)


def _out_kernel(oa_ref, ob_ref, x_ref, shift_ref, scale_ref, gate_ref, ng_ref, wg_ref,
                woa_ref, wob_ref, wout_ref, fg_ref, o_ref):
    subs = [slice(i, i + OUT_SUB) for i in range(0, o_ref.shape[1], OUT_SUB)]

    def gates(sl):
        h = _rms(x_ref[0, sl], ng_ref[...]) * (1.0 + scale_ref[0, 0]) + shift_ref[0, 0]
        hb = h.astype(BF16)
        proj = lambda c: _dot_nt(hb, wg_ref[c[0]:c[1], :])
        za = proj(_G_ZA)
        zb = proj(_G_ZB)
        return ((za * jax.nn.sigmoid(za)).astype(BF16), (zb * jax.nn.sigmoid(zb)).astype(BF16),
                jax.nn.sigmoid(proj(_G_GA)).astype(BF16), jax.nn.sigmoid(proj(_G_GB)).astype(BF16))

    def branches(sl, g):
        sza, szb, sga, sgb = g
        ya = _dot(oa_ref[0, sl] * sza, woa_ref[...])
        yb = _dot(ob_ref[0, sl] * szb, wob_ref[...])
        return (sga.astype(F32) * ya + sgb.astype(F32) * yb).astype(BF16)

    def finish(sl, mix):
        y = _dot(mix, wout_ref[...])
        r = x_ref[0, sl] + gate_ref[0, 0] * y
        o_ref[0, sl] = _rms(r, fg_ref[...])

    order = sorted([(3 * i + off, stage, i) for i in range(len(subs))
                    for stage, off in (("gates", 0), ("branches", 4), ("finish", 8))])
    g, mix = {}, {}
    for _, stage, i in order:
        if stage == "gates":
            g[i] = gates(subs[i])
        elif stage == "branches":
            mix[i] = branches(subs[i], g.pop(i))
        else:
            finish(subs[i], mix.pop(i))


OUT_SUB = 256


def _out_call(oa, ob, x, mod, norm_g, w_gate, w_oa, w_ob, w_out, final_g, tm=1024):
    b, s, d = x.shape
    tok3 = lambda i, j: (i, j, 0)
    half = HEADS * MLA_V

    def const(shape):
        return pl.BlockSpec(shape, lambda i, j: (0, 0), pipeline_mode=pl.Buffered(1))

    return pl.pallas_call(
        _out_kernel,
        grid=(b, s // tm),
        in_specs=[
            pl.BlockSpec((1, tm, half), tok3),
            pl.BlockSpec((1, tm, half), tok3),
            pl.BlockSpec((1, tm, d), tok3),
            _mod_spec(MOD_SHIFT),
            _mod_spec(MOD_SCALE),
            _mod_spec(MOD_GATE),
            const((1, d)),
            const((w_gate.shape[0], d)),
            const((half, d)),
            const((half, d)),
            const((d, d)),
            const((1, d)),
        ],
        out_specs=pl.BlockSpec((1, tm, d), tok3),
        out_shape=jax.ShapeDtypeStruct((b, s, d), F32),
        compiler_params=pltpu.CompilerParams(vmem_limit_bytes=VMEM_LIMIT),
        name="gated_out",
    )(oa, ob, x, mod, mod, mod, norm_g, w_gate, w_oa, w_ob, w_out, final_g)


def _rope_partner(w, axis):
    q = MLA_ROPE // 4
    parts = [lax.slice_in_dim(w, n * q, (n + 1) * q, axis=axis) for n in range(4)]
    return jnp.concatenate([-parts[1], parts[0], -parts[3], parts[2]], axis=axis)


def _rope_tables():
    t = np.arange(SEQ)
    half = MLA_ROPE // 4
    freqs = (ROPE_THETA ** (-np.arange(half, dtype=np.float32) / half)).astype(np.float32)
    ang_r = (t // GRID_W).astype(np.float32)[:, None] * freqs[None, :]
    ang_c = (t % GRID_W).astype(np.float32)[:, None] * freqs[None, :]
    cos = np.concatenate([np.cos(ang_r)] * 2 + [np.cos(ang_c)] * 2, axis=-1).astype(np.float32)
    sin = np.concatenate([np.sin(ang_r)] * 2 + [np.sin(ang_c)] * 2, axis=-1).astype(np.float32)
    return cos, sin


def _np_lane_group(nope, rope):
    pad = np.zeros((rope.shape[0], LANES - MLA_NOPE - MLA_ROPE), np.float32)
    return np.concatenate([nope, rope, pad], axis=-1).astype(np.float32)


def _rope_rows(r):
    d = r.shape[1]
    return jnp.concatenate([jnp.zeros((MLA_NOPE, d), F32), r,
                            jnp.zeros((LANES - MLA_NOPE - MLA_ROPE, d), F32)], axis=0)


def kernel(x, c, ctx, c_ctx, w_mod, b_mod, norm_g, w_in, g_cq, w_uq, g_ckv, w_ukv, rpb,
           w_oa, w_ob, w_out, final_g):
    b, s, d = x.shape
    depth = w_mod.shape[0]
    assert depth == 1 and s == SEQ and d == D_MODEL
    mla_scale = (MLA_NOPE + MLA_ROPE) ** -0.5 * LOG2_E
    na_scale = NA_DIM ** -0.5 * LOG2_E

    wt = jnp.transpose(w_in[0])
    o = np.cumsum([0, KV_LORA, MLA_ROPE, HEADS * NA_DIM, HEADS * NA_DIM, Q_LORA, HEADS * MLA_V,
                   HEADS * NA_DIM, HEADS * NA_DIM, d, d])
    seg = [wt[o[n]:o[n + 1]] for n in range(10)]
    w_ckv, w_kr, w_kb, w_vb, w_cq, w_za, w_qb, w_zb, w_ga, w_gb = seg
    w_ext = jnp.concatenate([w_ckv, _rope_rows(w_kr), _rope_rows(_rope_partner(w_kr, 0)), w_kb, w_cq],
                            axis=0).astype(BF16)
    wt_ext = jnp.concatenate([w_vb, w_qb * na_scale], axis=0).astype(BF16)
    w_gate = jnp.concatenate([w_za, w_zb, w_ga, w_gb], axis=0).astype(BF16)

    wkv3 = w_ukv[0].reshape(KV_LORA, HEADS, MLA_NOPE + MLA_V)
    wk = jnp.concatenate([wkv3[:, :, :MLA_NOPE], jnp.zeros((KV_LORA, HEADS, LANES - MLA_NOPE), F32)],
                         axis=-1).reshape(KV_LORA, HEADS * LANES).astype(BF16)
    wv = wkv3[:, :, MLA_NOPE:].reshape(KV_LORA, HEADS * MLA_V).astype(BF16).T
    wq3 = w_uq[0].reshape(Q_LORA, HEADS, MLA_NOPE + MLA_ROPE)
    zq = jnp.zeros((Q_LORA, HEADS, LANES - MLA_NOPE - MLA_ROPE), F32)
    wq = jnp.concatenate([wq3, zq], axis=-1).reshape(Q_LORA, HEADS * LANES).astype(BF16).T
    wqr = jnp.concatenate([jnp.zeros((Q_LORA, HEADS, MLA_NOPE), F32),
                           _rope_partner(wq3[:, :, MLA_NOPE:], 2), zq],
                          axis=-1).reshape(Q_LORA, HEADS * LANES).astype(BF16).T

    cos, sin = _rope_tables()
    ones_s = np.ones((s, MLA_NOPE), np.float32)
    zeros_s = np.zeros((s, MLA_NOPE), np.float32)
    cq_tab = jnp.asarray(np.ascontiguousarray((_np_lane_group(ones_s, cos) * np.float32(mla_scale)).T))
    sq_tab = jnp.asarray(np.ascontiguousarray((_np_lane_group(zeros_s, sin) * np.float32(mla_scale)).T))
    ck_tab = jnp.asarray(_np_lane_group(zeros_s, cos))
    sk_tab = jnp.asarray(_np_lane_group(zeros_s, sin))
    zeros_c = np.zeros((CTX_LEN, MLA_NOPE), np.float32)
    ck_ctx = jnp.asarray(_np_lane_group(zeros_c, np.ones((CTX_LEN, MLA_ROPE), np.float32)))
    sk_ctx = jnp.zeros((CTX_LEN, LANES), F32)

    cs = jnp.concatenate([c, c_ctx[None, :], jnp.zeros((8 - b - 1, d), F32)], axis=0)
    mod = _mod_call(cs, w_mod[0], b_mod[0][None, :])

    ng = norm_g[0][None, :]
    gkv = g_ckv[0][None, :]
    k, v, kb, vb, q, qb = _proj_call(
        x, mod, None, ng, w_ext, wt_ext, gkv, wk, wv, ck_tab, sk_tab,
        q_side=(g_cq[0][None, :], wq, wqr, cq_tab, sq_tab))
    kc, vc, kcb, vcb = _proj_call(
        ctx, mod, b, ng, w_ext[:KV_EXT], wt_ext[:_R_VB[1]], gkv, wk, wv,
        ck_ctx, sk_ctx)

    oa = _mla_call(q, k, kc, v, vc)
    ob = _na_call(qb, kb, vb, kcb, vcb, _na_bias_tables(rpb[0]))

    return _out_call(oa, ob, x, mod, ng, w_gate,
                     w_oa[0].astype(BF16), w_ob[0].astype(BF16), w_out[0].astype(BF16),
                     final_g[None, :])
```

```python
import functools

import jax
import jax.numpy as jnp
import numpy as np
from jax import lax
from jax.experimental import pallas as pl
from jax.experimental.pallas import tpu as pltpu

D_MODEL = 1024
SEQ = 4096
GRID_W = 64
GRID_H = SEQ // GRID_W
CTX_LEN = 256
HEADS = 8
MLA_NOPE = 64
MLA_ROPE = 32
MLA_V = 64
Q_LORA = 256
KV_LORA = 128
NA_DIM = 64
NA_KR = 8
NA_KC = 16
ROPE_THETA = 10000.0
EPS = 1e-6

LANES = 128
HEAD_PAIRS = HEADS // 2
VMEM_LIMIT = 56 * 1024 * 1024

_C_CKV = (0, 128)
_C_KR = (128, 256)
_C_KRR = (256, 384)
_C_KB = (384, 896)
KV_EXT = 896
_C_CQ = (896, 1152)
_G_ZA = (0, 512)
_G_ZB = (512, 1024)
_G_GA = (1024, 2048)
_G_GB = (2048, 3072)
_R_VB = (0, 512)
_R_QB = (512, 1024)

NA_ROWS_PER_BLOCK = 4
NA_BLOCKS = GRID_H // NA_ROWS_PER_BLOCK
NA_TQ = NA_ROWS_PER_BLOCK * GRID_W
NA_SLAB_ROWS = 12
NA_SLAB = NA_SLAB_ROWS * GRID_W
NA_MASK = -1e30
LOG2_E = 1.4426950408889634

F32 = jnp.float32
BF16 = jnp.bfloat16


def _dot(a, b):
    return jnp.dot(a, b, preferred_element_type=F32)


def _dot_nt(a, b):
    return lax.dot_general(a, b, (((1,), (1,)), ((), ())), preferred_element_type=F32)


def _rms(x, g):
    return x * lax.rsqrt(jnp.mean(x * x, axis=-1, keepdims=True) + EPS) * g


def _mod_kernel(c_ref, w_ref, b_ref, o_ref):
    c = c_ref[...]
    sc = c * jax.nn.sigmoid(c)
    o_ref[0, :, 0, :] = _dot(sc.astype(BF16), w_ref[...].astype(BF16)) + b_ref[...]


def _mod_call(cs, w_mod, b_mod):
    n = w_mod.shape[1]
    tn = 1024
    return pl.pallas_call(
        _mod_kernel,
        grid=(n // tn,),
        in_specs=[
            pl.BlockSpec((8, D_MODEL), lambda j: (0, 0)),
            pl.BlockSpec((D_MODEL, tn), lambda j: (0, j)),
            pl.BlockSpec((1, tn), lambda j: (0, j)),
        ],
        out_specs=pl.BlockSpec((1, 8, 1, tn), lambda j: (j, 0, 0, 0)),
        out_shape=jax.ShapeDtypeStruct((n // tn, 8, 1, tn), F32),
        name="adaln_mod",
    )(cs, w_mod, b_mod)


def _proj_kernel(with_q, x_ref, shift_ref, scale_ref, ng_ref, w_ref, wt_ref, gkv_ref, wk_ref, wv_ref,
                 ck_ref, sk_ref, *rest):
    if with_q:
        (gq_ref, wq_ref, wqr_ref, cq_ref, sq_ref,
         k_ref, v_ref, kb_ref, vb_ref, q_ref, qb_ref) = rest
    else:
        k_ref, v_ref, kb_ref, vb_ref = rest

    x = x_ref[0]
    h = _rms(x, ng_ref[...]) * (1.0 + scale_ref[0, 0]) + shift_ref[0, 0]
    hb = h.astype(BF16)

    def proj(c):
        return _dot_nt(hb, w_ref[c[0]:c[1], :])

    def proj_t(r):
        return _dot_nt(wt_ref[r[0]:r[1], :], hb)

    ckvn = _rms(proj(_C_CKV), gkv_ref[...]).astype(BF16)
    if with_q:
        cqn = _rms(proj(_C_CQ), gq_ref[...]).astype(BF16)
    krope = proj(_C_KR) * ck_ref[...] + proj(_C_KRR) * sk_ref[...]
    kb_ref[0] = proj(_C_KB).astype(BF16)
    vb_ref[0] = proj_t(_R_VB).astype(BF16)
    if with_q:
        qb_ref[0] = proj_t(_R_QB).astype(BF16)

    knope = _dot(ckvn, wk_ref[...])
    for hd in range(HEADS):
        sl = slice(hd * LANES, (hd + 1) * LANES)
        k_ref[0, :, sl] = (knope[:, sl] + krope).astype(BF16)
    v_ref[0] = _dot_nt(wv_ref[...], ckvn).astype(BF16)
    if with_q:
        q1 = _dot_nt(wq_ref[...], cqn)
        q2 = _dot_nt(wqr_ref[...], cqn)
        cq = cq_ref[...]
        sq = sq_ref[...]
        for hd in range(HEADS):
            sl = slice(hd * LANES, (hd + 1) * LANES)
            q_ref[0, sl, :] = (q1[sl] * cq + q2[sl] * sq).astype(BF16)


def _mod_spec(which, row=None):
    d = D_MODEL
    if row is None:
        return pl.BlockSpec((1, 1, 1, d), lambda i, j: (which, i, 0, 0))
    return pl.BlockSpec((1, 1, 1, d), lambda i, j: (which, row, 0, 0))


MOD_SHIFT, MOD_SCALE, MOD_GATE = 0, 1, 2


def _proj_call(x, mod, mod_row, norm_g, w_ext, wt_ext, g_ckv, wk, wv, ck, sk,
               q_side=None, tm=256):
    b, s, d = x.shape
    tm = min(tm, s)
    with_q = q_side is not None
    n_ext = w_ext.shape[0]
    n_t = wt_ext.shape[0]
    const2 = lambda i, j: (0, 0)
    tok3 = lambda i, j: (i, j, 0)
    tab = lambda i, j: (j, 0)

    in_specs = [
        pl.BlockSpec((1, tm, d), tok3),
        _mod_spec(MOD_SHIFT, mod_row),
        _mod_spec(MOD_SCALE, mod_row),
        pl.BlockSpec((1, d), const2),
        pl.BlockSpec((n_ext, d), const2),
        pl.BlockSpec((n_t, d), const2),
        pl.BlockSpec((1, KV_LORA), const2),
        pl.BlockSpec((KV_LORA, HEADS * LANES), const2),
        pl.BlockSpec((HEADS * MLA_V, KV_LORA), const2),
        pl.BlockSpec((tm, LANES), tab),
        pl.BlockSpec((tm, LANES), tab),
    ]
    args = [x, mod, mod, norm_g, w_ext, wt_ext, g_ckv, wk, wv, ck, sk]
    tok_major = lambda w: (pl.BlockSpec((1, tm, w), tok3), jax.ShapeDtypeStruct((b, s, w), BF16))
    feat_major = lambda w: (pl.BlockSpec((1, w, tm), lambda i, j: (i, 0, j)),
                            jax.ShapeDtypeStruct((b, w, s), BF16))
    outs = [tok_major(HEADS * LANES), feat_major(HEADS * MLA_V), tok_major(HEADS * NA_DIM),
            feat_major(HEADS * NA_DIM)]
    if with_q:
        g_cq, wq, wqr, cq, sq = q_side
        in_specs += [
            pl.BlockSpec((1, Q_LORA), const2),
            pl.BlockSpec((HEADS * LANES, Q_LORA), const2),
            pl.BlockSpec((HEADS * LANES, Q_LORA), const2),
            pl.BlockSpec((LANES, tm), lambda i, j: (0, j)),
            pl.BlockSpec((LANES, tm), lambda i, j: (0, j)),
        ]
        args += [g_cq, wq, wqr, cq, sq]
        outs += [feat_major(HEADS * LANES), feat_major(HEADS * NA_DIM)]
    out_specs = [o[0] for o in outs]
    out_shape = [o[1] for o in outs]
    return pl.pallas_call(
        functools.partial(_proj_kernel, with_q),
        grid=(b, s // tm),
        in_specs=in_specs,
        out_specs=out_specs,
        out_shape=out_shape,
        compiler_params=pltpu.CompilerParams(vmem_limit_bytes=VMEM_LIMIT),
        name="in_proj_q" if with_q else "in_proj_ctx",
    )(*args)


BF16_SUBLANES = 16
MLA_TQ = 512
MLA_TILES = 2
MLA_CHUNK = 256
MLA_QK_CHUNKS = 2
MLA_LOOKAHEAD = 3


def _mla_kernel(qt_ref, k_ref, kc_ref, vt_ref, vct_ref, o_ref):
    s_len = k_ref.shape[1]
    streams = [(t, hh) for t in range(MLA_TILES) for hh in range(2)]
    qts = {(t, hh): qt_ref[0, hh * LANES:(hh + 1) * LANES, t * MLA_TQ:(t + 1) * MLA_TQ] for t, hh in streams}
    qk_keys = MLA_QK_CHUNKS * MLA_CHUNK
    groups = []
    for t in range(MLA_TILES):
        groups += [((t, hh), k_ref, vt_ref, c * qk_keys, qk_keys)
                   for c in range(s_len // qk_keys) for hh in range(2)]
        groups += [((t, hh), kc_ref, vct_ref, 0, kc_ref.shape[1]) for hh in range(2)]
    items = [(st, vr, off + i, min(MLA_CHUNK, n - i))
             for st, _, vr, off, n in groups for i in range(0, n, MLA_CHUNK)]

    def group_scores(g):
        st, kr, _, off, n = g
        hh = st[1]
        big = _dot(kr[0, off:off + n, hh * LANES:(hh + 1) * LANES], qts[st]).astype(BF16)
        return [big[i:i + MLA_CHUNK] for i in range(0, n, MLA_CHUNK)]

    m = {st: None for st in streams}
    acc = {st: None for st in streams}
    pending = []
    gi = 0
    last_item = {st[0]: idx for idx, (st, _, _, _) in enumerate(items)}

    def weigh(idx, st, vr, off, n, p, alpha):
        hh = st[1]
        v_ones = jnp.concatenate([vr[0, hh * MLA_V:(hh + 1) * MLA_V, off:off + n],
                                  jnp.ones((BF16_SUBLANES, n), BF16)], axis=0)
        pv = _dot(v_ones, p)
        acc[st] = pv if alpha is None else acc[st] * alpha + pv
        t = st[0]
        if idx == last_item[t]:
            outs = [acc[(t, h2)][:MLA_V] / acc[(t, h2)][MLA_V:MLA_V + 1] for h2 in range(2)]
            o_ref[0, t * MLA_TQ:(t + 1) * MLA_TQ, :] = jnp.concatenate(outs, axis=0).T.astype(BF16)

    delayed = None
    for idx, (st, vr, off, n) in enumerate(items):
        while len(pending) <= MLA_LOOKAHEAD and gi < len(groups):
            pending.extend(group_scores(groups[gi]))
            gi += 1
        s = pending.pop(0)
        mc = jnp.max(s, axis=0, keepdims=True)
        m_new = mc if m[st] is None else jnp.maximum(m[st], mc)
        p = jnp.exp2(s - m_new)
        alpha = None if m[st] is None else jnp.exp2(m[st].astype(F32) - m_new.astype(F32))
        m[st] = m_new
        if delayed is not None:
            weigh(*delayed)
        delayed = (idx, st, vr, off, n, p, alpha)
    weigh(*delayed)


def _mla_call(qt, k, kc, vt, vct):
    b, _, s = qt.shape
    t = kc.shape[1]
    step_q = MLA_TILES * MLA_TQ
    return pl.pallas_call(
        _mla_kernel,
        grid=(b, HEAD_PAIRS, s // step_q),
        in_specs=[
            pl.BlockSpec((1, 2 * LANES, step_q), lambda i, p, j: (i, p, j)),
            pl.BlockSpec((1, s, 2 * LANES), lambda i, p, j: (i, 0, p)),
            pl.BlockSpec((1, t, 2 * LANES), lambda i, p, j: (i, 0, p)),
            pl.BlockSpec((1, 2 * MLA_V, s), lambda i, p, j: (i, p, 0)),
            pl.BlockSpec((1, 2 * MLA_V, t), lambda i, p, j: (i, p, 0)),
        ],
        out_specs=pl.BlockSpec((1, step_q, LANES), lambda i, p, j: (i, j, p)),
        out_shape=jax.ShapeDtypeStruct((b, s, HEADS * MLA_V), BF16),
        compiler_params=pltpu.CompilerParams(vmem_limit_bytes=VMEM_LIMIT),
        name="mla_attention",
    )(qt, k, kc, vt, vct)


NA_HEADS_PER_STEP = 8
NA_LOOKAHEAD = 3


def _na_slab_start(blk):
    return jnp.clip(blk * NA_ROWS_PER_BLOCK - NA_KR // 2, 0, GRID_H - NA_SLAB_ROWS)


def _na_variant(blk):
    return jnp.where(blk == 0, 0, jnp.where(blk == NA_BLOCKS - 1, 2, 1))


def _with_ones(v):
    return jnp.concatenate([v, jnp.ones((BF16_SUBLANES, v.shape[1]), v.dtype)], axis=0)


def _na_kernel(qt_ref, k_ref, vt_ref, kc_ref, vct_ref, bias_ref, o_ref):
    blk = pl.program_id(2)
    start = pl.multiple_of(_na_slab_start(blk) * GRID_W, NA_TQ)
    row_half = lax.broadcasted_iota(jnp.int32, (LANES, NA_TQ), 0) // NA_DIM

    k_all = [jnp.concatenate([k_ref[0, pl.ds(start, NA_SLAB), pair * LANES:(pair + 1) * LANES],
                              kc_ref[0, :, pair * LANES:(pair + 1) * LANES]], axis=0)
             for pair in range(NA_HEADS_PER_STEP // 2)]

    def scores(h):
        pair, hh = divmod(h, 2)
        qt = qt_ref[0, pair * LANES:(pair + 1) * LANES, :]
        qm = jnp.where(row_half == hh, qt, jnp.zeros_like(qt))
        s = _dot(k_all[pair], qm).astype(BF16)
        return jnp.concatenate([s[:NA_SLAB] + bias_ref[0, h], s[NA_SLAB:]], axis=0)

    outs = []
    ahead = [scores(h) for h in range(NA_LOOKAHEAD)]
    for h in range(NA_HEADS_PER_STEP):
        if h + NA_LOOKAHEAD < NA_HEADS_PER_STEP:
            ahead.append(scores(h + NA_LOOKAHEAD))
        s = ahead.pop(0)
        p = jnp.exp2(s - jnp.max(s, axis=0, keepdims=True))
        rows = slice(h * NA_DIM, (h + 1) * NA_DIM)
        v_all = jnp.concatenate([vt_ref[0, rows, pl.ds(start, NA_SLAB)], vct_ref[0, rows, :]], axis=1)
        pv = _dot(_with_ones(v_all), p)
        outs.append(pv[:NA_DIM] / pv[NA_DIM:NA_DIM + 1])
    o_ref[0] = jnp.concatenate(outs, axis=0).T.astype(BF16)


def _na_call(qbt, kb, vbt, kcb, vcbt, bias):
    b, s, _ = kb.shape
    t = kcb.shape[1]
    wide = NA_HEADS_PER_STEP * NA_DIM
    return pl.pallas_call(
        _na_kernel,
        grid=(HEADS // NA_HEADS_PER_STEP, b, NA_BLOCKS),
        in_specs=[
            pl.BlockSpec((1, wide, NA_TQ), lambda g, i, j: (i, g, j)),
            pl.BlockSpec((1, s, wide), lambda g, i, j: (i, 0, g)),
            pl.BlockSpec((1, wide, s), lambda g, i, j: (i, g, 0)),
            pl.BlockSpec((1, t, wide), lambda g, i, j: (i, 0, g)),
            pl.BlockSpec((1, wide, t), lambda g, i, j: (i, g, 0)),
            pl.BlockSpec((1, NA_HEADS_PER_STEP, NA_SLAB, NA_TQ),
                         lambda g, i, j: (_na_variant(j), g, 0, 0)),
        ],
        out_specs=pl.BlockSpec((1, NA_TQ, wide), lambda g, i, j: (i, j, g)),
        out_shape=jax.ShapeDtypeStruct((b, s, HEADS * NA_DIM), BF16),
        compiler_params=pltpu.CompilerParams(vmem_limit_bytes=VMEM_LIMIT),
        name="na_attention",
    )(qbt, kb, vbt, kcb, vcbt, bias)


def _na_bias_kernel(rpb_ref, o_ref):
    rp = pltpu.roll(rpb_ref[0] * LOG2_E, LANES - (NA_KC - 1), axis=1)
    shape = (GRID_W, LANES)
    wk = lax.broadcasted_iota(jnp.int32, shape, 0)
    lane = lax.broadcasted_iota(jnp.int32, shape, 1)
    c_start = jnp.clip(lane - NA_KC // 2, 0, GRID_W - NA_KC)
    left = lane < GRID_W
    mask_l = jnp.where(left, NA_MASK, 0.0).astype(F32)
    mask_r = jnp.where(left, 0.0, NA_MASK).astype(F32)
    tiles_l, tiles_r = [], []
    for d in range(2 * NA_KR - 1):
        row = jnp.broadcast_to(rp[d:d + 1, :], shape)
        t = pltpu.roll(row, 0, axis=1, stride=1, stride_axis=0)
        t = jnp.where(wk >= c_start, jnp.where(wk < c_start + NA_KC, t, NA_MASK), NA_MASK)
        t = jnp.where(left, t, 0.0)
        tiles_l.append(t)
        tiles_r.append(pltpu.roll(t, GRID_W, axis=1))
    for n, blk in enumerate((0, 1, NA_BLOCKS - 1)):
        sb = min(max(blk * NA_ROWS_PER_BLOCK - NA_KR // 2, 0), GRID_H - NA_SLAB_ROWS)
        for j in range(NA_SLAB_ROWS):
            kr = sb + j

            def pick(i, tiles, masked):
                r = blk * NA_ROWS_PER_BLOCK + i
                r_start = min(max(r - NA_KR // 2, 0), GRID_H - NA_KR)
                if r_start <= kr < r_start + NA_KR:
                    return tiles[kr - r + NA_KR - 1]
                return masked

            for ii in range(NA_ROWS_PER_BLOCK // 2):
                o_ref[n, 0, j * GRID_W:(j + 1) * GRID_W, ii * LANES:(ii + 1) * LANES] = (
                    pick(2 * ii, tiles_l, mask_l) + pick(2 * ii + 1, tiles_r, mask_r)).astype(BF16)


def _na_bias_tables(rpb):
    rpb_p = jnp.pad(rpb[:, :, ::-1], ((0, 0), (0, 1), (0, LANES - (2 * NA_KC - 1))))
    return pl.pallas_call(
        _na_bias_kernel,
        grid=(HEADS,),
        in_specs=[pl.BlockSpec((1, 2 * NA_KR, LANES), lambda h: (h, 0, 0))],
        out_specs=pl.BlockSpec((3, 1, NA_SLAB, NA_TQ), lambda h: (0, h, 0, 0)),
        out_shape=jax.ShapeDtypeStruct((3, HEADS, NA_SLAB, NA_TQ), BF16),
        name="na_bias_table",
    )(rpb_p)


def _out_kernel(oa_ref, ob_ref, x_ref, shift_ref, scale_ref, gate_ref, ng_ref, wg_ref,
                woa_ref, wob_ref, wout_ref, fg_ref, o_ref):
    subs = [slice(i, i + OUT_SUB) for i in range(0, o_ref.shape[1], OUT_SUB)]

    def gates(sl):
        h = _rms(x_ref[0, sl], ng_ref[...]) * (1.0 + scale_ref[0, 0]) + shift_ref[0, 0]
        hb = h.astype(BF16)
        proj = lambda c: _dot_nt(hb, wg_ref[c[0]:c[1], :])
        za = proj(_G_ZA)
        zb = proj(_G_ZB)
        return ((za * jax.nn.sigmoid(za)).astype(BF16), (zb * jax.nn.sigmoid(zb)).astype(BF16),
                jax.nn.sigmoid(proj(_G_GA)).astype(BF16), jax.nn.sigmoid(proj(_G_GB)).astype(BF16))

    def branches(sl, g):
        sza, szb, sga, sgb = g
        ya = _dot(oa_ref[0, sl] * sza, woa_ref[...])
        yb = _dot(ob_ref[0, sl] * szb, wob_ref[...])
        return (sga.astype(F32) * ya + sgb.astype(F32) * yb).astype(BF16)

    def finish(sl, mix):
        y = _dot(mix, wout_ref[...])
        r = x_ref[0, sl] + gate_ref[0, 0] * y
        o_ref[0, sl] = _rms(r, fg_ref[...])

    order = sorted([(3 * i + off, stage, i) for i in range(len(subs))
                    for stage, off in (("gates", 0), ("branches", 4), ("finish", 8))])
    g, mix = {}, {}
    for _, stage, i in order:
        if stage == "gates":
            g[i] = gates(subs[i])
        elif stage == "branches":
            mix[i] = branches(subs[i], g.pop(i))
        else:
            finish(subs[i], mix.pop(i))


OUT_SUB = 256


def _out_call(oa, ob, x, mod, norm_g, w_gate, w_oa, w_ob, w_out, final_g, tm=1024):
    b, s, d = x.shape
    tok3 = lambda i, j: (i, j, 0)
    half = HEADS * MLA_V

    def const(shape):
        return pl.BlockSpec(shape, lambda i, j: (0, 0), pipeline_mode=pl.Buffered(1))

    return pl.pallas_call(
        _out_kernel,
        grid=(b, s // tm),
        in_specs=[
            pl.BlockSpec((1, tm, half), tok3),
            pl.BlockSpec((1, tm, half), tok3),
            pl.BlockSpec((1, tm, d), tok3),
            _mod_spec(MOD_SHIFT),
            _mod_spec(MOD_SCALE),
            _mod_spec(MOD_GATE),
            const((1, d)),
            const((w_gate.shape[0], d)),
            const((half, d)),
            const((half, d)),
            const((d, d)),
            const((1, d)),
        ],
        out_specs=pl.BlockSpec((1, tm, d), tok3),
        out_shape=jax.ShapeDtypeStruct((b, s, d), F32),
        compiler_params=pltpu.CompilerParams(vmem_limit_bytes=VMEM_LIMIT),
        name="gated_out",
    )(oa, ob, x, mod, mod, mod, norm_g, w_gate, w_oa, w_ob, w_out, final_g)


def _rope_partner(w, axis):
    q = MLA_ROPE // 4
    parts = [lax.slice_in_dim(w, n * q, (n + 1) * q, axis=axis) for n in range(4)]
    return jnp.concatenate([-parts[1], parts[0], -parts[3], parts[2]], axis=axis)


def _rope_tables():
    t = np.arange(SEQ)
    half = MLA_ROPE // 4
    freqs = (ROPE_THETA ** (-np.arange(half, dtype=np.float32) / half)).astype(np.float32)
    ang_r = (t // GRID_W).astype(np.float32)[:, None] * freqs[None, :]
    ang_c = (t % GRID_W).astype(np.float32)[:, None] * freqs[None, :]
    cos = np.concatenate([np.cos(ang_r)] * 2 + [np.cos(ang_c)] * 2, axis=-1).astype(np.float32)
    sin = np.concatenate([np.sin(ang_r)] * 2 + [np.sin(ang_c)] * 2, axis=-1).astype(np.float32)
    return cos, sin


def _np_lane_group(nope, rope):
    pad = np.zeros((rope.shape[0], LANES - MLA_NOPE - MLA_ROPE), np.float32)
    return np.concatenate([nope, rope, pad], axis=-1).astype(np.float32)


def _rope_rows(r):
    d = r.shape[1]
    return jnp.concatenate([jnp.zeros((MLA_NOPE, d), F32), r,
                            jnp.zeros((LANES - MLA_NOPE - MLA_ROPE, d), F32)], axis=0)


def kernel(x, c, ctx, c_ctx, w_mod, b_mod, norm_g, w_in, g_cq, w_uq, g_ckv, w_ukv, rpb,
           w_oa, w_ob, w_out, final_g):
    b, s, d = x.shape
    depth = w_mod.shape[0]
    assert depth == 1 and s == SEQ and d == D_MODEL
    mla_scale = (MLA_NOPE + MLA_ROPE) ** -0.5 * LOG2_E
    na_scale = NA_DIM ** -0.5 * LOG2_E

    wt = jnp.transpose(w_in[0])
    o = np.cumsum([0, KV_LORA, MLA_ROPE, HEADS * NA_DIM, HEADS * NA_DIM, Q_LORA, HEADS * MLA_V,
                   HEADS * NA_DIM, HEADS * NA_DIM, d, d])
    seg = [wt[o[n]:o[n + 1]] for n in range(10)]
    w_ckv, w_kr, w_kb, w_vb, w_cq, w_za, w_qb, w_zb, w_ga, w_gb = seg
    w_ext = jnp.concatenate([w_ckv, _rope_rows(w_kr), _rope_rows(_rope_partner(w_kr, 0)), w_kb, w_cq],
                            axis=0).astype(BF16)
    wt_ext = jnp.concatenate([w_vb, w_qb * na_scale], axis=0).astype(BF16)
    w_gate = jnp.concatenate([w_za, w_zb, w_ga, w_gb], axis=0).astype(BF16)

    wkv3 = w_ukv[0].reshape(KV_LORA, HEADS, MLA_NOPE + MLA_V)
    wk = jnp.concatenate([wkv3[:, :, :MLA_NOPE], jnp.zeros((KV_LORA, HEADS, LANES - MLA_NOPE), F32)],
                         axis=-1).reshape(KV_LORA, HEADS * LANES).astype(BF16)
    wv = wkv3[:, :, MLA_NOPE:].reshape(KV_LORA, HEADS * MLA_V).astype(BF16).T
    wq3 = w_uq[0].reshape(Q_LORA, HEADS, MLA_NOPE + MLA_ROPE)
    zq = jnp.zeros((Q_LORA, HEADS, LANES - MLA_NOPE - MLA_ROPE), F32)
    wq = jnp.concatenate([wq3, zq], axis=-1).reshape(Q_LORA, HEADS * LANES).astype(BF16).T
    wqr = jnp.concatenate([jnp.zeros((Q_LORA, HEADS, MLA_NOPE), F32),
                           _rope_partner(wq3[:, :, MLA_NOPE:], 2), zq],
                          axis=-1).reshape(Q_LORA, HEADS * LANES).astype(BF16).T

    cos, sin = _rope_tables()
    ones_s = np.ones((s, MLA_NOPE), np.float32)
    zeros_s = np.zeros((s, MLA_NOPE), np.float32)
    cq_tab = jnp.asarray(np.ascontiguousarray((_np_lane_group(ones_s, cos) * np.float32(mla_scale)).T))
    sq_tab = jnp.asarray(np.ascontiguousarray((_np_lane_group(zeros_s, sin) * np.float32(mla_scale)).T))
    ck_tab = jnp.asarray(_np_lane_group(zeros_s, cos))
    sk_tab = jnp.asarray(_np_lane_group(zeros_s, sin))
    zeros_c = np.zeros((CTX_LEN, MLA_NOPE), np.float32)
    ck_ctx = jnp.asarray(_np_lane_group(zeros_c, np.ones((CTX_LEN, MLA_ROPE), np.float32)))
    sk_ctx = jnp.zeros((CTX_LEN, LANES), F32)

    cs = jnp.concatenate([c, c_ctx[None, :], jnp.zeros((8 - b - 1, d), F32)], axis=0)
    mod = _mod_call(cs, w_mod[0], b_mod[0][None, :])

    ng = norm_g[0][None, :]
    gkv = g_ckv[0][None, :]
    k, v, kb, vb, q, qb = _proj_call(
        x, mod, None, ng, w_ext, wt_ext, gkv, wk, wv, ck_tab, sk_tab,
        q_side=(g_cq[0][None, :], wq, wqr, cq_tab, sq_tab))
    kc, vc, kcb, vcb = _proj_call(
        ctx, mod, b, ng, w_ext[:KV_EXT], wt_ext[:_R_VB[1]], gkv, wk, wv,
        ck_ctx, sk_ctx)

    oa = _mla_call(q, k, kc, v, vc)
    ob = _na_call(qb, kb, vb, kcb, vcb, _na_bias_tables(rpb[0]))

    return _out_call(oa, ob, x, mod, ng, w_gate,
                     w_oa[0].astype(BF16), w_ob[0].astype(BF16), w_out[0].astype(BF16),
                     final_g[None, :])
```

```python
import functools

import jax
import jax.numpy as jnp
import numpy as np
from jax import lax
from jax.experimental import pallas as pl
from jax.experimental.pallas import tpu as pltpu

D_MODEL = 1024
SEQ = 4096
GRID_W = 64
GRID_H = SEQ // GRID_W
CTX_LEN = 256
HEADS = 8
MLA_NOPE = 64
MLA_ROPE = 32
MLA_V = 64
Q_LORA = 256
KV_LORA = 128
NA_DIM = 64
NA_KR = 8
NA_KC = 16
ROPE_THETA = 10000.0
EPS = 1e-6

LANES = 128
HEAD_PAIRS = HEADS // 2
VMEM_LIMIT = 56 * 1024 * 1024

_C_CKV = (0, 128)
_C_KR = (128, 256)
_C_KRR = (256, 384)
_C_KB = (384, 896)
KV_EXT = 896
_C_CQ = (896, 1152)
_G_ZA = (0, 512)
_G_ZB = (512, 1024)
_G_GA = (1024, 2048)
_G_GB = (2048, 3072)
_R_VB = (0, 512)
_R_QB = (512, 1024)

NA_ROWS_PER_BLOCK = 4
NA_BLOCKS = GRID_H // NA_ROWS_PER_BLOCK
NA_TQ = NA_ROWS_PER_BLOCK * GRID_W
NA_SLAB_ROWS = 12
NA_SLAB = NA_SLAB_ROWS * GRID_W
NA_MASK = -1e30
LOG2_E = 1.4426950408889634

F32 = jnp.float32
BF16 = jnp.bfloat16


def _dot(a, b):
    return jnp.dot(a, b, preferred_element_type=F32)


def _dot_nt(a, b):
    return lax.dot_general(a, b, (((1,), (1,)), ((), ())), preferred_element_type=F32)


def _rms(x, g):
    return x * lax.rsqrt(jnp.mean(x * x, axis=-1, keepdims=True) + EPS) * g


def _mod_kernel(c_ref, w_ref, b_ref, o_ref):
    c = c_ref[...]
    sc = c * jax.nn.sigmoid(c)
    o_ref[0, :, 0, :] = _dot(sc.astype(BF16), w_ref[...].astype(BF16)) + b_ref[...]


def _mod_call(cs, w_mod, b_mod):
    n = w_mod.shape[1]
    tn = 1024
    return pl.pallas_call(
        _mod_kernel,
        grid=(n // tn,),
        in_specs=[
            pl.BlockSpec((8, D_MODEL), lambda j: (0, 0)),
            pl.BlockSpec((D_MODEL, tn), lambda j: (0, j)),
            pl.BlockSpec((1, tn), lambda j: (0, j)),
        ],
        out_specs=pl.BlockSpec((1, 8, 1, tn), lambda j: (j, 0, 0, 0)),
        out_shape=jax.ShapeDtypeStruct((n // tn, 8, 1, tn), F32),
        name="adaln_mod",
    )(cs, w_mod, b_mod)


def _proj_kernel(with_q, x_ref, shift_ref, scale_ref, ng_ref, w_ref, wt_ref, gkv_ref, wk_ref, wv_ref,
                 ck_ref, sk_ref, *rest):
    if with_q:
        (gq_ref, wq_ref, wqr_ref, cq_ref, sq_ref,
         k_ref, v_ref, kb_ref, vb_ref, q_ref, qb_ref) = rest
    else:
        k_ref, v_ref, kb_ref, vb_ref = rest

    x = x_ref[0]
    h = _rms(x, ng_ref[...]) * (1.0 + scale_ref[0, 0]) + shift_ref[0, 0]
    hb = h.astype(BF16)

    def proj(c):
        return _dot_nt(hb, w_ref[c[0]:c[1], :])

    def proj_t(r):
        return _dot_nt(wt_ref[r[0]:r[1], :], hb)

    ckvn = _rms(proj(_C_CKV), gkv_ref[...]).astype(BF16)
    if with_q:
        cqn = _rms(proj(_C_CQ), gq_ref[...]).astype(BF16)
    krope = proj(_C_KR) * ck_ref[...] + proj(_C_KRR) * sk_ref[...]
    kb_ref[0] = proj(_C_KB).astype(BF16)
    vb_ref[0] = proj_t(_R_VB).astype(BF16)
    if with_q:
        qb_ref[0] = proj_t(_R_QB).astype(BF16)

    knope = _dot(ckvn, wk_ref[...])
    for hd in range(HEADS):
        sl = slice(hd * LANES, (hd + 1) * LANES)
        k_ref[0, :, sl] = (knope[:, sl] + krope).astype(BF16)
    v_ref[0] = _dot_nt(wv_ref[...], ckvn).astype(BF16)
    if with_q:
        q1 = _dot_nt(wq_ref[...], cqn)
        q2 = _dot_nt(wqr_ref[...], cqn)
        cq = cq_ref[...]
        sq = sq_ref[...]
        for hd in range(HEADS):
            sl = slice(hd * LANES, (hd + 1) * LANES)
            q_ref[0, sl, :] = (q1[sl] * cq + q2[sl] * sq).astype(BF16)


def _mod_spec(which, row=None):
    d = D_MODEL
    if row is None:
        return pl.BlockSpec((1, 1, 1, d), lambda i, j: (which, i, 0, 0))
    return pl.BlockSpec((1, 1, 1, d), lambda i, j: (which, row, 0, 0))


MOD_SHIFT, MOD_SCALE, MOD_GATE = 0, 1, 2


def _proj_call(x, mod, mod_row, norm_g, w_ext, wt_ext, g_ckv, wk, wv, ck, sk,
               q_side=None, tm=512):
    b, s, d = x.shape
    tm = min(tm, s)
    with_q = q_side is not None
    n_ext = w_ext.shape[0]
    n_t = wt_ext.shape[0]
    const2 = lambda i, j: (0, 0)
    tok3 = lambda i, j: (i, j, 0)
    tab = lambda i, j: (j, 0)

    in_specs = [
        pl.BlockSpec((1, tm, d), tok3),
        _mod_spec(MOD_SHIFT, mod_row),
        _mod_spec(MOD_SCALE, mod_row),
        pl.BlockSpec((1, d), const2),
        pl.BlockSpec((n_ext, d), const2),
        pl.BlockSpec((n_t, d), const2),
        pl.BlockSpec((1, KV_LORA), const2),
        pl.BlockSpec((KV_LORA, HEADS * LANES), const2),
        pl.BlockSpec((HEADS * MLA_V, KV_LORA), const2),
        pl.BlockSpec((tm, LANES), tab),
        pl.BlockSpec((tm, LANES), tab),
    ]
    args = [x, mod, mod, norm_g, w_ext, wt_ext, g_ckv, wk, wv, ck, sk]
    tok_major = lambda w: (pl.BlockSpec((1, tm, w), tok3), jax.ShapeDtypeStruct((b, s, w), BF16))
    feat_major = lambda w: (pl.BlockSpec((1, w, tm), lambda i, j: (i, 0, j)),
                            jax.ShapeDtypeStruct((b, w, s), BF16))
    outs = [tok_major(HEADS * LANES), feat_major(HEADS * MLA_V), tok_major(HEADS * NA_DIM),
            feat_major(HEADS * NA_DIM)]
    if with_q:
        g_cq, wq, wqr, cq, sq = q_side
        in_specs += [
            pl.BlockSpec((1, Q_LORA), const2),
            pl.BlockSpec((HEADS * LANES, Q_LORA), const2),
            pl.BlockSpec((HEADS * LANES, Q_LORA), const2),
            pl.BlockSpec((LANES, tm), lambda i, j: (0, j)),
            pl.BlockSpec((LANES, tm), lambda i, j: (0, j)),
        ]
        args += [g_cq, wq, wqr, cq, sq]
        outs += [feat_major(HEADS * LANES), feat_major(HEADS * NA_DIM)]
    out_specs = [o[0] for o in outs]
    out_shape = [o[1] for o in outs]
    return pl.pallas_call(
        functools.partial(_proj_kernel, with_q),
        grid=(b, s // tm),
        in_specs=in_specs,
        out_specs=out_specs,
        out_shape=out_shape,
        compiler_params=pltpu.CompilerParams(vmem_limit_bytes=VMEM_LIMIT),
        name="in_proj_q" if with_q else "in_proj_ctx",
    )(*args)


BF16_SUBLANES = 16
MLA_TQ = 512
MLA_TILES = 2
MLA_CHUNK = 256
MLA_QK_CHUNKS = 2
MLA_LOOKAHEAD = 3


def _mla_kernel(qt_ref, k_ref, kc_ref, vt_ref, vct_ref, o_ref):
    s_len = k_ref.shape[1]
    streams = [(t, hh) for t in range(MLA_TILES) for hh in range(2)]
    qts = {(t, hh): qt_ref[0, hh * LANES:(hh + 1) * LANES, t * MLA_TQ:(t + 1) * MLA_TQ] for t, hh in streams}
    qk_keys = MLA_QK_CHUNKS * MLA_CHUNK
    groups = []
    for t in range(MLA_TILES):
        groups += [((t, hh), k_ref, vt_ref, c * qk_keys, qk_keys)
                   for c in range(s_len // qk_keys) for hh in range(2)]
        groups += [((t, hh), kc_ref, vct_ref, 0, kc_ref.shape[1]) for hh in range(2)]
    items = [(st, vr, off + i, min(MLA_CHUNK, n - i))
             for st, _, vr, off, n in groups for i in range(0, n, MLA_CHUNK)]

    def group_scores(g):
        st, kr, _, off, n = g
        hh = st[1]
        big = _dot(kr[0, off:off + n, hh * LANES:(hh + 1) * LANES], qts[st]).astype(BF16)
        return [big[i:i + MLA_CHUNK] for i in range(0, n, MLA_CHUNK)]

    m = {st: None for st in streams}
    acc = {st: None for st in streams}
    pending = []
    gi = 0
    last_item = {st[0]: idx for idx, (st, _, _, _) in enumerate(items)}

    def weigh(idx, st, vr, off, n, p, alpha):
        hh = st[1]
        v_ones = jnp.concatenate([vr[0, hh * MLA_V:(hh + 1) * MLA_V, off:off + n],
                                  jnp.ones((BF16_SUBLANES, n), BF16)], axis=0)
        pv = _dot(v_ones, p)
        acc[st] = pv if alpha is None else acc[st] * alpha + pv
        t = st[0]
        if idx == last_item[t]:
            outs = [acc[(t, h2)][:MLA_V] / acc[(t, h2)][MLA_V:MLA_V + 1] for h2 in range(2)]
            o_ref[0, t * MLA_TQ:(t + 1) * MLA_TQ, :] = jnp.concatenate(outs, axis=0).T.astype(BF16)

    delayed = None
    for idx, (st, vr, off, n) in enumerate(items):
        while len(pending) <= MLA_LOOKAHEAD and gi < len(groups):
            pending.extend(group_scores(groups[gi]))
            gi += 1
        s = pending.pop(0)
        mc = jnp.max(s, axis=0, keepdims=True)
        m_new = mc if m[st] is None else jnp.maximum(m[st], mc)
        p = jnp.exp2(s - m_new)
        alpha = None if m[st] is None else jnp.exp2(m[st].astype(F32) - m_new.astype(F32))
        m[st] = m_new
        if delayed is not None:
            weigh(*delayed)
        delayed = (idx, st, vr, off, n, p, alpha)
    weigh(*delayed)


def _mla_call(qt, k, kc, vt, vct):
    b, _, s = qt.shape
    t = kc.shape[1]
    step_q = MLA_TILES * MLA_TQ
    return pl.pallas_call(
        _mla_kernel,
        grid=(b, HEAD_PAIRS, s // step_q),
        in_specs=[
            pl.BlockSpec((1, 2 * LANES, step_q), lambda i, p, j: (i, p, j)),
            pl.BlockSpec((1, s, 2 * LANES), lambda i, p, j: (i, 0, p)),
            pl.BlockSpec((1, t, 2 * LANES), lambda i, p, j: (i, 0, p)),
            pl.BlockSpec((1, 2 * MLA_V, s), lambda i, p, j: (i, p, 0)),
            pl.BlockSpec((1, 2 * MLA_V, t), lambda i, p, j: (i, p, 0)),
        ],
        out_specs=pl.BlockSpec((1, step_q, LANES), lambda i, p, j: (i, j, p)),
        out_shape=jax.ShapeDtypeStruct((b, s, HEADS * MLA_V), BF16),
        compiler_params=pltpu.CompilerParams(vmem_limit_bytes=VMEM_LIMIT),
        name="mla_attention",
    )(qt, k, kc, vt, vct)


NA_HEADS_PER_STEP = 8
NA_BLOCKS_PER_STEP = 4
NA_LOOKAHEAD = 3


def _na_slab_start(blk):
    return jnp.clip(blk * NA_ROWS_PER_BLOCK - NA_KR // 2, 0, GRID_H - NA_SLAB_ROWS)


def _na_variant(blk):
    return jnp.where(blk == 0, 0, jnp.where(blk == NA_BLOCKS - 1, 2, 1))


def _with_ones(v):
    return jnp.concatenate([v, jnp.ones((BF16_SUBLANES, v.shape[1]), v.dtype)], axis=0)


def _na_kernel(qt_ref, k_ref, vt_ref, kc_ref, vct_ref, *rest):
    bias_refs, o_ref = rest[:-1], rest[-1]
    row_half = lax.broadcasted_iota(jnp.int32, (LANES, NA_TQ), 0) // NA_DIM
    starts = [pl.multiple_of(_na_slab_start(pl.program_id(2) * NA_BLOCKS_PER_STEP + n) * GRID_W, NA_TQ)
              for n in range(NA_BLOCKS_PER_STEP)]
    items = [(n, h) for n in range(NA_BLOCKS_PER_STEP) for h in range(NA_HEADS_PER_STEP)]

    def scores(item):
        n, h = item
        pair, hh = divmod(h, 2)
        cols = slice(pair * LANES, (pair + 1) * LANES)
        k_all = jnp.concatenate([k_ref[0, pl.ds(starts[n], NA_SLAB), cols], kc_ref[0, :, cols]], axis=0)
        qt = qt_ref[0, cols, n * NA_TQ:(n + 1) * NA_TQ]
        qm = jnp.where(row_half == hh, qt, jnp.zeros_like(qt))
        s = _dot(k_all, qm).astype(BF16)
        return jnp.concatenate([s[:NA_SLAB] + bias_refs[n][0, h], s[NA_SLAB:]], axis=0)

    outs = []
    ahead = [scores(it) for it in items[:NA_LOOKAHEAD]]
    for idx, (n, h) in enumerate(items):
        if idx + NA_LOOKAHEAD < len(items):
            ahead.append(scores(items[idx + NA_LOOKAHEAD]))
        s = ahead.pop(0)
        p = jnp.exp2(s - jnp.max(s, axis=0, keepdims=True))
        rows = slice(h * NA_DIM, (h + 1) * NA_DIM)
        v_all = jnp.concatenate([vt_ref[0, rows, pl.ds(starts[n], NA_SLAB)], vct_ref[0, rows, :]], axis=1)
        pv = _dot(_with_ones(v_all), p)
        outs.append(pv[:NA_DIM] / pv[NA_DIM:NA_DIM + 1])
        if h == NA_HEADS_PER_STEP - 1:
            o_ref[0, n * NA_TQ:(n + 1) * NA_TQ, :] = jnp.concatenate(outs, axis=0).T.astype(BF16)
            outs = []


def _na_call(qbt, kb, vbt, kcb, vcbt, bias):
    b, s, _ = kb.shape
    t = kcb.shape[1]
    wide = NA_HEADS_PER_STEP * NA_DIM
    step_q = NA_BLOCKS_PER_STEP * NA_TQ

    def bias_spec(n):
        return pl.BlockSpec((1, NA_HEADS_PER_STEP, NA_SLAB, NA_TQ),
                            lambda g, i, j: (_na_variant(j * NA_BLOCKS_PER_STEP + n), g, 0, 0))

    return pl.pallas_call(
        _na_kernel,
        grid=(HEADS // NA_HEADS_PER_STEP, b, NA_BLOCKS // NA_BLOCKS_PER_STEP),
        in_specs=[
            pl.BlockSpec((1, wide, step_q), lambda g, i, j: (i, g, j)),
            pl.BlockSpec((1, s, wide), lambda g, i, j: (i, 0, g)),
            pl.BlockSpec((1, wide, s), lambda g, i, j: (i, g, 0)),
            pl.BlockSpec((1, t, wide), lambda g, i, j: (i, 0, g)),
            pl.BlockSpec((1, wide, t), lambda g, i, j: (i, g, 0)),
        ] + [bias_spec(n) for n in range(NA_BLOCKS_PER_STEP)],
        out_specs=pl.BlockSpec((1, step_q, wide), lambda g, i, j: (i, j, g)),
        out_shape=jax.ShapeDtypeStruct((b, s, HEADS * NA_DIM), BF16),
        compiler_params=pltpu.CompilerParams(vmem_limit_bytes=VMEM_LIMIT),
        name="na_attention",
    )(qbt, kb, vbt, kcb, vcbt, *([bias] * NA_BLOCKS_PER_STEP))


def _na_bias_kernel(rpb_ref, o_ref):
    rp = pltpu.roll(rpb_ref[0] * LOG2_E, LANES - (NA_KC - 1), axis=1)
    shape = (GRID_W, LANES)
    wk = lax.broadcasted_iota(jnp.int32, shape, 0)
    lane = lax.broadcasted_iota(jnp.int32, shape, 1)
    c_start = jnp.clip(lane - NA_KC // 2, 0, GRID_W - NA_KC)
    left = lane < GRID_W
    mask_l = jnp.where(left, NA_MASK, 0.0).astype(F32)
    mask_r = jnp.where(left, 0.0, NA_MASK).astype(F32)
    tiles_l, tiles_r = [], []
    for d in range(2 * NA_KR - 1):
        row = jnp.broadcast_to(rp[d:d + 1, :], shape)
        t = pltpu.roll(row, 0, axis=1, stride=1, stride_axis=0)
        t = jnp.where(wk >= c_start, jnp.where(wk < c_start + NA_KC, t, NA_MASK), NA_MASK)
        t = jnp.where(left, t, 0.0)
        tiles_l.append(t)
        tiles_r.append(pltpu.roll(t, GRID_W, axis=1))
    for n, blk in enumerate((0, 1, NA_BLOCKS - 1)):
        sb = min(max(blk * NA_ROWS_PER_BLOCK - NA_KR // 2, 0), GRID_H - NA_SLAB_ROWS)
        for j in range(NA_SLAB_ROWS):
            kr = sb + j

            def pick(i, tiles, masked):
                r = blk * NA_ROWS_PER_BLOCK + i
                r_start = min(max(r - NA_KR // 2, 0), GRID_H - NA_KR)
                if r_start <= kr < r_start + NA_KR:
                    return tiles[kr - r + NA_KR - 1]
                return masked

            for ii in range(NA_ROWS_PER_BLOCK // 2):
                o_ref[n, 0, j * GRID_W:(j + 1) * GRID_W, ii * LANES:(ii + 1) * LANES] = (
                    pick(2 * ii, tiles_l, mask_l) + pick(2 * ii + 1, tiles_r, mask_r)).astype(BF16)


def _na_bias_tables(rpb):
    rpb_p = jnp.pad(rpb[:, :, ::-1], ((0, 0), (0, 1), (0, LANES - (2 * NA_KC - 1))))
    return pl.pallas_call(
        _na_bias_kernel,
        grid=(HEADS,),
        in_specs=[pl.BlockSpec((1, 2 * NA_KR, LANES), lambda h: (h, 0, 0))],
        out_specs=pl.BlockSpec((3, 1, NA_SLAB, NA_TQ), lambda h: (0, h, 0, 0)),
        out_shape=jax.ShapeDtypeStruct((3, HEADS, NA_SLAB, NA_TQ), BF16),
        name="na_bias_table",
    )(rpb_p)


def _out_kernel(oa_ref, ob_ref, x_ref, shift_ref, scale_ref, gate_ref, ng_ref, wg_ref,
                woa_ref, wob_ref, wout_ref, fg_ref, o_ref):
    subs = [slice(i, i + OUT_SUB) for i in range(0, o_ref.shape[1], OUT_SUB)]

    def gates(sl):
        h = _rms(x_ref[0, sl], ng_ref[...]) * (1.0 + scale_ref[0, 0]) + shift_ref[0, 0]
        hb = h.astype(BF16)
        proj = lambda c: _dot_nt(hb, wg_ref[c[0]:c[1], :])
        za = proj(_G_ZA)
        zb = proj(_G_ZB)
        return ((za * jax.nn.sigmoid(za)).astype(BF16), (zb * jax.nn.sigmoid(zb)).astype(BF16),
                jax.nn.sigmoid(proj(_G_GA)).astype(BF16), jax.nn.sigmoid(proj(_G_GB)).astype(BF16))

    def branches(sl, g):
        sza, szb, sga, sgb = g
        ya = _dot(oa_ref[0, sl] * sza, woa_ref[...])
        yb = _dot(ob_ref[0, sl] * szb, wob_ref[...])
        return (sga.astype(F32) * ya + sgb.astype(F32) * yb).astype(BF16)

    def finish(sl, mix):
        y = _dot(mix, wout_ref[...])
        r = x_ref[0, sl] + gate_ref[0, 0] * y
        o_ref[0, sl] = _rms(r, fg_ref[...])

    order = sorted([(3 * i + off, stage, i) for i in range(len(subs))
                    for stage, off in (("gates", 0), ("branches", 4), ("finish", 8))])
    g, mix = {}, {}
    for _, stage, i in order:
        if stage == "gates":
            g[i] = gates(subs[i])
        elif stage == "branches":
            mix[i] = branches(subs[i], g.pop(i))
        else:
            finish(subs[i], mix.pop(i))


OUT_SUB = 256


def _out_call(oa, ob, x, mod, norm_g, w_gate, w_oa, w_ob, w_out, final_g, tm=1024):
    b, s, d = x.shape
    tok3 = lambda i, j: (i, j, 0)
    half = HEADS * MLA_V

    def const(shape):
        return pl.BlockSpec(shape, lambda i, j: (0, 0), pipeline_mode=pl.Buffered(1))

    return pl.pallas_call(
        _out_kernel,
        grid=(b, s // tm),
        in_specs=[
            pl.BlockSpec((1, tm, half), tok3),
            pl.BlockSpec((1, tm, half), tok3),
            pl.BlockSpec((1, tm, d), tok3),
            _mod_spec(MOD_SHIFT),
            _mod_spec(MOD_SCALE),
            _mod_spec(MOD_GATE),
            const((1, d)),
            const((w_gate.shape[0], d)),
            const((half, d)),
            const((half, d)),
            const((d, d)),
            const((1, d)),
        ],
        out_specs=pl.BlockSpec((1, tm, d), tok3),
        out_shape=jax.ShapeDtypeStruct((b, s, d), F32),
        compiler_params=pltpu.CompilerParams(vmem_limit_bytes=VMEM_LIMIT),
        name="gated_out",
    )(oa, ob, x, mod, mod, mod, norm_g, w_gate, w_oa, w_ob, w_out, final_g)


def _rope_partner(w, axis):
    q = MLA_ROPE // 4
    parts = [lax.slice_in_dim(w, n * q, (n + 1) * q, axis=axis) for n in range(4)]
    return jnp.concatenate([-parts[1], parts[0], -parts[3], parts[2]], axis=axis)


def _rope_tables():
    t = np.arange(SEQ)
    half = MLA_ROPE // 4
    freqs = (ROPE_THETA ** (-np.arange(half, dtype=np.float32) / half)).astype(np.float32)
    ang_r = (t // GRID_W).astype(np.float32)[:, None] * freqs[None, :]
    ang_c = (t % GRID_W).astype(np.float32)[:, None] * freqs[None, :]
    cos = np.concatenate([np.cos(ang_r)] * 2 + [np.cos(ang_c)] * 2, axis=-1).astype(np.float32)
    sin = np.concatenate([np.sin(ang_r)] * 2 + [np.sin(ang_c)] * 2, axis=-1).astype(np.float32)
    return cos, sin


def _np_lane_group(nope, rope):
    pad = np.zeros((rope.shape[0], LANES - MLA_NOPE - MLA_ROPE), np.float32)
    return np.concatenate([nope, rope, pad], axis=-1).astype(np.float32)


def _rope_rows(r):
    d = r.shape[1]
    return jnp.concatenate([jnp.zeros((MLA_NOPE, d), F32), r,
                            jnp.zeros((LANES - MLA_NOPE - MLA_ROPE, d), F32)], axis=0)


def kernel(x, c, ctx, c_ctx, w_mod, b_mod, norm_g, w_in, g_cq, w_uq, g_ckv, w_ukv, rpb,
           w_oa, w_ob, w_out, final_g):
    b, s, d = x.shape
    depth = w_mod.shape[0]
    assert depth == 1 and s == SEQ and d == D_MODEL
    mla_scale = (MLA_NOPE + MLA_ROPE) ** -0.5 * LOG2_E
    na_scale = NA_DIM ** -0.5 * LOG2_E

    wt = jnp.transpose(w_in[0])
    o = np.cumsum([0, KV_LORA, MLA_ROPE, HEADS * NA_DIM, HEADS * NA_DIM, Q_LORA, HEADS * MLA_V,
                   HEADS * NA_DIM, HEADS * NA_DIM, d, d])
    seg = [wt[o[n]:o[n + 1]] for n in range(10)]
    w_ckv, w_kr, w_kb, w_vb, w_cq, w_za, w_qb, w_zb, w_ga, w_gb = seg
    w_ext = jnp.concatenate([w_ckv, _rope_rows(w_kr), _rope_rows(_rope_partner(w_kr, 0)), w_kb, w_cq],
                            axis=0).astype(BF16)
    wt_ext = jnp.concatenate([w_vb, w_qb * na_scale], axis=0).astype(BF16)
    w_gate = jnp.concatenate([w_za, w_zb, w_ga, w_gb], axis=0).astype(BF16)

    wkv3 = w_ukv[0].reshape(KV_LORA, HEADS, MLA_NOPE + MLA_V)
    wk = jnp.concatenate([wkv3[:, :, :MLA_NOPE], jnp.zeros((KV_LORA, HEADS, LANES - MLA_NOPE), F32)],
                         axis=-1).reshape(KV_LORA, HEADS * LANES).astype(BF16)
    wv = wkv3[:, :, MLA_NOPE:].reshape(KV_LORA, HEADS * MLA_V).astype(BF16).T
    wq3 = w_uq[0].reshape(Q_LORA, HEADS, MLA_NOPE + MLA_ROPE)
    zq = jnp.zeros((Q_LORA, HEADS, LANES - MLA_NOPE - MLA_ROPE), F32)
    wq = jnp.concatenate([wq3, zq], axis=-1).reshape(Q_LORA, HEADS * LANES).astype(BF16).T
    wqr = jnp.concatenate([jnp.zeros((Q_LORA, HEADS, MLA_NOPE), F32),
                           _rope_partner(wq3[:, :, MLA_NOPE:], 2), zq],
                          axis=-1).reshape(Q_LORA, HEADS * LANES).astype(BF16).T

    cos, sin = _rope_tables()
    ones_s = np.ones((s, MLA_NOPE), np.float32)
    zeros_s = np.zeros((s, MLA_NOPE), np.float32)
    cq_tab = jnp.asarray(np.ascontiguousarray((_np_lane_group(ones_s, cos) * np.float32(mla_scale)).T))
    sq_tab = jnp.asarray(np.ascontiguousarray((_np_lane_group(zeros_s, sin) * np.float32(mla_scale)).T))
    ck_tab = jnp.asarray(_np_lane_group(zeros_s, cos))
    sk_tab = jnp.asarray(_np_lane_group(zeros_s, sin))
    zeros_c = np.zeros((CTX_LEN, MLA_NOPE), np.float32)
    ck_ctx = jnp.asarray(_np_lane_group(zeros_c, np.ones((CTX_LEN, MLA_ROPE), np.float32)))
    sk_ctx = jnp.zeros((CTX_LEN, LANES), F32)

    cs = jnp.concatenate([c, c_ctx[None, :], jnp.zeros((8 - b - 1, d), F32)], axis=0)
    mod = _mod_call(cs, w_mod[0], b_mod[0][None, :])

    ng = norm_g[0][None, :]
    gkv = g_ckv[0][None, :]
    k, v, kb, vb, q, qb = _proj_call(
        x, mod, None, ng, w_ext, wt_ext, gkv, wk, wv, ck_tab, sk_tab,
        q_side=(g_cq[0][None, :], wq, wqr, cq_tab, sq_tab))
    kc, vc, kcb, vcb = _proj_call(
        ctx, mod, b, ng, w_ext[:KV_EXT], wt_ext[:_R_VB[1]], gkv, wk, wv,
        ck_ctx, sk_ctx)

    oa = _mla_call(q, k, kc, v, vc)
    ob = _na_call(qb, kb, vb, kcb, vcb, _na_bias_tables(rpb[0]))

    return _out_call(oa, ob, x, mod, ng, w_gate,
                     w_oa[0].astype(BF16), w_ob[0].astype(BF16), w_out[0].astype(BF16),
                     final_g[None, :])
```

```python
import functools

import jax
import jax.numpy as jnp
import numpy as np
from jax import lax
from jax.experimental import pallas as pl
from jax.experimental.pallas import tpu as pltpu

D_MODEL = 1024
SEQ = 4096
GRID_W = 64
GRID_H = SEQ // GRID_W
CTX_LEN = 256
HEADS = 8
MLA_NOPE = 64
MLA_ROPE = 32
MLA_V = 64
Q_LORA = 256
KV_LORA = 128
NA_DIM = 64
NA_KR = 8
NA_KC = 16
ROPE_THETA = 10000.0
EPS = 1e-6

LANES = 128
HEAD_PAIRS = HEADS // 2
VMEM_LIMIT = 56 * 1024 * 1024

_C_CKV = (0, 128)
_C_KR = (128, 256)
_C_KRR = (256, 384)
_C_KB = (384, 896)
KV_EXT = 896
_C_CQ = (896, 1152)
_G_ZA = (0, 512)
_G_ZB = (512, 1024)
_G_GA = (1024, 2048)
_G_GB = (2048, 3072)
_R_VB = (0, 512)
_R_QB = (512, 1024)

NA_ROWS_PER_BLOCK = 4
NA_BLOCKS = GRID_H // NA_ROWS_PER_BLOCK
NA_TQ = NA_ROWS_PER_BLOCK * GRID_W
NA_SLAB_ROWS = 12
NA_SLAB = NA_SLAB_ROWS * GRID_W
NA_MASK = -1e30
LOG2_E = 1.4426950408889634

F32 = jnp.float32
BF16 = jnp.bfloat16


def _dot(a, b):
    return jnp.dot(a, b, preferred_element_type=F32)


def _dot_nt(a, b):
    return lax.dot_general(a, b, (((1,), (1,)), ((), ())), preferred_element_type=F32)


def _rms(x, g):
    return x * lax.rsqrt(jnp.mean(x * x, axis=-1, keepdims=True) + EPS) * g


def _mod_kernel(c_ref, w_ref, b_ref, o_ref):
    c = c_ref[...]
    sc = c * jax.nn.sigmoid(c)
    o_ref[0, :, 0, :] = _dot(sc.astype(BF16), w_ref[...].astype(BF16)) + b_ref[...]


def _mod_call(cs, w_mod, b_mod):
    n = w_mod.shape[1]
    tn = 1024
    return pl.pallas_call(
        _mod_kernel,
        grid=(n // tn,),
        in_specs=[
            pl.BlockSpec((8, D_MODEL), lambda j: (0, 0)),
            pl.BlockSpec((D_MODEL, tn), lambda j: (0, j)),
            pl.BlockSpec((1, tn), lambda j: (0, j)),
        ],
        out_specs=pl.BlockSpec((1, 8, 1, tn), lambda j: (j, 0, 0, 0)),
        out_shape=jax.ShapeDtypeStruct((n // tn, 8, 1, tn), F32),
        name="adaln_mod",
    )(cs, w_mod, b_mod)


def _proj_kernel(with_q, x_ref, shift_ref, scale_ref, ng_ref, w_ref, wt_ref, gkv_ref, wk_ref, wv_ref,
                 ck_ref, sk_ref, *rest):
    if with_q:
        (gq_ref, wq_ref, wqr_ref, cq_ref, sq_ref,
         k_ref, v_ref, kb_ref, vb_ref, q_ref, qb_ref) = rest
    else:
        k_ref, v_ref, kb_ref, vb_ref = rest

    x = x_ref[0]
    h = _rms(x, ng_ref[...]) * (1.0 + scale_ref[0, 0]) + shift_ref[0, 0]
    hb = h.astype(BF16)

    def proj(c):
        return _dot_nt(hb, w_ref[c[0]:c[1], :])

    def proj_t(r):
        return _dot_nt(wt_ref[r[0]:r[1], :], hb)

    ckvn = _rms(proj(_C_CKV), gkv_ref[...]).astype(BF16)
    if with_q:
        cqn = _rms(proj(_C_CQ), gq_ref[...]).astype(BF16)
    krope = proj(_C_KR) * ck_ref[...] + proj(_C_KRR) * sk_ref[...]
    kb_ref[0] = proj(_C_KB).astype(BF16)
    vb_ref[0] = proj_t(_R_VB).astype(BF16)
    if with_q:
        qb_ref[0] = proj_t(_R_QB).astype(BF16)

    knope = _dot(ckvn, wk_ref[...])
    for hd in range(HEADS):
        sl = slice(hd * LANES, (hd + 1) * LANES)
        k_ref[0, :, sl] = (knope[:, sl] + krope).astype(BF16)
    v_ref[0] = _dot_nt(wv_ref[...], ckvn).astype(BF16)
    if with_q:
        q1 = _dot_nt(wq_ref[...], cqn)
        q2 = _dot_nt(wqr_ref[...], cqn)
        cq = cq_ref[...]
        sq = sq_ref[...]
        for hd in range(HEADS):
            sl = slice(hd * LANES, (hd + 1) * LANES)
            q_ref[0, sl, :] = (q1[sl] * cq + q2[sl] * sq).astype(BF16)


def _mod_spec(which, row=None):
    d = D_MODEL
    if row is None:
        return pl.BlockSpec((1, 1, 1, d), lambda i, j: (which, i, 0, 0))
    return pl.BlockSpec((1, 1, 1, d), lambda i, j: (which, row, 0, 0))


MOD_SHIFT, MOD_SCALE, MOD_GATE = 0, 1, 2


def _proj_call(x, mod, mod_row, norm_g, w_ext, wt_ext, g_ckv, wk, wv, ck, sk,
               q_side=None, tm=512):
    b, s, d = x.shape
    tm = min(tm, s)
    with_q = q_side is not None
    n_ext = w_ext.shape[0]
    n_t = wt_ext.shape[0]
    const2 = lambda i, j: (0, 0)
    tok3 = lambda i, j: (i, j, 0)
    tab = lambda i, j: (j, 0)

    in_specs = [
        pl.BlockSpec((1, tm, d), tok3),
        _mod_spec(MOD_SHIFT, mod_row),
        _mod_spec(MOD_SCALE, mod_row),
        pl.BlockSpec((1, d), const2),
        pl.BlockSpec((n_ext, d), const2),
        pl.BlockSpec((n_t, d), const2),
        pl.BlockSpec((1, KV_LORA), const2),
        pl.BlockSpec((KV_LORA, HEADS * LANES), const2),
        pl.BlockSpec((HEADS * MLA_V, KV_LORA), const2),
        pl.BlockSpec((tm, LANES), tab),
        pl.BlockSpec((tm, LANES), tab),
    ]
    args = [x, mod, mod, norm_g, w_ext, wt_ext, g_ckv, wk, wv, ck, sk]
    tok_major = lambda w: (pl.BlockSpec((1, tm, w), tok3), jax.ShapeDtypeStruct((b, s, w), BF16))
    feat_major = lambda w: (pl.BlockSpec((1, w, tm), lambda i, j: (i, 0, j)),
                            jax.ShapeDtypeStruct((b, w, s), BF16))
    outs = [tok_major(HEADS * LANES), feat_major(HEADS * MLA_V), tok_major(HEADS * NA_DIM),
            feat_major(HEADS * NA_DIM)]
    if with_q:
        g_cq, wq, wqr, cq, sq = q_side
        in_specs += [
            pl.BlockSpec((1, Q_LORA), const2),
            pl.BlockSpec((HEADS * LANES, Q_LORA), const2),
            pl.BlockSpec((HEADS * LANES, Q_LORA), const2),
            pl.BlockSpec((LANES, tm), lambda i, j: (0, j)),
            pl.BlockSpec((LANES, tm), lambda i, j: (0, j)),
        ]
        args += [g_cq, wq, wqr, cq, sq]
        outs += [feat_major(HEADS * LANES), feat_major(HEADS * NA_DIM)]
    out_specs = [o[0] for o in outs]
    out_shape = [o[1] for o in outs]
    return pl.pallas_call(
        functools.partial(_proj_kernel, with_q),
        grid=(b, s // tm),
        in_specs=in_specs,
        out_specs=out_specs,
        out_shape=out_shape,
        compiler_params=pltpu.CompilerParams(vmem_limit_bytes=VMEM_LIMIT),
        name="in_proj_q" if with_q else "in_proj_ctx",
    )(*args)


BF16_SUBLANES = 16
MLA_TQ = 256
MLA_TILES = 4
MLA_CHUNK = 256
MLA_QK_CHUNKS = 2
MLA_PV_DELAY = 1
MLA_LOOKAHEAD = 6


def _mla_kernel(qt_ref, k_ref, kc_ref, vt_ref, vct_ref, o_ref):
    s_len = k_ref.shape[1]
    streams = [(t, hh) for t in range(MLA_TILES) for hh in range(2)]
    qts = {(t, hh): qt_ref[0, hh * LANES:(hh + 1) * LANES, t * MLA_TQ:(t + 1) * MLA_TQ] for t, hh in streams}
    qk_keys = MLA_QK_CHUNKS * MLA_CHUNK
    groups = []
    for t in range(MLA_TILES):
        groups += [((t, hh), k_ref, vt_ref, c * qk_keys, qk_keys)
                   for c in range(s_len // qk_keys) for hh in range(2)]
        groups += [((t, hh), kc_ref, vct_ref, 0, kc_ref.shape[1]) for hh in range(2)]
    items = [(st, vr, off + i, min(MLA_CHUNK, n - i))
             for st, _, vr, off, n in groups for i in range(0, n, MLA_CHUNK)]

    def group_scores(g):
        st, kr, _, off, n = g
        hh = st[1]
        big = _dot(kr[0, off:off + n, hh * LANES:(hh + 1) * LANES], qts[st]).astype(BF16)
        return [big[i:i + MLA_CHUNK] for i in range(0, n, MLA_CHUNK)]

    m = {st: None for st in streams}
    acc = {st: None for st in streams}
    pending = []
    gi = 0
    last_item = {st[0]: idx for idx, (st, _, _, _) in enumerate(items)}

    def weigh(idx, st, vr, off, n, p, alpha):
        hh = st[1]
        v_ones = jnp.concatenate([vr[0, hh * MLA_V:(hh + 1) * MLA_V, off:off + n],
                                  jnp.ones((BF16_SUBLANES, n), BF16)], axis=0)
        pv = _dot(v_ones, p)
        acc[st] = pv if alpha is None else acc[st] * alpha + pv
        t = st[0]
        if idx == last_item[t]:
            outs = [acc[(t, h2)][:MLA_V] / acc[(t, h2)][MLA_V:MLA_V + 1] for h2 in range(2)]
            o_ref[0, t * MLA_TQ:(t + 1) * MLA_TQ, :] = jnp.concatenate(outs, axis=0).T.astype(BF16)

    delayed = []
    for idx, (st, vr, off, n) in enumerate(items):
        while len(pending) <= MLA_LOOKAHEAD and gi < len(groups):
            pending.extend(group_scores(groups[gi]))
            gi += 1
        s = pending.pop(0)
        mc = jnp.max(s, axis=0, keepdims=True)
        m_new = mc if m[st] is None else jnp.maximum(m[st], mc)
        p = jnp.exp2(s - m_new)
        alpha = None if m[st] is None else jnp.exp2(m[st].astype(F32) - m_new.astype(F32))
        m[st] = m_new
        if len(delayed) == MLA_PV_DELAY:
            weigh(*delayed.pop(0))
        delayed.append((idx, st, vr, off, n, p, alpha))
    for d in delayed:
        weigh(*d)


def _mla_call(qt, k, kc, vt, vct):
    b, _, s = qt.shape
    t = kc.shape[1]
    step_q = MLA_TILES * MLA_TQ
    return pl.pallas_call(
        _mla_kernel,
        grid=(b, HEAD_PAIRS, s // step_q),
        in_specs=[
            pl.BlockSpec((1, 2 * LANES, step_q), lambda i, p, j: (i, p, j)),
            pl.BlockSpec((1, s, 2 * LANES), lambda i, p, j: (i, 0, p)),
            pl.BlockSpec((1, t, 2 * LANES), lambda i, p, j: (i, 0, p)),
            pl.BlockSpec((1, 2 * MLA_V, s), lambda i, p, j: (i, p, 0)),
            pl.BlockSpec((1, 2 * MLA_V, t), lambda i, p, j: (i, p, 0)),
        ],
        out_specs=pl.BlockSpec((1, step_q, LANES), lambda i, p, j: (i, j, p)),
        out_shape=jax.ShapeDtypeStruct((b, s, HEADS * MLA_V), BF16),
        compiler_params=pltpu.CompilerParams(vmem_limit_bytes=VMEM_LIMIT),
        name="mla_attention",
    )(qt, k, kc, vt, vct)


NA_HEADS_PER_STEP = 8
NA_BLOCKS_PER_STEP = 4
NA_LOOKAHEAD = 3


def _na_slab_start(blk):
    return jnp.clip(blk * NA_ROWS_PER_BLOCK - NA_KR // 2, 0, GRID_H - NA_SLAB_ROWS)


def _na_variant(blk):
    return jnp.where(blk == 0, 0, jnp.where(blk == NA_BLOCKS - 1, 2, 1))


def _with_ones(v):
    return jnp.concatenate([v, jnp.ones((BF16_SUBLANES, v.shape[1]), v.dtype)], axis=0)


def _na_kernel(qt_ref, k_ref, vt_ref, kc_ref, vct_ref, *rest):
    bias_refs, o_ref = rest[:-1], rest[-1]
    row_half = lax.broadcasted_iota(jnp.int32, (LANES, NA_TQ), 0) // NA_DIM
    starts = [pl.multiple_of(_na_slab_start(pl.program_id(2) * NA_BLOCKS_PER_STEP + n) * GRID_W, NA_TQ)
              for n in range(NA_BLOCKS_PER_STEP)]
    items = [(n, h) for n in range(NA_BLOCKS_PER_STEP) for h in range(NA_HEADS_PER_STEP)]

    def scores(item):
        n, h = item
        pair, hh = divmod(h, 2)
        cols = slice(pair * LANES, (pair + 1) * LANES)
        k_all = jnp.concatenate([k_ref[0, pl.ds(starts[n], NA_SLAB), cols], kc_ref[0, :, cols]], axis=0)
        qt = qt_ref[0, cols, n * NA_TQ:(n + 1) * NA_TQ]
        qm = jnp.where(row_half == hh, qt, jnp.zeros_like(qt))
        s = _dot(k_all, qm).astype(BF16)
        return jnp.concatenate([s[:NA_SLAB] + bias_refs[n][0, h], s[NA_SLAB:]], axis=0)

    outs = []
    ahead = [scores(it) for it in items[:NA_LOOKAHEAD]]
    for idx, (n, h) in enumerate(items):
        if idx + NA_LOOKAHEAD < len(items):
            ahead.append(scores(items[idx + NA_LOOKAHEAD]))
        s = ahead.pop(0)
        p = jnp.exp2(s - jnp.max(s, axis=0, keepdims=True))
        rows = slice(h * NA_DIM, (h + 1) * NA_DIM)
        v_all = jnp.concatenate([vt_ref[0, rows, pl.ds(starts[n], NA_SLAB)], vct_ref[0, rows, :]], axis=1)
        pv = _dot(_with_ones(v_all), p)
        outs.append(pv[:NA_DIM] / pv[NA_DIM:NA_DIM + 1])
        if h == NA_HEADS_PER_STEP - 1:
            o_ref[0, n * NA_TQ:(n + 1) * NA_TQ, :] = jnp.concatenate(outs, axis=0).T.astype(BF16)
            outs = []


def _na_call(qbt, kb, vbt, kcb, vcbt, bias):
    b, s, _ = kb.shape
    t = kcb.shape[1]
    wide = NA_HEADS_PER_STEP * NA_DIM
    step_q = NA_BLOCKS_PER_STEP * NA_TQ

    def bias_spec(n):
        return pl.BlockSpec((1, NA_HEADS_PER_STEP, NA_SLAB, NA_TQ),
                            lambda g, i, j: (_na_variant(j * NA_BLOCKS_PER_STEP + n), g, 0, 0))

    return pl.pallas_call(
        _na_kernel,
        grid=(HEADS // NA_HEADS_PER_STEP, b, NA_BLOCKS // NA_BLOCKS_PER_STEP),
        in_specs=[
            pl.BlockSpec((1, wide, step_q), lambda g, i, j: (i, g, j)),
            pl.BlockSpec((1, s, wide), lambda g, i, j: (i, 0, g)),
            pl.BlockSpec((1, wide, s), lambda g, i, j: (i, g, 0)),
            pl.BlockSpec((1, t, wide), lambda g, i, j: (i, 0, g)),
            pl.BlockSpec((1, wide, t), lambda g, i, j: (i, g, 0)),
        ] + [bias_spec(n) for n in range(NA_BLOCKS_PER_STEP)],
        out_specs=pl.BlockSpec((1, step_q, wide), lambda g, i, j: (i, j, g)),
        out_shape=jax.ShapeDtypeStruct((b, s, HEADS * NA_DIM), BF16),
        compiler_params=pltpu.CompilerParams(vmem_limit_bytes=VMEM_LIMIT),
        name="na_attention",
    )(qbt, kb, vbt, kcb, vcbt, *([bias] * NA_BLOCKS_PER_STEP))


def _na_bias_kernel(rpb_ref, o_ref):
    rp = pltpu.roll(rpb_ref[0] * LOG2_E, LANES - (NA_KC - 1), axis=1)
    shape = (GRID_W, LANES)
    wk = lax.broadcasted_iota(jnp.int32, shape, 0)
    lane = lax.broadcasted_iota(jnp.int32, shape, 1)
    c_start = jnp.clip(lane - NA_KC // 2, 0, GRID_W - NA_KC)
    left = lane < GRID_W
    mask_l = jnp.where(left, NA_MASK, 0.0).astype(F32)
    mask_r = jnp.where(left, 0.0, NA_MASK).astype(F32)
    tiles_l, tiles_r = [], []
    for d in range(2 * NA_KR - 1):
        row = jnp.broadcast_to(rp[d:d + 1, :], shape)
        t = pltpu.roll(row, 0, axis=1, stride=1, stride_axis=0)
        t = jnp.where(wk >= c_start, jnp.where(wk < c_start + NA_KC, t, NA_MASK), NA_MASK)
        t = jnp.where(left, t, 0.0)
        tiles_l.append(t)
        tiles_r.append(pltpu.roll(t, GRID_W, axis=1))
    for n, blk in enumerate((0, 1, NA_BLOCKS - 1)):
        sb = min(max(blk * NA_ROWS_PER_BLOCK - NA_KR // 2, 0), GRID_H - NA_SLAB_ROWS)
        for j in range(NA_SLAB_ROWS):
            kr = sb + j

            def pick(i, tiles, masked):
                r = blk * NA_ROWS_PER_BLOCK + i
                r_start = min(max(r - NA_KR // 2, 0), GRID_H - NA_KR)
                if r_start <= kr < r_start + NA_KR:
                    return tiles[kr - r + NA_KR - 1]
                return masked

            for ii in range(NA_ROWS_PER_BLOCK // 2):
                o_ref[n, 0, j * GRID_W:(j + 1) * GRID_W, ii * LANES:(ii + 1) * LANES] = (
                    pick(2 * ii, tiles_l, mask_l) + pick(2 * ii + 1, tiles_r, mask_r)).astype(BF16)


def _na_bias_tables(rpb):
    rpb_p = jnp.pad(rpb[:, :, ::-1], ((0, 0), (0, 1), (0, LANES - (2 * NA_KC - 1))))
    return pl.pallas_call(
        _na_bias_kernel,
        grid=(HEADS,),
        in_specs=[pl.BlockSpec((1, 2 * NA_KR, LANES), lambda h: (h, 0, 0))],
        out_specs=pl.BlockSpec((3, 1, NA_SLAB, NA_TQ), lambda h: (0, h, 0, 0)),
        out_shape=jax.ShapeDtypeStruct((3, HEADS, NA_SLAB, NA_TQ), BF16),
        name="na_bias_table",
    )(rpb_p)


def _out_kernel(oa_ref, ob_ref, x_ref, shift_ref, scale_ref, gate_ref, ng_ref, wg_ref,
                woa_ref, wob_ref, wout_ref, fg_ref, o_ref):
    subs = [slice(i, i + OUT_SUB) for i in range(0, o_ref.shape[1], OUT_SUB)]

    def gates(sl):
        h = _rms(x_ref[0, sl], ng_ref[...]) * (1.0 + scale_ref[0, 0]) + shift_ref[0, 0]
        hb = h.astype(BF16)
        proj = lambda c: _dot_nt(hb, wg_ref[c[0]:c[1], :])
        za = proj(_G_ZA)
        zb = proj(_G_ZB)
        return ((za * jax.nn.sigmoid(za)).astype(BF16), (zb * jax.nn.sigmoid(zb)).astype(BF16),
                jax.nn.sigmoid(proj(_G_GA)).astype(BF16), jax.nn.sigmoid(proj(_G_GB)).astype(BF16))

    def branches(sl, g):
        sza, szb, sga, sgb = g
        ya = _dot(oa_ref[0, sl] * sza, woa_ref[...])
        yb = _dot(ob_ref[0, sl] * szb, wob_ref[...])
        return (sga.astype(F32) * ya + sgb.astype(F32) * yb).astype(BF16)

    def finish(sl, mix):
        y = _dot(mix, wout_ref[...])
        r = x_ref[0, sl] + gate_ref[0, 0] * y
        o_ref[0, sl] = _rms(r, fg_ref[...])

    order = sorted([(3 * i + off, stage, i) for i in range(len(subs))
                    for stage, off in (("gates", 0), ("branches", 4), ("finish", 8))])
    g, mix = {}, {}
    for _, stage, i in order:
        if stage == "gates":
            g[i] = gates(subs[i])
        elif stage == "branches":
            mix[i] = branches(subs[i], g.pop(i))
        else:
            finish(subs[i], mix.pop(i))


OUT_SUB = 256


def _out_call(oa, ob, x, mod, norm_g, w_gate, w_oa, w_ob, w_out, final_g, tm=1024):
    b, s, d = x.shape
    tok3 = lambda i, j: (i, j, 0)
    half = HEADS * MLA_V

    def const(shape):
        return pl.BlockSpec(shape, lambda i, j: (0, 0), pipeline_mode=pl.Buffered(1))

    return pl.pallas_call(
        _out_kernel,
        grid=(b, s // tm),
        in_specs=[
            pl.BlockSpec((1, tm, half), tok3),
            pl.BlockSpec((1, tm, half), tok3),
            pl.BlockSpec((1, tm, d), tok3),
            _mod_spec(MOD_SHIFT),
            _mod_spec(MOD_SCALE),
            _mod_spec(MOD_GATE),
            const((1, d)),
            const((w_gate.shape[0], d)),
            const((half, d)),
            const((half, d)),
            const((d, d)),
            const((1, d)),
        ],
        out_specs=pl.BlockSpec((1, tm, d), tok3),
        out_shape=jax.ShapeDtypeStruct((b, s, d), F32),
        compiler_params=pltpu.CompilerParams(vmem_limit_bytes=VMEM_LIMIT),
        name="gated_out",
    )(oa, ob, x, mod, mod, mod, norm_g, w_gate, w_oa, w_ob, w_out, final_g)


def _rope_partner(w, axis):
    q = MLA_ROPE // 4
    parts = [lax.slice_in_dim(w, n * q, (n + 1) * q, axis=axis) for n in range(4)]
    return jnp.concatenate([-parts[1], parts[0], -parts[3], parts[2]], axis=axis)


def _rope_tables():
    t = np.arange(SEQ)
    half = MLA_ROPE // 4
    freqs = (ROPE_THETA ** (-np.arange(half, dtype=np.float32) / half)).astype(np.float32)
    ang_r = (t // GRID_W).astype(np.float32)[:, None] * freqs[None, :]
    ang_c = (t % GRID_W).astype(np.float32)[:, None] * freqs[None, :]
    cos = np.concatenate([np.cos(ang_r)] * 2 + [np.cos(ang_c)] * 2, axis=-1).astype(np.float32)
    sin = np.concatenate([np.sin(ang_r)] * 2 + [np.sin(ang_c)] * 2, axis=-1).astype(np.float32)
    return cos, sin


def _np_lane_group(nope, rope):
    pad = np.zeros((rope.shape[0], LANES - MLA_NOPE - MLA_ROPE), np.float32)
    return np.concatenate([nope, rope, pad], axis=-1).astype(np.float32)


def _rope_rows(r):
    d = r.shape[1]
    return jnp.concatenate([jnp.zeros((MLA_NOPE, d), F32), r,
                            jnp.zeros((LANES - MLA_NOPE - MLA_ROPE, d), F32)], axis=0)


def kernel(x, c, ctx, c_ctx, w_mod, b_mod, norm_g, w_in, g_cq, w_uq, g_ckv, w_ukv, rpb,
           w_oa, w_ob, w_out, final_g):
    b, s, d = x.shape
    depth = w_mod.shape[0]
    assert depth == 1 and s == SEQ and d == D_MODEL
    mla_scale = (MLA_NOPE + MLA_ROPE) ** -0.5 * LOG2_E
    na_scale = NA_DIM ** -0.5 * LOG2_E

    wt = jnp.transpose(w_in[0])
    o = np.cumsum([0, KV_LORA, MLA_ROPE, HEADS * NA_DIM, HEADS * NA_DIM, Q_LORA, HEADS * MLA_V,
                   HEADS * NA_DIM, HEADS * NA_DIM, d, d])
    seg = [wt[o[n]:o[n + 1]] for n in range(10)]
    w_ckv, w_kr, w_kb, w_vb, w_cq, w_za, w_qb, w_zb, w_ga, w_gb = seg
    w_ext = jnp.concatenate([w_ckv, _rope_rows(w_kr), _rope_rows(_rope_partner(w_kr, 0)), w_kb, w_cq],
                            axis=0).astype(BF16)
    wt_ext = jnp.concatenate([w_vb, w_qb * na_scale], axis=0).astype(BF16)
    w_gate = jnp.concatenate([w_za, w_zb, w_ga, w_gb], axis=0).astype(BF16)

    wkv3 = w_ukv[0].reshape(KV_LORA, HEADS, MLA_NOPE + MLA_V)
    wk = jnp.concatenate([wkv3[:, :, :MLA_NOPE], jnp.zeros((KV_LORA, HEADS, LANES - MLA_NOPE), F32)],
                         axis=-1).reshape(KV_LORA, HEADS * LANES).astype(BF16)
    wv = wkv3[:, :, MLA_NOPE:].reshape(KV_LORA, HEADS * MLA_V).astype(BF16).T
    wq3 = w_uq[0].reshape(Q_LORA, HEADS, MLA_NOPE + MLA_ROPE)
    zq = jnp.zeros((Q_LORA, HEADS, LANES - MLA_NOPE - MLA_ROPE), F32)
    wq = jnp.concatenate([wq3, zq], axis=-1).reshape(Q_LORA, HEADS * LANES).astype(BF16).T
    wqr = jnp.concatenate([jnp.zeros((Q_LORA, HEADS, MLA_NOPE), F32),
                           _rope_partner(wq3[:, :, MLA_NOPE:], 2), zq],
                          axis=-1).reshape(Q_LORA, HEADS * LANES).astype(BF16).T

    cos, sin = _rope_tables()
    ones_s = np.ones((s, MLA_NOPE), np.float32)
    zeros_s = np.zeros((s, MLA_NOPE), np.float32)
    cq_tab = jnp.asarray(np.ascontiguousarray((_np_lane_group(ones_s, cos) * np.float32(mla_scale)).T))
    sq_tab = jnp.asarray(np.ascontiguousarray((_np_lane_group(zeros_s, sin) * np.float32(mla_scale)).T))
    ck_tab = jnp.asarray(_np_lane_group(zeros_s, cos))
    sk_tab = jnp.asarray(_np_lane_group(zeros_s, sin))
    zeros_c = np.zeros((CTX_LEN, MLA_NOPE), np.float32)
    ck_ctx = jnp.asarray(_np_lane_group(zeros_c, np.ones((CTX_LEN, MLA_ROPE), np.float32)))
    sk_ctx = jnp.zeros((CTX_LEN, LANES), F32)

    cs = jnp.concatenate([c, c_ctx[None, :], jnp.zeros((8 - b - 1, d), F32)], axis=0)
    mod = _mod_call(cs, w_mod[0], b_mod[0][None, :])

    ng = norm_g[0][None, :]
    gkv = g_ckv[0][None, :]
    k, v, kb, vb, q, qb = _proj_call(
        x, mod, None, ng, w_ext, wt_ext, gkv, wk, wv, ck_tab, sk_tab,
        q_side=(g_cq[0][None, :], wq, wqr, cq_tab, sq_tab))
    kc, vc, kcb, vcb = _proj_call(
        ctx, mod, b, ng, w_ext[:KV_EXT], wt_ext[:_R_VB[1]], gkv, wk, wv,
        ck_ctx, sk_ctx)

    oa = _mla_call(q, k, kc, v, vc)
    ob = _na_call(qb, kb, vb, kcb, vcb, _na_bias_tables(rpb[0]))

    return _out_call(oa, ob, x, mod, ng, w_gate,
                     w_oa[0].astype(BF16), w_ob[0].astype(BF16), w_out[0].astype(BF16),
                     final_g[None, :])
```

```python
import functools

import jax
import jax.numpy as jnp
import numpy as np
from jax import lax
from jax.experimental import pallas as pl
from jax.experimental.pallas import tpu as pltpu

D_MODEL = 1024
SEQ = 4096
GRID_W = 64
GRID_H = SEQ // GRID_W
CTX_LEN = 256
HEADS = 8
MLA_NOPE = 64
MLA_ROPE = 32
MLA_V = 64
Q_LORA = 256
KV_LORA = 128
NA_DIM = 64
NA_KR = 8
NA_KC = 16
ROPE_THETA = 10000.0
EPS = 1e-6

LANES = 128
HEAD_PAIRS = HEADS // 2
VMEM_LIMIT = 56 * 1024 * 1024

_C_CKV = (0, 128)
_C_KR = (128, 256)
_C_KRR = (256, 384)
_C_KB = (384, 896)
KV_EXT = 896
_C_CQ = (896, 1152)
_G_ZA = (0, 512)
_G_ZB = (512, 1024)
_G_GA = (1024, 2048)
_G_GB = (2048, 3072)
_R_VB = (0, 512)
_R_QB = (512, 1024)

NA_ROWS_PER_BLOCK = 4
NA_BLOCKS = GRID_H // NA_ROWS_PER_BLOCK
NA_TQ = NA_ROWS_PER_BLOCK * GRID_W
NA_SLAB_ROWS = 12
NA_SLAB = NA_SLAB_ROWS * GRID_W
NA_MASK = -1e30
LOG2_E = 1.4426950408889634

F32 = jnp.float32
BF16 = jnp.bfloat16


def _dot(a, b):
    return jnp.dot(a, b, preferred_element_type=F32)


def _dot_nt(a, b):
    return lax.dot_general(a, b, (((1,), (1,)), ((), ())), preferred_element_type=F32)


def _rms(x, g):
    return x * lax.rsqrt(jnp.mean(x * x, axis=-1, keepdims=True) + EPS) * g


def _mod_kernel(c_ref, w_ref, b_ref, o_ref):
    c = c_ref[...]
    sc = c * jax.nn.sigmoid(c)
    o_ref[0, :, 0, :] = _dot(sc.astype(BF16), w_ref[...].astype(BF16)) + b_ref[...]


def _mod_call(cs, w_mod, b_mod):
    n = w_mod.shape[1]
    tn = 1024
    return pl.pallas_call(
        _mod_kernel,
        grid=(n // tn,),
        in_specs=[
            pl.BlockSpec((8, D_MODEL), lambda j: (0, 0)),
            pl.BlockSpec((D_MODEL, tn), lambda j: (0, j)),
            pl.BlockSpec((1, tn), lambda j: (0, j)),
        ],
        out_specs=pl.BlockSpec((1, 8, 1, tn), lambda j: (j, 0, 0, 0)),
        out_shape=jax.ShapeDtypeStruct((n // tn, 8, 1, tn), F32),
        name="adaln_mod",
    )(cs, w_mod, b_mod)


def _proj_kernel(with_q, x_ref, shift_ref, scale_ref, ng_ref, w_ref, wt_ref, gkv_ref, wk_ref, wv_ref,
                 ck_ref, sk_ref, *rest):
    if with_q:
        (gq_ref, wq_ref, wqr_ref, cq_ref, sq_ref,
         k_ref, v_ref, kb_ref, vb_ref, q_ref, qb_ref) = rest
    else:
        k_ref, v_ref, kb_ref, vb_ref = rest

    x = x_ref[0]
    h = _rms(x, ng_ref[...]) * (1.0 + scale_ref[0, 0]) + shift_ref[0, 0]
    hb = h.astype(BF16)

    def proj(c):
        return _dot_nt(hb, w_ref[c[0]:c[1], :])

    def proj_t(r):
        return _dot_nt(wt_ref[r[0]:r[1], :], hb)

    ckvn = _rms(proj(_C_CKV), gkv_ref[...]).astype(BF16)
    if with_q:
        cqn = _rms(proj(_C_CQ), gq_ref[...]).astype(BF16)
    krope = proj(_C_KR) * ck_ref[...] + proj(_C_KRR) * sk_ref[...]
    kb_ref[0] = proj(_C_KB).astype(BF16)
    vb_ref[0] = proj_t(_R_VB).astype(BF16)
    if with_q:
        qb_ref[0] = proj_t(_R_QB).astype(BF16)

    knope = _dot(ckvn, wk_ref[...])
    for hd in range(HEADS):
        sl = slice(hd * LANES, (hd + 1) * LANES)
        k_ref[0, :, sl] = (knope[:, sl] + krope).astype(BF16)
    v_ref[0] = _dot_nt(wv_ref[...], ckvn).astype(BF16)
    if with_q:
        q1 = _dot_nt(wq_ref[...], cqn)
        q2 = _dot_nt(wqr_ref[...], cqn)
        cq = cq_ref[...]
        sq = sq_ref[...]
        for hd in range(HEADS):
            sl = slice(hd * LANES, (hd + 1) * LANES)
            q_ref[0, sl, :] = (q1[sl] * cq + q2[sl] * sq).astype(BF16)


def _mod_spec(which, row=None):
    d = D_MODEL
    if row is None:
        return pl.BlockSpec((1, 1, 1, d), lambda i, j: (which, i, 0, 0))
    return pl.BlockSpec((1, 1, 1, d), lambda i, j: (which, row, 0, 0))


MOD_SHIFT, MOD_SCALE, MOD_GATE = 0, 1, 2


def _proj_call(x, mod, mod_row, norm_g, w_ext, wt_ext, g_ckv, wk, wv, ck, sk,
               q_side=None, tm=512):
    b, s, d = x.shape
    tm = min(tm, s)
    with_q = q_side is not None
    n_ext = w_ext.shape[0]
    n_t = wt_ext.shape[0]
    const2 = lambda i, j: (0, 0)
    tok3 = lambda i, j: (i, j, 0)
    tab = lambda i, j: (j, 0)

    in_specs = [
        pl.BlockSpec((1, tm, d), tok3),
        _mod_spec(MOD_SHIFT, mod_row),
        _mod_spec(MOD_SCALE, mod_row),
        pl.BlockSpec((1, d), const2),
        pl.BlockSpec((n_ext, d), const2),
        pl.BlockSpec((n_t, d), const2),
        pl.BlockSpec((1, KV_LORA), const2),
        pl.BlockSpec((KV_LORA, HEADS * LANES), const2),
        pl.BlockSpec((HEADS * MLA_V, KV_LORA), const2),
        pl.BlockSpec((tm, LANES), tab),
        pl.BlockSpec((tm, LANES), tab),
    ]
    args = [x, mod, mod, norm_g, w_ext, wt_ext, g_ckv, wk, wv, ck, sk]
    tok_major = lambda w: (pl.BlockSpec((1, tm, w), tok3), jax.ShapeDtypeStruct((b, s, w), BF16))
    feat_major = lambda w: (pl.BlockSpec((1, w, tm), lambda i, j: (i, 0, j)),
                            jax.ShapeDtypeStruct((b, w, s), BF16))
    outs = [tok_major(HEADS * LANES), feat_major(HEADS * MLA_V), tok_major(HEADS * NA_DIM),
            feat_major(HEADS * NA_DIM)]
    if with_q:
        g_cq, wq, wqr, cq, sq = q_side
        in_specs += [
            pl.BlockSpec((1, Q_LORA), const2),
            pl.BlockSpec((HEADS * LANES, Q_LORA), const2),
            pl.BlockSpec((HEADS * LANES, Q_LORA), const2),
            pl.BlockSpec((LANES, tm), lambda i, j: (0, j)),
            pl.BlockSpec((LANES, tm), lambda i, j: (0, j)),
        ]
        args += [g_cq, wq, wqr, cq, sq]
        outs += [feat_major(HEADS * LANES), feat_major(HEADS * NA_DIM)]
    out_specs = [o[0] for o in outs]
    out_shape = [o[1] for o in outs]
    return pl.pallas_call(
        functools.partial(_proj_kernel, with_q),
        grid=(b, s // tm),
        in_specs=in_specs,
        out_specs=out_specs,
        out_shape=out_shape,
        compiler_params=pltpu.CompilerParams(vmem_limit_bytes=VMEM_LIMIT),
        name="in_proj_q" if with_q else "in_proj_ctx",
    )(*args)


BF16_SUBLANES = 16
MLA_TQ = 256
MLA_TILES = 4
MLA_CHUNK = 256
MLA_QK_CHUNKS = 2
MLA_PV_DELAY = 1
MLA_LOOKAHEAD = 6


def _mla_kernel(qt_ref, k_ref, kc_ref, vt_ref, vct_ref, o_ref):
    s_len = k_ref.shape[1]
    streams = [(t, hh) for t in range(MLA_TILES) for hh in range(2)]
    qts = {(t, hh): qt_ref[0, hh * LANES:(hh + 1) * LANES, t * MLA_TQ:(t + 1) * MLA_TQ] for t, hh in streams}
    qk_keys = MLA_QK_CHUNKS * MLA_CHUNK
    groups = []
    for t in range(MLA_TILES):
        groups += [((t, hh), k_ref, vt_ref, c * qk_keys, qk_keys)
                   for c in range(s_len // qk_keys) for hh in range(2)]
        groups += [((t, hh), kc_ref, vct_ref, 0, kc_ref.shape[1]) for hh in range(2)]
    items = [(st, vr, off + i, min(MLA_CHUNK, n - i))
             for st, _, vr, off, n in groups for i in range(0, n, MLA_CHUNK)]

    def group_scores(g):
        st, kr, _, off, n = g
        hh = st[1]
        big = _dot(kr[0, off:off + n, hh * LANES:(hh + 1) * LANES], qts[st]).astype(BF16)
        return [big[i:i + MLA_CHUNK] for i in range(0, n, MLA_CHUNK)]

    m = {st: None for st in streams}
    acc = {st: None for st in streams}
    pending = []
    gi = 0
    last_item = {st[0]: idx for idx, (st, _, _, _) in enumerate(items)}

    def weigh(idx, st, vr, off, n, p, alpha):
        hh = st[1]
        v_ones = jnp.concatenate([vr[0, hh * MLA_V:(hh + 1) * MLA_V, off:off + n],
                                  jnp.ones((BF16_SUBLANES, n), BF16)], axis=0)
        pv = _dot(v_ones, p)
        acc[st] = pv if alpha is None else acc[st] * alpha + pv
        t = st[0]
        if idx == last_item[t]:
            outs = [acc[(t, h2)][:MLA_V] / acc[(t, h2)][MLA_V:MLA_V + 1] for h2 in range(2)]
            o_ref[0, t * MLA_TQ:(t + 1) * MLA_TQ, :] = jnp.concatenate(outs, axis=0).T.astype(BF16)

    delayed = []
    for idx, (st, vr, off, n) in enumerate(items):
        while len(pending) <= MLA_LOOKAHEAD and gi < len(groups):
            pending.extend(group_scores(groups[gi]))
            gi += 1
        s = pending.pop(0)
        mc = jnp.max(s, axis=0, keepdims=True)
        m_new = mc if m[st] is None else jnp.maximum(m[st], mc)
        p = jnp.exp2(s - m_new)
        alpha = None if m[st] is None else jnp.exp2(m[st].astype(F32) - m_new.astype(F32))
        m[st] = m_new
        if len(delayed) == MLA_PV_DELAY:
            weigh(*delayed.pop(0))
        delayed.append((idx, st, vr, off, n, p, alpha))
    for d in delayed:
        weigh(*d)


def _mla_call(qt, k, kc, vt, vct):
    b, _, s = qt.shape
    t = kc.shape[1]
    step_q = MLA_TILES * MLA_TQ
    return pl.pallas_call(
        _mla_kernel,
        grid=(b, HEAD_PAIRS, s // step_q),
        in_specs=[
            pl.BlockSpec((1, 2 * LANES, step_q), lambda i, p, j: (i, p, j)),
            pl.BlockSpec((1, s, 2 * LANES), lambda i, p, j: (i, 0, p)),
            pl.BlockSpec((1, t, 2 * LANES), lambda i, p, j: (i, 0, p)),
            pl.BlockSpec((1, 2 * MLA_V, s), lambda i, p, j: (i, p, 0)),
            pl.BlockSpec((1, 2 * MLA_V, t), lambda i, p, j: (i, p, 0)),
        ],
        out_specs=pl.BlockSpec((1, step_q, LANES), lambda i, p, j: (i, j, p)),
        out_shape=jax.ShapeDtypeStruct((b, s, HEADS * MLA_V), BF16),
        compiler_params=pltpu.CompilerParams(vmem_limit_bytes=VMEM_LIMIT),
        name="mla_attention",
    )(qt, k, kc, vt, vct)


NA_HEADS_PER_STEP = 8
NA_BLOCKS_PER_STEP = 4
NA_CHUNK = 256
NA_LOOKAHEAD = 13


def _na_slab_start(blk):
    return jnp.clip(blk * NA_ROWS_PER_BLOCK - NA_KR // 2, 0, GRID_H - NA_SLAB_ROWS)


def _na_variant(blk):
    return jnp.where(blk == 0, 0, jnp.where(blk == NA_BLOCKS - 1, 2, 1))


def _with_ones(v):
    return jnp.concatenate([v, jnp.ones((BF16_SUBLANES, v.shape[1]), v.dtype)], axis=0)


def _na_kernel(qt_ref, k_ref, vt_ref, kc_ref, vct_ref, *rest):
    bias_refs, o_ref = rest[:-1], rest[-1]
    row_half = lax.broadcasted_iota(jnp.int32, (LANES, NA_TQ), 0) // NA_DIM
    starts = [pl.multiple_of(_na_slab_start(pl.program_id(2) * NA_BLOCKS_PER_STEP + n) * GRID_W, NA_TQ)
              for n in range(NA_BLOCKS_PER_STEP)]
    heads = [(n, h) for n in range(NA_BLOCKS_PER_STEP) for h in range(NA_HEADS_PER_STEP)]
    n_keys = NA_SLAB + kc_ref.shape[1]
    items = [(n, h, c) for n, h in heads for c in range(0, n_keys, NA_CHUNK)]

    def scores(head):
        n, h = head
        pair, hh = divmod(h, 2)
        cols = slice(pair * LANES, (pair + 1) * LANES)
        k_all = jnp.concatenate([k_ref[0, pl.ds(starts[n], NA_SLAB), cols], kc_ref[0, :, cols]], axis=0)
        qt = qt_ref[0, cols, n * NA_TQ:(n + 1) * NA_TQ]
        qm = jnp.where(row_half == hh, qt, jnp.zeros_like(qt))
        s = _dot(k_all, qm).astype(BF16)
        s = jnp.concatenate([s[:NA_SLAB] + bias_refs[n][0, h], s[NA_SLAB:]], axis=0)
        return [s[c:c + NA_CHUNK] for c in range(0, n_keys, NA_CHUNK)]

    m, acc, outs = {}, {}, []
    pending = []
    gi = 0
    for n, h, c in items:
        while len(pending) <= NA_LOOKAHEAD and gi < len(heads):
            pending.extend(scores(heads[gi]))
            gi += 1
        s = pending.pop(0)
        mc = jnp.max(s, axis=0, keepdims=True)
        m_new = mc if c == 0 else jnp.maximum(m[n, h], mc)
        p = jnp.exp2(s - m_new)
        rows = slice(h * NA_DIM, (h + 1) * NA_DIM)
        v_all = jnp.concatenate([vt_ref[0, rows, pl.ds(starts[n], NA_SLAB)], vct_ref[0, rows, :]], axis=1)
        pv = _dot(_with_ones(v_all[:, c:c + NA_CHUNK]), p)
        if c == 0:
            acc[n, h] = pv
        else:
            acc[n, h] = acc[n, h] * jnp.exp2(m[n, h].astype(F32) - m_new.astype(F32)) + pv
        m[n, h] = m_new
        if c + NA_CHUNK >= n_keys:
            a = acc.pop((n, h))
            outs.append(a[:NA_DIM] / a[NA_DIM:NA_DIM + 1])
            if h == NA_HEADS_PER_STEP - 1:
                o_ref[0, n * NA_TQ:(n + 1) * NA_TQ, :] = jnp.concatenate(outs, axis=0).T.astype(BF16)
                outs = []


def _na_call(qbt, kb, vbt, kcb, vcbt, bias):
    b, s, _ = kb.shape
    t = kcb.shape[1]
    wide = NA_HEADS_PER_STEP * NA_DIM
    step_q = NA_BLOCKS_PER_STEP * NA_TQ

    def bias_spec(n):
        return pl.BlockSpec((1, NA_HEADS_PER_STEP, NA_SLAB, NA_TQ),
                            lambda g, i, j: (_na_variant(j * NA_BLOCKS_PER_STEP + n), g, 0, 0))

    return pl.pallas_call(
        _na_kernel,
        grid=(HEADS // NA_HEADS_PER_STEP, b, NA_BLOCKS // NA_BLOCKS_PER_STEP),
        in_specs=[
            pl.BlockSpec((1, wide, step_q), lambda g, i, j: (i, g, j)),
            pl.BlockSpec((1, s, wide), lambda g, i, j: (i, 0, g)),
            pl.BlockSpec((1, wide, s), lambda g, i, j: (i, g, 0)),
            pl.BlockSpec((1, t, wide), lambda g, i, j: (i, 0, g)),
            pl.BlockSpec((1, wide, t), lambda g, i, j: (i, g, 0)),
        ] + [bias_spec(n) for n in range(NA_BLOCKS_PER_STEP)],
        out_specs=pl.BlockSpec((1, step_q, wide), lambda g, i, j: (i, j, g)),
        out_shape=jax.ShapeDtypeStruct((b, s, HEADS * NA_DIM), BF16),
        compiler_params=pltpu.CompilerParams(vmem_limit_bytes=VMEM_LIMIT),
        name="na_attention",
    )(qbt, kb, vbt, kcb, vcbt, *([bias] * NA_BLOCKS_PER_STEP))


def _na_bias_kernel(rpb_ref, o_ref):
    rp = pltpu.roll(rpb_ref[0] * LOG2_E, LANES - (NA_KC - 1), axis=1)
    shape = (GRID_W, LANES)
    wk = lax.broadcasted_iota(jnp.int32, shape, 0)
    lane = lax.broadcasted_iota(jnp.int32, shape, 1)
    c_start = jnp.clip(lane - NA_KC // 2, 0, GRID_W - NA_KC)
    left = lane < GRID_W
    mask_l = jnp.where(left, NA_MASK, 0.0).astype(F32)
    mask_r = jnp.where(left, 0.0, NA_MASK).astype(F32)
    tiles_l, tiles_r = [], []
    for d in range(2 * NA_KR - 1):
        row = jnp.broadcast_to(rp[d:d + 1, :], shape)
        t = pltpu.roll(row, 0, axis=1, stride=1, stride_axis=0)
        t = jnp.where(wk >= c_start, jnp.where(wk < c_start + NA_KC, t, NA_MASK), NA_MASK)
        t = jnp.where(left, t, 0.0)
        tiles_l.append(t)
        tiles_r.append(pltpu.roll(t, GRID_W, axis=1))
    for n, blk in enumerate((0, 1, NA_BLOCKS - 1)):
        sb = min(max(blk * NA_ROWS_PER_BLOCK - NA_KR // 2, 0), GRID_H - NA_SLAB_ROWS)
        for j in range(NA_SLAB_ROWS):
            kr = sb + j

            def pick(i, tiles, masked):
                r = blk * NA_ROWS_PER_BLOCK + i
                r_start = min(max(r - NA_KR // 2, 0), GRID_H - NA_KR)
                if r_start <= kr < r_start + NA_KR:
                    return tiles[kr - r + NA_KR - 1]
                return masked

            for ii in range(NA_ROWS_PER_BLOCK // 2):
                o_ref[n, 0, j * GRID_W:(j + 1) * GRID_W, ii * LANES:(ii + 1) * LANES] = (
                    pick(2 * ii, tiles_l, mask_l) + pick(2 * ii + 1, tiles_r, mask_r)).astype(BF16)


def _na_bias_tables(rpb):
    rpb_p = jnp.pad(rpb[:, :, ::-1], ((0, 0), (0, 1), (0, LANES - (2 * NA_KC - 1))))
    return pl.pallas_call(
        _na_bias_kernel,
        grid=(HEADS,),
        in_specs=[pl.BlockSpec((1, 2 * NA_KR, LANES), lambda h: (h, 0, 0))],
        out_specs=pl.BlockSpec((3, 1, NA_SLAB, NA_TQ), lambda h: (0, h, 0, 0)),
        out_shape=jax.ShapeDtypeStruct((3, HEADS, NA_SLAB, NA_TQ), BF16),
        name="na_bias_table",
    )(rpb_p)


def _out_kernel(oa_ref, ob_ref, x_ref, shift_ref, scale_ref, gate_ref, ng_ref, wg_ref,
                woa_ref, wob_ref, wout_ref, fg_ref, o_ref):
    subs = [slice(i, i + OUT_SUB) for i in range(0, o_ref.shape[1], OUT_SUB)]

    def gates(sl):
        h = _rms(x_ref[0, sl], ng_ref[...]) * (1.0 + scale_ref[0, 0]) + shift_ref[0, 0]
        hb = h.astype(BF16)
        proj = lambda c: _dot_nt(hb, wg_ref[c[0]:c[1], :])
        za = proj(_G_ZA)
        zb = proj(_G_ZB)
        return ((za * jax.nn.sigmoid(za)).astype(BF16), (zb * jax.nn.sigmoid(zb)).astype(BF16),
                jax.nn.sigmoid(proj(_G_GA)).astype(BF16), jax.nn.sigmoid(proj(_G_GB)).astype(BF16))

    def branches(sl, g):
        sza, szb, sga, sgb = g
        ya = _dot(oa_ref[0, sl] * sza, woa_ref[...])
        yb = _dot(ob_ref[0, sl] * szb, wob_ref[...])
        return (sga.astype(F32) * ya + sgb.astype(F32) * yb).astype(BF16)

    def finish(sl, mix):
        y = _dot(mix, wout_ref[...])
        r = x_ref[0, sl] + gate_ref[0, 0] * y
        o_ref[0, sl] = _rms(r, fg_ref[...])

    order = sorted([(3 * i + off, stage, i) for i in range(len(subs))
                    for stage, off in (("gates", 0), ("branches", 4), ("finish", 8))])
    g, mix = {}, {}
    for _, stage, i in order:
        if stage == "gates":
            g[i] = gates(subs[i])
        elif stage == "branches":
            mix[i] = branches(subs[i], g.pop(i))
        else:
            finish(subs[i], mix.pop(i))


OUT_SUB = 256


def _out_call(oa, ob, x, mod, norm_g, w_gate, w_oa, w_ob, w_out, final_g, tm=1024):
    b, s, d = x.shape
    tok3 = lambda i, j: (i, j, 0)
    half = HEADS * MLA_V

    def const(shape):
        return pl.BlockSpec(shape, lambda i, j: (0, 0), pipeline_mode=pl.Buffered(1))

    return pl.pallas_call(
        _out_kernel,
        grid=(b, s // tm),
        in_specs=[
            pl.BlockSpec((1, tm, half), tok3),
            pl.BlockSpec((1, tm, half), tok3),
            pl.BlockSpec((1, tm, d), tok3),
            _mod_spec(MOD_SHIFT),
            _mod_spec(MOD_SCALE),
            _mod_spec(MOD_GATE),
            const((1, d)),
            const((w_gate.shape[0], d)),
            const((half, d)),
            const((half, d)),
            const((d, d)),
            const((1, d)),
        ],
        out_specs=pl.BlockSpec((1, tm, d), tok3),
        out_shape=jax.ShapeDtypeStruct((b, s, d), F32),
        compiler_params=pltpu.CompilerParams(vmem_limit_bytes=VMEM_LIMIT),
        name="gated_out",
    )(oa, ob, x, mod, mod, mod, norm_g, w_gate, w_oa, w_ob, w_out, final_g)


def _rope_partner(w, axis):
    q = MLA_ROPE // 4
    parts = [lax.slice_in_dim(w, n * q, (n + 1) * q, axis=axis) for n in range(4)]
    return jnp.concatenate([-parts[1], parts[0], -parts[3], parts[2]], axis=axis)


def _rope_tables():
    t = np.arange(SEQ)
    half = MLA_ROPE // 4
    freqs = (ROPE_THETA ** (-np.arange(half, dtype=np.float32) / half)).astype(np.float32)
    ang_r = (t // GRID_W).astype(np.float32)[:, None] * freqs[None, :]
    ang_c = (t % GRID_W).astype(np.float32)[:, None] * freqs[None, :]
    cos = np.concatenate([np.cos(ang_r)] * 2 + [np.cos(ang_c)] * 2, axis=-1).astype(np.float32)
    sin = np.concatenate([np.sin(ang_r)] * 2 + [np.sin(ang_c)] * 2, axis=-1).astype(np.float32)
    return cos, sin


def _np_lane_group(nope, rope):
    pad = np.zeros((rope.shape[0], LANES - MLA_NOPE - MLA_ROPE), np.float32)
    return np.concatenate([nope, rope, pad], axis=-1).astype(np.float32)


def _rope_rows(r):
    d = r.shape[1]
    return jnp.concatenate([jnp.zeros((MLA_NOPE, d), F32), r,
                            jnp.zeros((LANES - MLA_NOPE - MLA_ROPE, d), F32)], axis=0)


def kernel(x, c, ctx, c_ctx, w_mod, b_mod, norm_g, w_in, g_cq, w_uq, g_ckv, w_ukv, rpb,
           w_oa, w_ob, w_out, final_g):
    b, s, d = x.shape
    depth = w_mod.shape[0]
    assert depth == 1 and s == SEQ and d == D_MODEL
    mla_scale = (MLA_NOPE + MLA_ROPE) ** -0.5 * LOG2_E
    na_scale = NA_DIM ** -0.5 * LOG2_E

    wt = jnp.transpose(w_in[0])
    o = np.cumsum([0, KV_LORA, MLA_ROPE, HEADS * NA_DIM, HEADS * NA_DIM, Q_LORA, HEADS * MLA_V,
                   HEADS * NA_DIM, HEADS * NA_DIM, d, d])
    seg = [wt[o[n]:o[n + 1]] for n in range(10)]
    w_ckv, w_kr, w_kb, w_vb, w_cq, w_za, w_qb, w_zb, w_ga, w_gb = seg
    w_ext = jnp.concatenate([w_ckv, _rope_rows(w_kr), _rope_rows(_rope_partner(w_kr, 0)), w_kb, w_cq],
                            axis=0).astype(BF16)
    wt_ext = jnp.concatenate([w_vb, w_qb * na_scale], axis=0).astype(BF16)
    w_gate = jnp.concatenate([w_za, w_zb, w_ga, w_gb], axis=0).astype(BF16)

    wkv3 = w_ukv[0].reshape(KV_LORA, HEADS, MLA_NOPE + MLA_V)
    wk = jnp.concatenate([wkv3[:, :, :MLA_NOPE], jnp.zeros((KV_LORA, HEADS, LANES - MLA_NOPE), F32)],
                         axis=-1).reshape(KV_LORA, HEADS * LANES).astype(BF16)
    wv = wkv3[:, :, MLA_NOPE:].reshape(KV_LORA, HEADS * MLA_V).astype(BF16).T
    wq3 = w_uq[0].reshape(Q_LORA, HEADS, MLA_NOPE + MLA_ROPE)
    zq = jnp.zeros((Q_LORA, HEADS, LANES - MLA_NOPE - MLA_ROPE), F32)
    wq = jnp.concatenate([wq3, zq], axis=-1).reshape(Q_LORA, HEADS * LANES).astype(BF16).T
    wqr = jnp.concatenate([jnp.zeros((Q_LORA, HEADS, MLA_NOPE), F32),
                           _rope_partner(wq3[:, :, MLA_NOPE:], 2), zq],
                          axis=-1).reshape(Q_LORA, HEADS * LANES).astype(BF16).T

    cos, sin = _rope_tables()
    ones_s = np.ones((s, MLA_NOPE), np.float32)
    zeros_s = np.zeros((s, MLA_NOPE), np.float32)
    cq_tab = jnp.asarray(np.ascontiguousarray((_np_lane_group(ones_s, cos) * np.float32(mla_scale)).T))
    sq_tab = jnp.asarray(np.ascontiguousarray((_np_lane_group(zeros_s, sin) * np.float32(mla_scale)).T))
    ck_tab = jnp.asarray(_np_lane_group(zeros_s, cos))
    sk_tab = jnp.asarray(_np_lane_group(zeros_s, sin))
    zeros_c = np.zeros((CTX_LEN, MLA_NOPE), np.float32)
    ck_ctx = jnp.asarray(_np_lane_group(zeros_c, np.ones((CTX_LEN, MLA_ROPE), np.float32)))
    sk_ctx = jnp.zeros((CTX_LEN, LANES), F32)

    cs = jnp.concatenate([c, c_ctx[None, :], jnp.zeros((8 - b - 1, d), F32)], axis=0)
    mod = _mod_call(cs, w_mod[0], b_mod[0][None, :])

    ng = norm_g[0][None, :]
    gkv = g_ckv[0][None, :]
    k, v, kb, vb, q, qb = _proj_call(
        x, mod, None, ng, w_ext, wt_ext, gkv, wk, wv, ck_tab, sk_tab,
        q_side=(g_cq[0][None, :], wq, wqr, cq_tab, sq_tab))
    kc, vc, kcb, vcb = _proj_call(
        ctx, mod, b, ng, w_ext[:KV_EXT], wt_ext[:_R_VB[1]], gkv, wk, wv,
        ck_ctx, sk_ctx)

    oa = _mla_call(q, k, kc, v, vc)
    ob = _na_call(qb, kb, vb, kcb, vcb, _na_bias_tables(rpb[0]))

    return _out_call(oa, ob, x, mod, ng, w_gate,
                     w_oa[0].astype(BF16), w_ob[0].astype(BF16), w_out[0].astype(BF16),
                     final_g[None, :])
```

```python
import functools

import jax
import jax.numpy as jnp
import numpy as np
from jax import lax
from jax.experimental import pallas as pl
from jax.experimental.pallas import tpu as pltpu

D_MODEL = 1024
SEQ = 4096
GRID_W = 64
GRID_H = SEQ // GRID_W
CTX_LEN = 256
HEADS = 8
MLA_NOPE = 64
MLA_ROPE = 32
MLA_V = 64
Q_LORA = 256
KV_LORA = 128
NA_DIM = 64
NA_KR = 8
NA_KC = 16
ROPE_THETA = 10000.0
EPS = 1e-6

LANES = 128
HEAD_PAIRS = HEADS // 2
VMEM_LIMIT = 56 * 1024 * 1024

_C_CKV = (0, 128)
_C_KB = (160, 672)
_C_CQ = (1184, 1440)
_G_ZA = (1440, 1952)
_G_ZB = (2464, 2976)
_G_GA = (2976, 4000)
_G_GB = (4000, 5024)
_R_VB = (672, 1184)
_R_QB = (1952, 2464)
_C_KR = (0, 128)
_C_KRR = (128, 256)

NA_ROWS_PER_BLOCK = 4
NA_BLOCKS = GRID_H // NA_ROWS_PER_BLOCK
NA_TQ = NA_ROWS_PER_BLOCK * GRID_W
NA_SLAB_ROWS = 12
NA_SLAB = NA_SLAB_ROWS * GRID_W
NA_MASK = -1e30
LOG2_E = 1.4426950408889634
NA_QSCALE = NA_DIM ** -0.5 * LOG2_E

F32 = jnp.float32
BF16 = jnp.bfloat16


def _dot(a, b):
    return jnp.dot(a, b, preferred_element_type=F32)


def _dot_nt(a, b):
    return lax.dot_general(a, b, (((1,), (1,)), ((), ())), preferred_element_type=F32)


def _rms(x, g):
    return x * lax.rsqrt(jnp.mean(x * x, axis=-1, keepdims=True) + EPS) * g


def _mod_kernel(c_ref, w_ref, b_ref, o_ref):
    c = c_ref[...]
    sc = c * jax.nn.sigmoid(c)
    o_ref[0, :, 0, :] = _dot(sc.astype(BF16), w_ref[...].astype(BF16)) + b_ref[...]


def _mod_call(cs, w_mod, b_mod):
    n = w_mod.shape[1]
    tn = 1024
    return pl.pallas_call(
        _mod_kernel,
        grid=(n // tn,),
        in_specs=[
            pl.BlockSpec((8, D_MODEL), lambda j: (0, 0)),
            pl.BlockSpec((D_MODEL, tn), lambda j: (0, j)),
            pl.BlockSpec((1, tn), lambda j: (0, j)),
        ],
        out_specs=pl.BlockSpec((1, 8, 1, tn), lambda j: (j, 0, 0, 0)),
        out_shape=jax.ShapeDtypeStruct((n // tn, 8, 1, tn), F32),
        name="adaln_mod",
    )(cs, w_mod, b_mod)


def _proj_kernel(with_q, x_ref, shift_ref, scale_ref, ng_ref, w_ref, wkr_ref, gkv_ref, wk_ref, wv_ref,
                 ck_ref, sk_ref, *rest):
    if with_q:
        (gq_ref, wq_ref, wqr_ref, cq_ref, sq_ref,
         k_ref, v_ref, kb_ref, vb_ref, q_ref, qb_ref) = rest
    else:
        k_ref, v_ref, kb_ref, vb_ref = rest

    x = x_ref[0]
    h = _rms(x, ng_ref[...]) * (1.0 + scale_ref[0, 0]) + shift_ref[0, 0]
    hb = h.astype(BF16)

    def proj(c):
        return _dot_nt(hb, w_ref[c[0]:c[1], :])

    def proj_t(r):
        return _dot_nt(w_ref[r[0]:r[1], :], hb)

    def proj_kr(c):
        return _dot_nt(hb, wkr_ref[c[0]:c[1], :])

    ckvn = _rms(proj(_C_CKV), gkv_ref[...]).astype(BF16)
    if with_q:
        cqn = _rms(proj(_C_CQ), gq_ref[...]).astype(BF16)
    krope = proj_kr(_C_KR) * ck_ref[...] + proj_kr(_C_KRR) * sk_ref[...]
    kb_ref[0] = proj(_C_KB).astype(BF16)
    vb_ref[0] = proj_t(_R_VB).astype(BF16)
    if with_q:
        qb_ref[0] = (proj_t(_R_QB) * NA_QSCALE).astype(BF16)

    knope = _dot(ckvn, wk_ref[...])
    for hd in range(HEADS):
        sl = slice(hd * LANES, (hd + 1) * LANES)
        k_ref[0, :, sl] = (knope[:, sl] + krope).astype(BF16)
    v_ref[0] = _dot_nt(wv_ref[...], ckvn).astype(BF16)
    if with_q:
        q1 = _dot_nt(wq_ref[...], cqn)
        q2 = _dot_nt(wqr_ref[...], cqn)
        cq = cq_ref[...]
        sq = sq_ref[...]
        for hd in range(HEADS):
            sl = slice(hd * LANES, (hd + 1) * LANES)
            q_ref[0, sl, :] = (q1[sl] * cq + q2[sl] * sq).astype(BF16)


def _mod_spec(which, row=None):
    d = D_MODEL
    if row is None:
        return pl.BlockSpec((1, 1, 1, d), lambda i, j: (which, i, 0, 0))
    return pl.BlockSpec((1, 1, 1, d), lambda i, j: (which, row, 0, 0))


MOD_SHIFT, MOD_SCALE, MOD_GATE = 0, 1, 2


def _proj_call(x, mod, mod_row, norm_g, w_all, w_kr, g_ckv, wk, wv, ck, sk,
               q_side=None, tm=512):
    b, s, d = x.shape
    tm = min(tm, s)
    with_q = q_side is not None
    n_all = w_all.shape[0]
    n_kr = w_kr.shape[0]
    const2 = lambda i, j: (0, 0)
    tok3 = lambda i, j: (i, j, 0)
    tab = lambda i, j: (j, 0)

    in_specs = [
        pl.BlockSpec((1, tm, d), tok3),
        _mod_spec(MOD_SHIFT, mod_row),
        _mod_spec(MOD_SCALE, mod_row),
        pl.BlockSpec((1, d), const2),
        pl.BlockSpec((n_all, d), const2, pipeline_mode=pl.Buffered(1)),
        pl.BlockSpec((n_kr, d), const2),
        pl.BlockSpec((1, KV_LORA), const2),
        pl.BlockSpec((KV_LORA, HEADS * LANES), const2),
        pl.BlockSpec((HEADS * MLA_V, KV_LORA), const2),
        pl.BlockSpec((tm, LANES), tab),
        pl.BlockSpec((tm, LANES), tab),
    ]
    args = [x, mod, mod, norm_g, w_all, w_kr, g_ckv, wk, wv, ck, sk]
    tok_major = lambda w: (pl.BlockSpec((1, tm, w), tok3), jax.ShapeDtypeStruct((b, s, w), BF16))
    feat_major = lambda w: (pl.BlockSpec((1, w, tm), lambda i, j: (i, 0, j)),
                            jax.ShapeDtypeStruct((b, w, s), BF16))
    outs = [tok_major(HEADS * LANES), feat_major(HEADS * MLA_V), tok_major(HEADS * NA_DIM),
            feat_major(HEADS * NA_DIM)]
    if with_q:
        g_cq, wq, wqr, cq, sq = q_side
        in_specs += [
            pl.BlockSpec((1, Q_LORA), const2),
            pl.BlockSpec((HEADS * LANES, Q_LORA), const2),
            pl.BlockSpec((HEADS * LANES, Q_LORA), const2),
            pl.BlockSpec((LANES, tm), lambda i, j: (0, j)),
            pl.BlockSpec((LANES, tm), lambda i, j: (0, j)),
        ]
        args += [g_cq, wq, wqr, cq, sq]
        outs += [feat_major(HEADS * LANES), feat_major(HEADS * NA_DIM)]
    out_specs = [o[0] for o in outs]
    out_shape = [o[1] for o in outs]
    return pl.pallas_call(
        functools.partial(_proj_kernel, with_q),
        grid=(b, s // tm),
        in_specs=in_specs,
        out_specs=out_specs,
        out_shape=out_shape,
        compiler_params=pltpu.CompilerParams(vmem_limit_bytes=VMEM_LIMIT),
        name="in_proj_q" if with_q else "in_proj_ctx",
    )(*args)


BF16_SUBLANES = 16
MLA_TQ = 256
MLA_TILES = 4
MLA_CHUNK = 256
MLA_QK_CHUNKS = 2
MLA_PV_DELAY = 1
MLA_LOOKAHEAD = 6


def _mla_kernel(qt_ref, k_ref, kc_ref, vt_ref, vct_ref, o_ref):
    s_len = k_ref.shape[1]
    streams = [(t, hh) for t in range(MLA_TILES) for hh in range(2)]
    qts = {(t, hh): qt_ref[0, hh * LANES:(hh + 1) * LANES, t * MLA_TQ:(t + 1) * MLA_TQ] for t, hh in streams}
    qk_keys = MLA_QK_CHUNKS * MLA_CHUNK
    groups = [(st, k_ref, vt_ref, c * qk_keys, qk_keys) for c in range(s_len // qk_keys) for st in streams]
    groups += [(st, kc_ref, vct_ref, 0, kc_ref.shape[1]) for st in streams]
    items = [(st, vr, off + i, min(MLA_CHUNK, n - i))
             for st, _, vr, off, n in groups for i in range(0, n, MLA_CHUNK)]

    def group_scores(g):
        st, kr, _, off, n = g
        hh = st[1]
        big = _dot(kr[0, off:off + n, hh * LANES:(hh + 1) * LANES], qts[st]).astype(BF16)
        return [big[i:i + MLA_CHUNK] for i in range(0, n, MLA_CHUNK)]

    m = {st: None for st in streams}
    acc = {st: None for st in streams}
    pending = []
    gi = 0
    last_item = {st[0]: idx for idx, (st, _, _, _) in enumerate(items)}

    def weigh(idx, st, vr, off, n, p, alpha):
        hh = st[1]
        v_ones = jnp.concatenate([vr[0, hh * MLA_V:(hh + 1) * MLA_V, off:off + n],
                                  jnp.ones((BF16_SUBLANES, n), BF16)], axis=0)
        pv = _dot(v_ones, p)
        acc[st] = pv if alpha is None else acc[st] * alpha + pv
        t = st[0]
        if idx == last_item[t]:
            outs = [acc[(t, h2)][:MLA_V] / acc[(t, h2)][MLA_V:MLA_V + 1] for h2 in range(2)]
            o_ref[0, t * MLA_TQ:(t + 1) * MLA_TQ, :] = jnp.concatenate(outs, axis=0).T.astype(BF16)

    delayed = []
    for idx, (st, vr, off, n) in enumerate(items):
        while len(pending) <= MLA_LOOKAHEAD and gi < len(groups):
            pending.extend(group_scores(groups[gi]))
            gi += 1
        s = pending.pop(0)
        mc = jnp.max(s, axis=0, keepdims=True)
        m_new = mc if m[st] is None else jnp.maximum(m[st], mc)
        p = jnp.exp2(s - m_new)
        alpha = None if m[st] is None else jnp.exp2(m[st].astype(F32) - m_new.astype(F32))
        m[st] = m_new
        if len(delayed) == MLA_PV_DELAY:
            weigh(*delayed.pop(0))
        delayed.append((idx, st, vr, off, n, p, alpha))
    for d in delayed:
        weigh(*d)


def _mla_call(qt, k, kc, vt, vct):
    b, _, s = qt.shape
    t = kc.shape[1]
    step_q = MLA_TILES * MLA_TQ
    return pl.pallas_call(
        _mla_kernel,
        grid=(b, HEAD_PAIRS, s // step_q),
        in_specs=[
            pl.BlockSpec((1, 2 * LANES, step_q), lambda i, p, j: (i, p, j)),
            pl.BlockSpec((1, s, 2 * LANES), lambda i, p, j: (i, 0, p)),
            pl.BlockSpec((1, t, 2 * LANES), lambda i, p, j: (i, 0, p)),
            pl.BlockSpec((1, 2 * MLA_V, s), lambda i, p, j: (i, p, 0)),
            pl.BlockSpec((1, 2 * MLA_V, t), lambda i, p, j: (i, p, 0)),
        ],
        out_specs=pl.BlockSpec((1, step_q, LANES), lambda i, p, j: (i, j, p)),
        out_shape=jax.ShapeDtypeStruct((b, s, HEADS * MLA_V), BF16),
        compiler_params=pltpu.CompilerParams(vmem_limit_bytes=VMEM_LIMIT),
        name="mla_attention",
    )(qt, k, kc, vt, vct)


NA_HEADS_PER_STEP = 8
NA_BLOCKS_PER_STEP = 4
NA_CHUNK = 256
NA_LOOKAHEAD = 13


def _na_slab_start(blk):
    return jnp.clip(blk * NA_ROWS_PER_BLOCK - NA_KR // 2, 0, GRID_H - NA_SLAB_ROWS)


def _na_variant(blk):
    return jnp.where(blk == 0, 0, jnp.where(blk == NA_BLOCKS - 1, 2, 1))


def _with_ones(v):
    return jnp.concatenate([v, jnp.ones((BF16_SUBLANES, v.shape[1]), v.dtype)], axis=0)


def _na_kernel(qt_ref, k_ref, vt_ref, kc_ref, vct_ref, *rest):
    bias_refs, o_ref = rest[:-1], rest[-1]
    row_half = lax.broadcasted_iota(jnp.int32, (LANES, NA_TQ), 0) // NA_DIM
    starts = [pl.multiple_of(_na_slab_start(pl.program_id(2) * NA_BLOCKS_PER_STEP + n) * GRID_W, NA_TQ)
              for n in range(NA_BLOCKS_PER_STEP)]
    heads = [(n, h) for n in range(NA_BLOCKS_PER_STEP) for h in range(NA_HEADS_PER_STEP)]
    n_keys = NA_SLAB + kc_ref.shape[1]
    items = [(n, h, c) for n, h in heads for c in range(0, n_keys, NA_CHUNK)]

    def scores(head):
        n, h = head
        pair, hh = divmod(h, 2)
        cols = slice(pair * LANES, (pair + 1) * LANES)
        k_all = jnp.concatenate([k_ref[0, pl.ds(starts[n], NA_SLAB), cols], kc_ref[0, :, cols]], axis=0)
        qt = qt_ref[0, cols, n * NA_TQ:(n + 1) * NA_TQ]
        qm = jnp.where(row_half == hh, qt, jnp.zeros_like(qt))
        s = _dot(k_all, qm).astype(BF16)
        s = jnp.concatenate([s[:NA_SLAB] + bias_refs[n][0, h], s[NA_SLAB:]], axis=0)
        return [s[c:c + NA_CHUNK] for c in range(0, n_keys, NA_CHUNK)]

    m, acc, outs = {}, {}, []
    pending = []
    gi = 0
    for n, h, c in items:
        while len(pending) <= NA_LOOKAHEAD and gi < len(heads):
            pending.extend(scores(heads[gi]))
            gi += 1
        s = pending.pop(0)
        mc = jnp.max(s, axis=0, keepdims=True)
        m_new = mc if c == 0 else jnp.maximum(m[n, h], mc)
        p = jnp.exp2(s - m_new)
        rows = slice(h * NA_DIM, (h + 1) * NA_DIM)
        v_all = jnp.concatenate([vt_ref[0, rows, pl.ds(starts[n], NA_SLAB)], vct_ref[0, rows, :]], axis=1)
        pv = _dot(_with_ones(v_all[:, c:c + NA_CHUNK]), p)
        if c == 0:
            acc[n, h] = pv
        else:
            acc[n, h] = acc[n, h] * jnp.exp2(m[n, h].astype(F32) - m_new.astype(F32)) + pv
        m[n, h] = m_new
        if c + NA_CHUNK >= n_keys:
            a = acc.pop((n, h))
            outs.append(a[:NA_DIM] / a[NA_DIM:NA_DIM + 1])
            if h == NA_HEADS_PER_STEP - 1:
                o_ref[0, n * NA_TQ:(n + 1) * NA_TQ, :] = jnp.concatenate(outs, axis=0).T.astype(BF16)
                outs = []


def _na_call(qbt, kb, vbt, kcb, vcbt, bias):
    b, s, _ = kb.shape
    t = kcb.shape[1]
    wide = NA_HEADS_PER_STEP * NA_DIM
    step_q = NA_BLOCKS_PER_STEP * NA_TQ

    def bias_spec(n):
        return pl.BlockSpec((1, NA_HEADS_PER_STEP, NA_SLAB, NA_TQ),
                            lambda g, i, j: (_na_variant(j * NA_BLOCKS_PER_STEP + n), g, 0, 0))

    return pl.pallas_call(
        _na_kernel,
        grid=(HEADS // NA_HEADS_PER_STEP, b, NA_BLOCKS // NA_BLOCKS_PER_STEP),
        in_specs=[
            pl.BlockSpec((1, wide, step_q), lambda g, i, j: (i, g, j)),
            pl.BlockSpec((1, s, wide), lambda g, i, j: (i, 0, g)),
            pl.BlockSpec((1, wide, s), lambda g, i, j: (i, g, 0)),
            pl.BlockSpec((1, t, wide), lambda g, i, j: (i, 0, g)),
            pl.BlockSpec((1, wide, t), lambda g, i, j: (i, g, 0)),
        ] + [bias_spec(n) for n in range(NA_BLOCKS_PER_STEP)],
        out_specs=pl.BlockSpec((1, step_q, wide), lambda g, i, j: (i, j, g)),
        out_shape=jax.ShapeDtypeStruct((b, s, HEADS * NA_DIM), BF16),
        compiler_params=pltpu.CompilerParams(vmem_limit_bytes=VMEM_LIMIT),
        name="na_attention",
    )(qbt, kb, vbt, kcb, vcbt, *([bias] * NA_BLOCKS_PER_STEP))


def _na_bias_kernel(rpb_ref, o_ref):
    rp = pltpu.roll(rpb_ref[0] * LOG2_E, LANES - (NA_KC - 1), axis=1)
    shape = (GRID_W, LANES)
    wk = lax.broadcasted_iota(jnp.int32, shape, 0)
    lane = lax.broadcasted_iota(jnp.int32, shape, 1)
    c_start = jnp.clip(lane - NA_KC // 2, 0, GRID_W - NA_KC)
    left = lane < GRID_W
    mask_l = jnp.where(left, NA_MASK, 0.0).astype(F32)
    mask_r = jnp.where(left, 0.0, NA_MASK).astype(F32)
    tiles_l, tiles_r = [], []
    for d in range(2 * NA_KR - 1):
        row = jnp.broadcast_to(rp[d:d + 1, :], shape)
        t = pltpu.roll(row, 0, axis=1, stride=1, stride_axis=0)
        t = jnp.where(wk >= c_start, jnp.where(wk < c_start + NA_KC, t, NA_MASK), NA_MASK)
        t = jnp.where(left, t, 0.0)
        tiles_l.append(t)
        tiles_r.append(pltpu.roll(t, GRID_W, axis=1))
    for n, blk in enumerate((0, 1, NA_BLOCKS - 1)):
        sb = min(max(blk * NA_ROWS_PER_BLOCK - NA_KR // 2, 0), GRID_H - NA_SLAB_ROWS)
        for j in range(NA_SLAB_ROWS):
            kr = sb + j

            def pick(i, tiles, masked):
                r = blk * NA_ROWS_PER_BLOCK + i
                r_start = min(max(r - NA_KR // 2, 0), GRID_H - NA_KR)
                if r_start <= kr < r_start + NA_KR:
                    return tiles[kr - r + NA_KR - 1]
                return masked

            for ii in range(NA_ROWS_PER_BLOCK // 2):
                o_ref[n, 0, j * GRID_W:(j + 1) * GRID_W, ii * LANES:(ii + 1) * LANES] = (
                    pick(2 * ii, tiles_l, mask_l) + pick(2 * ii + 1, tiles_r, mask_r)).astype(BF16)


def _na_bias_tables(rpb):
    rpb_p = jnp.pad(rpb[:, :, ::-1], ((0, 0), (0, 1), (0, LANES - (2 * NA_KC - 1))))
    return pl.pallas_call(
        _na_bias_kernel,
        grid=(HEADS,),
        in_specs=[pl.BlockSpec((1, 2 * NA_KR, LANES), lambda h: (h, 0, 0))],
        out_specs=pl.BlockSpec((3, 1, NA_SLAB, NA_TQ), lambda h: (0, h, 0, 0)),
        out_shape=jax.ShapeDtypeStruct((3, HEADS, NA_SLAB, NA_TQ), BF16),
        name="na_bias_table",
    )(rpb_p)


def _out_kernel(oa_ref, ob_ref, x_ref, shift_ref, scale_ref, gate_ref, ng_ref, wg_ref,
                woa_ref, wob_ref, wout_ref, fg_ref, o_ref):
    subs = [slice(i, i + OUT_SUB) for i in range(0, o_ref.shape[1], OUT_SUB)]

    def gates(sl):
        h = _rms(x_ref[0, sl], ng_ref[...]) * (1.0 + scale_ref[0, 0]) + shift_ref[0, 0]
        hb = h.astype(BF16)
        proj = lambda c: _dot_nt(hb, wg_ref[c[0]:c[1], :])
        za = proj(_G_ZA)
        zb = proj(_G_ZB)
        return ((za * jax.nn.sigmoid(za)).astype(BF16), (zb * jax.nn.sigmoid(zb)).astype(BF16),
                jax.nn.sigmoid(proj(_G_GA)).astype(BF16), jax.nn.sigmoid(proj(_G_GB)).astype(BF16))

    def branches(sl, g):
        sza, szb, sga, sgb = g
        ya = _dot(oa_ref[0, sl] * sza, woa_ref[...])
        yb = _dot(ob_ref[0, sl] * szb, wob_ref[...])
        return (sga.astype(F32) * ya + sgb.astype(F32) * yb).astype(BF16)

    def finish(sl, mix):
        y = _dot(mix, wout_ref[...])
        r = x_ref[0, sl] + gate_ref[0, 0] * y
        o_ref[0, sl] = _rms(r, fg_ref[...])

    order = sorted([(3 * i + off, stage, i) for i in range(len(subs))
                    for stage, off in (("gates", 0), ("branches", 4), ("finish", 8))])
    g, mix = {}, {}
    for _, stage, i in order:
        if stage == "gates":
            g[i] = gates(subs[i])
        elif stage == "branches":
            mix[i] = branches(subs[i], g.pop(i))
        else:
            finish(subs[i], mix.pop(i))


OUT_SUB = 256


def _out_call(oa, ob, x, mod, norm_g, w_gate, w_oa, w_ob, w_out, final_g, tm=1024):
    b, s, d = x.shape
    tok3 = lambda i, j: (i, j, 0)
    half = HEADS * MLA_V

    def const(shape):
        return pl.BlockSpec(shape, lambda i, j: (0, 0), pipeline_mode=pl.Buffered(1))

    return pl.pallas_call(
        _out_kernel,
        grid=(b, s // tm),
        in_specs=[
            pl.BlockSpec((1, tm, half), tok3),
            pl.BlockSpec((1, tm, half), tok3),
            pl.BlockSpec((1, tm, d), tok3),
            _mod_spec(MOD_SHIFT),
            _mod_spec(MOD_SCALE),
            _mod_spec(MOD_GATE),
            const((1, d)),
            const((w_gate.shape[0], d)),
            const((half, d)),
            const((half, d)),
            const((d, d)),
            const((1, d)),
        ],
        out_specs=pl.BlockSpec((1, tm, d), tok3),
        out_shape=jax.ShapeDtypeStruct((b, s, d), F32),
        compiler_params=pltpu.CompilerParams(vmem_limit_bytes=VMEM_LIMIT),
        name="gated_out",
    )(oa, ob, x, mod, mod, mod, norm_g, w_gate, w_oa, w_ob, w_out, final_g)


def _rope_partner(w, axis):
    q = MLA_ROPE // 4
    parts = [lax.slice_in_dim(w, n * q, (n + 1) * q, axis=axis) for n in range(4)]
    return jnp.concatenate([-parts[1], parts[0], -parts[3], parts[2]], axis=axis)


def _rope_tables():
    t = np.arange(SEQ)
    half = MLA_ROPE // 4
    freqs = (ROPE_THETA ** (-np.arange(half, dtype=np.float32) / half)).astype(np.float32)
    ang_r = (t // GRID_W).astype(np.float32)[:, None] * freqs[None, :]
    ang_c = (t % GRID_W).astype(np.float32)[:, None] * freqs[None, :]
    cos = np.concatenate([np.cos(ang_r)] * 2 + [np.cos(ang_c)] * 2, axis=-1).astype(np.float32)
    sin = np.concatenate([np.sin(ang_r)] * 2 + [np.sin(ang_c)] * 2, axis=-1).astype(np.float32)
    return cos, sin


def _np_lane_group(nope, rope):
    pad = np.zeros((rope.shape[0], LANES - MLA_NOPE - MLA_ROPE), np.float32)
    return np.concatenate([nope, rope, pad], axis=-1).astype(np.float32)


def _rope_rows(r):
    d = r.shape[1]
    return jnp.concatenate([jnp.zeros((MLA_NOPE, d), F32), r,
                            jnp.zeros((LANES - MLA_NOPE - MLA_ROPE, d), F32)], axis=0)


def kernel(x, c, ctx, c_ctx, w_mod, b_mod, norm_g, w_in, g_cq, w_uq, g_ckv, w_ukv, rpb,
           w_oa, w_ob, w_out, final_g):
    b, s, d = x.shape
    depth = w_mod.shape[0]
    assert depth == 1 and s == SEQ and d == D_MODEL
    mla_scale = (MLA_NOPE + MLA_ROPE) ** -0.5 * LOG2_E

    wt = jnp.transpose(w_in[0])
    w_all = wt.astype(BF16)
    w_kr = wt[KV_LORA:KV_LORA + MLA_ROPE]
    w_kr2 = jnp.concatenate([_rope_rows(w_kr), _rope_rows(_rope_partner(w_kr, 0))], axis=0).astype(BF16)

    wkv3 = w_ukv[0].reshape(KV_LORA, HEADS, MLA_NOPE + MLA_V)
    wk = jnp.concatenate([wkv3[:, :, :MLA_NOPE], jnp.zeros((KV_LORA, HEADS, LANES - MLA_NOPE), F32)],
                         axis=-1).reshape(KV_LORA, HEADS * LANES).astype(BF16)
    wv = wkv3[:, :, MLA_NOPE:].reshape(KV_LORA, HEADS * MLA_V).astype(BF16).T
    wq3 = w_uq[0].reshape(Q_LORA, HEADS, MLA_NOPE + MLA_ROPE)
    zq = jnp.zeros((Q_LORA, HEADS, LANES - MLA_NOPE - MLA_ROPE), F32)
    wq = jnp.concatenate([wq3, zq], axis=-1).reshape(Q_LORA, HEADS * LANES).astype(BF16).T
    wqr = jnp.concatenate([jnp.zeros((Q_LORA, HEADS, MLA_NOPE), F32),
                           _rope_partner(wq3[:, :, MLA_NOPE:], 2), zq],
                          axis=-1).reshape(Q_LORA, HEADS * LANES).astype(BF16).T

    cos, sin = _rope_tables()
    ones_s = np.ones((s, MLA_NOPE), np.float32)
    zeros_s = np.zeros((s, MLA_NOPE), np.float32)
    cq_tab = jnp.asarray(np.ascontiguousarray((_np_lane_group(ones_s, cos) * np.float32(mla_scale)).T))
    sq_tab = jnp.asarray(np.ascontiguousarray((_np_lane_group(zeros_s, sin) * np.float32(mla_scale)).T))
    ck_tab = jnp.asarray(_np_lane_group(zeros_s, cos))
    sk_tab = jnp.asarray(_np_lane_group(zeros_s, sin))
    zeros_c = np.zeros((CTX_LEN, MLA_NOPE), np.float32)
    ck_ctx = jnp.asarray(_np_lane_group(zeros_c, np.ones((CTX_LEN, MLA_ROPE), np.float32)))
    sk_ctx = jnp.zeros((CTX_LEN, LANES), F32)

    cs = jnp.concatenate([c, c_ctx[None, :], jnp.zeros((8 - b - 1, d), F32)], axis=0)
    mod = _mod_call(cs, w_mod[0], b_mod[0][None, :])

    ng = norm_g[0][None, :]
    gkv = g_ckv[0][None, :]
    k, v, kb, vb, q, qb = _proj_call(
        x, mod, None, ng, w_all, w_kr2, gkv, wk, wv, ck_tab, sk_tab,
        q_side=(g_cq[0][None, :], wq, wqr, cq_tab, sq_tab))
    kc, vc, kcb, vcb = _proj_call(
        ctx, mod, b, ng, w_all, w_kr2, gkv, wk, wv,
        ck_ctx, sk_ctx)

    oa = _mla_call(q, k, kc, v, vc)
    ob = _na_call(qb, kb, vb, kcb, vcb, _na_bias_tables(rpb[0]))

    return _out_call(oa, ob, x, mod, ng, w_all,
                     w_oa[0].astype(BF16), w_ob[0].astype(BF16), w_out[0].astype(BF16),
                     final_g[None, :])
```

```python
import functools

import jax
import jax.numpy as jnp
import numpy as np
from jax import lax
from jax.experimental import pallas as pl
from jax.experimental.pallas import tpu as pltpu

D_MODEL = 1024
SEQ = 4096
GRID_W = 64
GRID_H = SEQ // GRID_W
CTX_LEN = 256
HEADS = 8
MLA_NOPE = 64
MLA_ROPE = 32
MLA_V = 64
Q_LORA = 256
KV_LORA = 128
NA_DIM = 64
NA_KR = 8
NA_KC = 16
ROPE_THETA = 10000.0
EPS = 1e-6

LANES = 128
HEAD_PAIRS = HEADS // 2
VMEM_LIMIT = 56 * 1024 * 1024

_C_CKV = (0, 128)
_C_KB = (160, 672)
_C_CQ = (1184, 1440)
_G_ZA = (1440, 1952)
_G_ZB = (2464, 2976)
_G_GA = (2976, 4000)
_G_GB = (4000, 5024)
_R_VB = (672, 1184)
_R_QB = (1952, 2464)
_C_KR = (0, 128)
_C_KRR = (128, 256)

NA_ROWS_PER_BLOCK = 4
NA_BLOCKS = GRID_H // NA_ROWS_PER_BLOCK
NA_TQ = NA_ROWS_PER_BLOCK * GRID_W
NA_SLAB_ROWS = 12
NA_SLAB = NA_SLAB_ROWS * GRID_W
NA_MASK = -1e30
LOG2_E = 1.4426950408889634
NA_QSCALE = NA_DIM ** -0.5 * LOG2_E

F32 = jnp.float32
BF16 = jnp.bfloat16


def _dot(a, b):
    return jnp.dot(a, b, preferred_element_type=F32)


def _dot_nt(a, b):
    return lax.dot_general(a, b, (((1,), (1,)), ((), ())), preferred_element_type=F32)


def _rms(x, g):
    return x * lax.rsqrt(jnp.mean(x * x, axis=-1, keepdims=True) + EPS) * g


def _mod_kernel(c_ref, w_ref, b_ref, o_ref):
    c = c_ref[...]
    sc = c * jax.nn.sigmoid(c)
    o_ref[0, :, 0, :] = _dot(sc.astype(BF16), w_ref[...].astype(BF16)) + b_ref[...]


def _mod_call(cs, w_mod, b_mod):
    n = w_mod.shape[1]
    tn = 1024
    return pl.pallas_call(
        _mod_kernel,
        grid=(n // tn,),
        in_specs=[
            pl.BlockSpec((8, D_MODEL), lambda j: (0, 0)),
            pl.BlockSpec((D_MODEL, tn), lambda j: (0, j)),
            pl.BlockSpec((1, tn), lambda j: (0, j)),
        ],
        out_specs=pl.BlockSpec((1, 8, 1, tn), lambda j: (j, 0, 0, 0)),
        out_shape=jax.ShapeDtypeStruct((n // tn, 8, 1, tn), F32),
        name="adaln_mod",
    )(cs, w_mod, b_mod)


def _proj_kernel(with_q, x_ref, shift_ref, scale_ref, ng_ref, w_ref, wkr_ref, gkv_ref, wk_ref, wv_ref,
                 ck_ref, sk_ref, *rest):
    if with_q:
        (gq_ref, wq_ref, wqr_ref, cq_ref, sq_ref,
         k_ref, v_ref, kb_ref, vb_ref, q_ref, qb_ref) = rest
    else:
        k_ref, v_ref, kb_ref, vb_ref = rest

    x = x_ref[0]
    h = _rms(x, ng_ref[...]) * (1.0 + scale_ref[0, 0]) + shift_ref[0, 0]
    hb = h.astype(BF16)

    def proj(c):
        return _dot_nt(hb, w_ref[c[0]:c[1], :])

    def proj_t(r):
        return _dot_nt(w_ref[r[0]:r[1], :], hb)

    def proj_kr(c):
        return _dot_nt(hb, wkr_ref[c[0]:c[1], :])

    ckvn = _rms(proj(_C_CKV), gkv_ref[...]).astype(BF16)
    if with_q:
        cqn = _rms(proj(_C_CQ), gq_ref[...]).astype(BF16)
    krope = proj_kr(_C_KR) * ck_ref[...] + proj_kr(_C_KRR) * sk_ref[...]
    kb_ref[0] = proj(_C_KB).astype(BF16)
    vb_ref[0] = proj_t(_R_VB).astype(BF16)
    if with_q:
        qb_ref[0] = (proj_t(_R_QB) * NA_QSCALE).astype(BF16)

    knope = _dot(ckvn, wk_ref[...])
    for hd in range(HEADS):
        sl = slice(hd * LANES, (hd + 1) * LANES)
        k_ref[0, :, sl] = (knope[:, sl] + krope).astype(BF16)
    v_ref[0] = _dot_nt(wv_ref[...], ckvn).astype(BF16)
    if with_q:
        q1 = _dot_nt(wq_ref[...], cqn)
        q2 = _dot_nt(wqr_ref[...], cqn)
        cq = cq_ref[...]
        sq = sq_ref[...]
        for hd in range(HEADS):
            sl = slice(hd * LANES, (hd + 1) * LANES)
            q_ref[0, sl, :] = (q1[sl] * cq + q2[sl] * sq).astype(BF16)


def _mod_spec(which, row=None):
    d = D_MODEL
    if row is None:
        return pl.BlockSpec((1, 1, 1, d), lambda i, j: (which, i, 0, 0))
    return pl.BlockSpec((1, 1, 1, d), lambda i, j: (which, row, 0, 0))


MOD_SHIFT, MOD_SCALE, MOD_GATE = 0, 1, 2


def _proj_call(x, mod, mod_row, norm_g, w_all, w_kr, g_ckv, wk, wv, ck, sk,
               q_side=None, tm=512):
    b, s, d = x.shape
    tm = min(tm, s)
    with_q = q_side is not None
    n_all = w_all.shape[0]
    n_kr = w_kr.shape[0]
    const2 = lambda i, j: (0, 0)
    tok3 = lambda i, j: (i, j, 0)
    tab = lambda i, j: (j, 0)

    in_specs = [
        pl.BlockSpec((1, tm, d), tok3),
        _mod_spec(MOD_SHIFT, mod_row),
        _mod_spec(MOD_SCALE, mod_row),
        pl.BlockSpec((1, d), const2),
        pl.BlockSpec((n_all, d), const2, pipeline_mode=pl.Buffered(1)),
        pl.BlockSpec((n_kr, d), const2),
        pl.BlockSpec((1, KV_LORA), const2),
        pl.BlockSpec((KV_LORA, HEADS * LANES), const2),
        pl.BlockSpec((HEADS * MLA_V, KV_LORA), const2),
        pl.BlockSpec((tm, LANES), tab),
        pl.BlockSpec((tm, LANES), tab),
    ]
    args = [x, mod, mod, norm_g, w_all, w_kr, g_ckv, wk, wv, ck, sk]
    tok_major = lambda w: (pl.BlockSpec((1, tm, w), tok3), jax.ShapeDtypeStruct((b, s, w), BF16))
    feat_major = lambda w: (pl.BlockSpec((1, w, tm), lambda i, j: (i, 0, j)),
                            jax.ShapeDtypeStruct((b, w, s), BF16))
    outs = [tok_major(HEADS * LANES), feat_major(HEADS * MLA_V), tok_major(HEADS * NA_DIM),
            feat_major(HEADS * NA_DIM)]
    if with_q:
        g_cq, wq, wqr, cq, sq = q_side
        in_specs += [
            pl.BlockSpec((1, Q_LORA), const2),
            pl.BlockSpec((HEADS * LANES, Q_LORA), const2),
            pl.BlockSpec((HEADS * LANES, Q_LORA), const2),
            pl.BlockSpec((LANES, tm), lambda i, j: (0, j)),
            pl.BlockSpec((LANES, tm), lambda i, j: (0, j)),
        ]
        args += [g_cq, wq, wqr, cq, sq]
        outs += [feat_major(HEADS * LANES), feat_major(HEADS * NA_DIM)]
    out_specs = [o[0] for o in outs]
    out_shape = [o[1] for o in outs]
    return pl.pallas_call(
        functools.partial(_proj_kernel, with_q),
        grid=(b, s // tm),
        in_specs=in_specs,
        out_specs=out_specs,
        out_shape=out_shape,
        compiler_params=pltpu.CompilerParams(vmem_limit_bytes=VMEM_LIMIT),
        name="in_proj_q" if with_q else "in_proj_ctx",
    )(*args)


BF16_SUBLANES = 16
MLA_TQ = 256
MLA_TILES = 8
MLA_CHUNK = 256
MLA_QK_CHUNKS = 2
MLA_PV_DELAY = 1
MLA_LOOKAHEAD = 6


def _mla_kernel(qt_ref, k_ref, kc_ref, vt_ref, vct_ref, o_ref):
    s_len = k_ref.shape[1]
    streams = [(t, hh) for t in range(MLA_TILES) for hh in range(2)]
    qts = {(t, hh): qt_ref[0, hh * LANES:(hh + 1) * LANES, t * MLA_TQ:(t + 1) * MLA_TQ] for t, hh in streams}
    qk_keys = MLA_QK_CHUNKS * MLA_CHUNK
    groups = [(st, k_ref, vt_ref, c * qk_keys, qk_keys) for c in range(s_len // qk_keys) for st in streams]
    groups += [(st, kc_ref, vct_ref, 0, kc_ref.shape[1]) for st in streams]
    items = [(st, vr, off + i, min(MLA_CHUNK, n - i))
             for st, _, vr, off, n in groups for i in range(0, n, MLA_CHUNK)]

    def group_scores(g):
        st, kr, _, off, n = g
        hh = st[1]
        big = _dot(kr[0, off:off + n, hh * LANES:(hh + 1) * LANES], qts[st]).astype(BF16)
        return [big[i:i + MLA_CHUNK] for i in range(0, n, MLA_CHUNK)]

    m = {st: None for st in streams}
    acc = {st: None for st in streams}
    pending = []
    gi = 0
    last_item = {st[0]: idx for idx, (st, _, _, _) in enumerate(items)}

    def weigh(idx, st, vr, off, n, p, alpha):
        hh = st[1]
        v_ones = jnp.concatenate([vr[0, hh * MLA_V:(hh + 1) * MLA_V, off:off + n],
                                  jnp.ones((BF16_SUBLANES, n), BF16)], axis=0)
        pv = _dot(v_ones, p)
        acc[st] = pv if alpha is None else acc[st] * alpha + pv
        t = st[0]
        if idx == last_item[t]:
            outs = [acc[(t, h2)][:MLA_V] / acc[(t, h2)][MLA_V:MLA_V + 1] for h2 in range(2)]
            o_ref[0, t * MLA_TQ:(t + 1) * MLA_TQ, :] = jnp.concatenate(outs, axis=0).T.astype(BF16)

    delayed = []
    for idx, (st, vr, off, n) in enumerate(items):
        while len(pending) <= MLA_LOOKAHEAD and gi < len(groups):
            pending.extend(group_scores(groups[gi]))
            gi += 1
        s = pending.pop(0)
        mc = jnp.max(s, axis=0, keepdims=True)
        m_new = mc if m[st] is None else jnp.maximum(m[st], mc)
        p = jnp.exp2(s - m_new)
        alpha = None if m[st] is None else jnp.exp2(m[st].astype(F32) - m_new.astype(F32))
        m[st] = m_new
        if len(delayed) == MLA_PV_DELAY:
            weigh(*delayed.pop(0))
        delayed.append((idx, st, vr, off, n, p, alpha))
    for d in delayed:
        weigh(*d)


def _mla_call(qt, k, kc, vt, vct):
    b, _, s = qt.shape
    t = kc.shape[1]
    step_q = MLA_TILES * MLA_TQ
    return pl.pallas_call(
        _mla_kernel,
        grid=(b, HEAD_PAIRS, s // step_q),
        in_specs=[
            pl.BlockSpec((1, 2 * LANES, step_q), lambda i, p, j: (i, p, j)),
            pl.BlockSpec((1, s, 2 * LANES), lambda i, p, j: (i, 0, p)),
            pl.BlockSpec((1, t, 2 * LANES), lambda i, p, j: (i, 0, p)),
            pl.BlockSpec((1, 2 * MLA_V, s), lambda i, p, j: (i, p, 0)),
            pl.BlockSpec((1, 2 * MLA_V, t), lambda i, p, j: (i, p, 0)),
        ],
        out_specs=pl.BlockSpec((1, step_q, LANES), lambda i, p, j: (i, j, p)),
        out_shape=jax.ShapeDtypeStruct((b, s, HEADS * MLA_V), BF16),
        compiler_params=pltpu.CompilerParams(vmem_limit_bytes=VMEM_LIMIT),
        name="mla_attention",
    )(qt, k, kc, vt, vct)


NA_HEADS_PER_STEP = 8
NA_BLOCKS_PER_STEP = 4
NA_CHUNK = 256
NA_LOOKAHEAD = 13


def _na_slab_start(blk):
    return jnp.clip(blk * NA_ROWS_PER_BLOCK - NA_KR // 2, 0, GRID_H - NA_SLAB_ROWS)


def _na_variant(blk):
    return jnp.where(blk == 0, 0, jnp.where(blk == NA_BLOCKS - 1, 2, 1))


def _with_ones(v):
    return jnp.concatenate([v, jnp.ones((BF16_SUBLANES, v.shape[1]), v.dtype)], axis=0)


def _na_kernel(qt_ref, k_ref, vt_ref, kc_ref, vct_ref, *rest):
    bias_refs, o_ref = rest[:-1], rest[-1]
    row_half = lax.broadcasted_iota(jnp.int32, (LANES, NA_TQ), 0) // NA_DIM
    starts = [pl.multiple_of(_na_slab_start(pl.program_id(2) * NA_BLOCKS_PER_STEP + n) * GRID_W, NA_TQ)
              for n in range(NA_BLOCKS_PER_STEP)]
    heads = [(n, h) for n in range(NA_BLOCKS_PER_STEP) for h in range(NA_HEADS_PER_STEP)]
    n_keys = NA_SLAB + kc_ref.shape[1]
    items = [(n, h, c) for n, h in heads for c in range(0, n_keys, NA_CHUNK)]

    def scores(head):
        n, h = head
        pair, hh = divmod(h, 2)
        cols = slice(pair * LANES, (pair + 1) * LANES)
        k_all = jnp.concatenate([k_ref[0, pl.ds(starts[n], NA_SLAB), cols], kc_ref[0, :, cols]], axis=0)
        qt = qt_ref[0, cols, n * NA_TQ:(n + 1) * NA_TQ]
        qm = jnp.where(row_half == hh, qt, jnp.zeros_like(qt))
        s = _dot(k_all, qm).astype(BF16)
        s = jnp.concatenate([s[:NA_SLAB] + bias_refs[n][0, h], s[NA_SLAB:]], axis=0)
        return [s[c:c + NA_CHUNK] for c in range(0, n_keys, NA_CHUNK)]

    m, acc, outs = {}, {}, []
    pending = []
    gi = 0
    for n, h, c in items:
        while len(pending) <= NA_LOOKAHEAD and gi < len(heads):
            pending.extend(scores(heads[gi]))
            gi += 1
        s = pending.pop(0)
        mc = jnp.max(s, axis=0, keepdims=True)
        m_new = mc if c == 0 else jnp.maximum(m[n, h], mc)
        p = jnp.exp2(s - m_new)
        rows = slice(h * NA_DIM, (h + 1) * NA_DIM)
        v_all = jnp.concatenate([vt_ref[0, rows, pl.ds(starts[n], NA_SLAB)], vct_ref[0, rows, :]], axis=1)
        pv = _dot(_with_ones(v_all[:, c:c + NA_CHUNK]), p)
        if c == 0:
            acc[n, h] = pv
        else:
            acc[n, h] = acc[n, h] * jnp.exp2(m[n, h].astype(F32) - m_new.astype(F32)) + pv
        m[n, h] = m_new
        if c + NA_CHUNK >= n_keys:
            a = acc.pop((n, h))
            outs.append(a[:NA_DIM] / a[NA_DIM:NA_DIM + 1])
            if h == NA_HEADS_PER_STEP - 1:
                o_ref[0, n * NA_TQ:(n + 1) * NA_TQ, :] = jnp.concatenate(outs, axis=0).T.astype(BF16)
                outs = []


def _na_call(qbt, kb, vbt, kcb, vcbt, bias):
    b, s, _ = kb.shape
    t = kcb.shape[1]
    wide = NA_HEADS_PER_STEP * NA_DIM
    step_q = NA_BLOCKS_PER_STEP * NA_TQ

    def bias_spec(n):
        return pl.BlockSpec((1, NA_HEADS_PER_STEP, NA_SLAB, NA_TQ),
                            lambda g, i, j: (_na_variant(j * NA_BLOCKS_PER_STEP + n), g, 0, 0))

    return pl.pallas_call(
        _na_kernel,
        grid=(HEADS // NA_HEADS_PER_STEP, b, NA_BLOCKS // NA_BLOCKS_PER_STEP),
        in_specs=[
            pl.BlockSpec((1, wide, step_q), lambda g, i, j: (i, g, j)),
            pl.BlockSpec((1, s, wide), lambda g, i, j: (i, 0, g)),
            pl.BlockSpec((1, wide, s), lambda g, i, j: (i, g, 0)),
            pl.BlockSpec((1, t, wide), lambda g, i, j: (i, 0, g)),
            pl.BlockSpec((1, wide, t), lambda g, i, j: (i, g, 0)),
        ] + [bias_spec(n) for n in range(NA_BLOCKS_PER_STEP)],
        out_specs=pl.BlockSpec((1, step_q, wide), lambda g, i, j: (i, j, g)),
        out_shape=jax.ShapeDtypeStruct((b, s, HEADS * NA_DIM), BF16),
        compiler_params=pltpu.CompilerParams(vmem_limit_bytes=VMEM_LIMIT),
        name="na_attention",
    )(qbt, kb, vbt, kcb, vcbt, *([bias] * NA_BLOCKS_PER_STEP))


def _na_bias_kernel(rpb_ref, o_ref):
    rp = pltpu.roll(rpb_ref[0] * LOG2_E, LANES - (NA_KC - 1), axis=1)
    shape = (GRID_W, LANES)
    wk = lax.broadcasted_iota(jnp.int32, shape, 0)
    lane = lax.broadcasted_iota(jnp.int32, shape, 1)
    c_start = jnp.clip(lane - NA_KC // 2, 0, GRID_W - NA_KC)
    left = lane < GRID_W
    mask_l = jnp.where(left, NA_MASK, 0.0).astype(F32)
    mask_r = jnp.where(left, 0.0, NA_MASK).astype(F32)
    tiles_l, tiles_r = [], []
    for d in range(2 * NA_KR - 1):
        row = jnp.broadcast_to(rp[d:d + 1, :], shape)
        t = pltpu.roll(row, 0, axis=1, stride=1, stride_axis=0)
        t = jnp.where(wk >= c_start, jnp.where(wk < c_start + NA_KC, t, NA_MASK), NA_MASK)
        t = jnp.where(left, t, 0.0)
        tiles_l.append(t)
        tiles_r.append(pltpu.roll(t, GRID_W, axis=1))
    for n, blk in enumerate((0, 1, NA_BLOCKS - 1)):
        sb = min(max(blk * NA_ROWS_PER_BLOCK - NA_KR // 2, 0), GRID_H - NA_SLAB_ROWS)
        for j in range(NA_SLAB_ROWS):
            kr = sb + j

            def pick(i, tiles, masked):
                r = blk * NA_ROWS_PER_BLOCK + i
                r_start = min(max(r - NA_KR // 2, 0), GRID_H - NA_KR)
                if r_start <= kr < r_start + NA_KR:
                    return tiles[kr - r + NA_KR - 1]
                return masked

            for ii in range(NA_ROWS_PER_BLOCK // 2):
                o_ref[n, 0, j * GRID_W:(j + 1) * GRID_W, ii * LANES:(ii + 1) * LANES] = (
                    pick(2 * ii, tiles_l, mask_l) + pick(2 * ii + 1, tiles_r, mask_r)).astype(BF16)


def _na_bias_tables(rpb):
    rpb_p = jnp.pad(rpb[:, :, ::-1], ((0, 0), (0, 1), (0, LANES - (2 * NA_KC - 1))))
    return pl.pallas_call(
        _na_bias_kernel,
        grid=(HEADS,),
        in_specs=[pl.BlockSpec((1, 2 * NA_KR, LANES), lambda h: (h, 0, 0))],
        out_specs=pl.BlockSpec((3, 1, NA_SLAB, NA_TQ), lambda h: (0, h, 0, 0)),
        out_shape=jax.ShapeDtypeStruct((3, HEADS, NA_SLAB, NA_TQ), BF16),
        name="na_bias_table",
    )(rpb_p)


def _out_kernel(oa_ref, ob_ref, x_ref, shift_ref, scale_ref, gate_ref, ng_ref, wg_ref,
                woa_ref, wob_ref, wout_ref, fg_ref, o_ref):
    subs = [slice(i, i + OUT_SUB) for i in range(0, o_ref.shape[1], OUT_SUB)]

    def gates(sl):
        h = _rms(x_ref[0, sl], ng_ref[...]) * (1.0 + scale_ref[0, 0]) + shift_ref[0, 0]
        hb = h.astype(BF16)
        proj = lambda c: _dot_nt(hb, wg_ref[c[0]:c[1], :])
        za = proj(_G_ZA)
        zb = proj(_G_ZB)
        return ((za * jax.nn.sigmoid(za)).astype(BF16), (zb * jax.nn.sigmoid(zb)).astype(BF16),
                jax.nn.sigmoid(proj(_G_GA)).astype(BF16), jax.nn.sigmoid(proj(_G_GB)).astype(BF16))

    def branches(sl, g):
        sza, szb, sga, sgb = g
        ya = _dot(oa_ref[0, sl] * sza, woa_ref[...])
        yb = _dot(ob_ref[0, sl] * szb, wob_ref[...])
        return (sga.astype(F32) * ya + sgb.astype(F32) * yb).astype(BF16)

    def finish(sl, mix):
        y = _dot(mix, wout_ref[...])
        r = x_ref[0, sl] + gate_ref[0, 0] * y
        o_ref[0, sl] = _rms(r, fg_ref[...])

    order = sorted([(3 * i + off, stage, i) for i in range(len(subs))
                    for stage, off in (("gates", 0), ("branches", 4), ("finish", 8))])
    g, mix = {}, {}
    for _, stage, i in order:
        if stage == "gates":
            g[i] = gates(subs[i])
        elif stage == "branches":
            mix[i] = branches(subs[i], g.pop(i))
        else:
            finish(subs[i], mix.pop(i))


OUT_SUB = 256


def _out_call(oa, ob, x, mod, norm_g, w_gate, w_oa, w_ob, w_out, final_g, tm=1024):
    b, s, d = x.shape
    tok3 = lambda i, j: (i, j, 0)
    half = HEADS * MLA_V

    def const(shape):
        return pl.BlockSpec(shape, lambda i, j: (0, 0), pipeline_mode=pl.Buffered(1))

    return pl.pallas_call(
        _out_kernel,
        grid=(b, s // tm),
        in_specs=[
            pl.BlockSpec((1, tm, half), tok3),
            pl.BlockSpec((1, tm, half), tok3),
            pl.BlockSpec((1, tm, d), tok3),
            _mod_spec(MOD_SHIFT),
            _mod_spec(MOD_SCALE),
            _mod_spec(MOD_GATE),
            const((1, d)),
            const((w_gate.shape[0], d)),
            const((half, d)),
            const((half, d)),
            const((d, d)),
            const((1, d)),
        ],
        out_specs=pl.BlockSpec((1, tm, d), tok3),
        out_shape=jax.ShapeDtypeStruct((b, s, d), F32),
        compiler_params=pltpu.CompilerParams(vmem_limit_bytes=VMEM_LIMIT),
        name="gated_out",
    )(oa, ob, x, mod, mod, mod, norm_g, w_gate, w_oa, w_ob, w_out, final_g)


def _rope_partner(w, axis):
    q = MLA_ROPE // 4
    parts = [lax.slice_in_dim(w, n * q, (n + 1) * q, axis=axis) for n in range(4)]
    return jnp.concatenate([-parts[1], parts[0], -parts[3], parts[2]], axis=axis)


def _rope_tables():
    t = np.arange(SEQ)
    half = MLA_ROPE // 4
    freqs = (ROPE_THETA ** (-np.arange(half, dtype=np.float32) / half)).astype(np.float32)
    ang_r = (t // GRID_W).astype(np.float32)[:, None] * freqs[None, :]
    ang_c = (t % GRID_W).astype(np.float32)[:, None] * freqs[None, :]
    cos = np.concatenate([np.cos(ang_r)] * 2 + [np.cos(ang_c)] * 2, axis=-1).astype(np.float32)
    sin = np.concatenate([np.sin(ang_r)] * 2 + [np.sin(ang_c)] * 2, axis=-1).astype(np.float32)
    return cos, sin


def _np_lane_group(nope, rope):
    pad = np.zeros((rope.shape[0], LANES - MLA_NOPE - MLA_ROPE), np.float32)
    return np.concatenate([nope, rope, pad], axis=-1).astype(np.float32)


def _rope_rows(r):
    d = r.shape[1]
    return jnp.concatenate([jnp.zeros((MLA_NOPE, d), F32), r,
                            jnp.zeros((LANES - MLA_NOPE - MLA_ROPE, d), F32)], axis=0)


def kernel(x, c, ctx, c_ctx, w_mod, b_mod, norm_g, w_in, g_cq, w_uq, g_ckv, w_ukv, rpb,
           w_oa, w_ob, w_out, final_g):
    b, s, d = x.shape
    depth = w_mod.shape[0]
    assert depth == 1 and s == SEQ and d == D_MODEL
    mla_scale = (MLA_NOPE + MLA_ROPE) ** -0.5 * LOG2_E

    wt = jnp.transpose(w_in[0])
    w_all = wt.astype(BF16)
    w_kr = wt[KV_LORA:KV_LORA + MLA_ROPE]
    w_kr2 = jnp.concatenate([_rope_rows(w_kr), _rope_rows(_rope_partner(w_kr, 0))], axis=0).astype(BF16)

    wkv3 = w_ukv[0].reshape(KV_LORA, HEADS, MLA_NOPE + MLA_V)
    wk = jnp.concatenate([wkv3[:, :, :MLA_NOPE], jnp.zeros((KV_LORA, HEADS, LANES - MLA_NOPE), F32)],
                         axis=-1).reshape(KV_LORA, HEADS * LANES).astype(BF16)
    wv = wkv3[:, :, MLA_NOPE:].reshape(KV_LORA, HEADS * MLA_V).astype(BF16).T
    wq3 = w_uq[0].reshape(Q_LORA, HEADS, MLA_NOPE + MLA_ROPE)
    zq = jnp.zeros((Q_LORA, HEADS, LANES - MLA_NOPE - MLA_ROPE), F32)
    wq = jnp.concatenate([wq3, zq], axis=-1).reshape(Q_LORA, HEADS * LANES).astype(BF16).T
    wqr = jnp.concatenate([jnp.zeros((Q_LORA, HEADS, MLA_NOPE), F32),
                           _rope_partner(wq3[:, :, MLA_NOPE:], 2), zq],
                          axis=-1).reshape(Q_LORA, HEADS * LANES).astype(BF16).T

    cos, sin = _rope_tables()
    ones_s = np.ones((s, MLA_NOPE), np.float32)
    zeros_s = np.zeros((s, MLA_NOPE), np.float32)
    cq_tab = jnp.asarray(np.ascontiguousarray((_np_lane_group(ones_s, cos) * np.float32(mla_scale)).T))
    sq_tab = jnp.asarray(np.ascontiguousarray((_np_lane_group(zeros_s, sin) * np.float32(mla_scale)).T))
    ck_tab = jnp.asarray(_np_lane_group(zeros_s, cos))
    sk_tab = jnp.asarray(_np_lane_group(zeros_s, sin))
    zeros_c = np.zeros((CTX_LEN, MLA_NOPE), np.float32)
    ck_ctx = jnp.asarray(_np_lane_group(zeros_c, np.ones((CTX_LEN, MLA_ROPE), np.float32)))
    sk_ctx = jnp.zeros((CTX_LEN, LANES), F32)

    cs = jnp.concatenate([c, c_ctx[None, :], jnp.zeros((8 - b - 1, d), F32)], axis=0)
    mod = _mod_call(cs, w_mod[0], b_mod[0][None, :])

    ng = norm_g[0][None, :]
    gkv = g_ckv[0][None, :]
    k, v, kb, vb, q, qb = _proj_call(
        x, mod, None, ng, w_all, w_kr2, gkv, wk, wv, ck_tab, sk_tab,
        q_side=(g_cq[0][None, :], wq, wqr, cq_tab, sq_tab))
    kc, vc, kcb, vcb = _proj_call(
        ctx, mod, b, ng, w_all, w_kr2, gkv, wk, wv,
        ck_ctx, sk_ctx)

    oa = _mla_call(q, k, kc, v, vc)
    ob = _na_call(qb, kb, vb, kcb, vcb, _na_bias_tables(rpb[0]))

    return _out_call(oa, ob, x, mod, ng, w_all,
                     w_oa[0].astype(BF16), w_ob[0].astype(BF16), w_out[0].astype(BF16),
                     final_g[None, :])
```

```python
import functools

import jax
import jax.numpy as jnp
import numpy as np
from jax import lax
from jax.experimental import pallas as pl
from jax.experimental.pallas import tpu as pltpu

D_MODEL = 1024
SEQ = 4096
GRID_W = 64
GRID_H = SEQ // GRID_W
CTX_LEN = 256
HEADS = 8
MLA_NOPE = 64
MLA_ROPE = 32
MLA_V = 64
Q_LORA = 256
KV_LORA = 128
NA_DIM = 64
NA_KR = 8
NA_KC = 16
ROPE_THETA = 10000.0
EPS = 1e-6

LANES = 128
HEAD_PAIRS = HEADS // 2
VMEM_LIMIT = 56 * 1024 * 1024

_C_CKV = (0, 128)
_C_KB = (160, 672)
_C_CQ = (1184, 1440)
_G_ZA = (1440, 1952)
_G_ZB = (2464, 2976)
_G_GA = (2976, 4000)
_G_GB = (4000, 5024)
_R_VB = (672, 1184)
_R_QB = (1952, 2464)
_C_KR = (0, 128)
_C_KRR = (128, 256)

NA_ROWS_PER_BLOCK = 4
NA_BLOCKS = GRID_H // NA_ROWS_PER_BLOCK
NA_TQ = NA_ROWS_PER_BLOCK * GRID_W
NA_SLAB_ROWS = 12
NA_SLAB = NA_SLAB_ROWS * GRID_W
NA_MASK = -1e30
LOG2_E = 1.4426950408889634
NA_QSCALE = NA_DIM ** -0.5 * LOG2_E

F32 = jnp.float32
BF16 = jnp.bfloat16


def _dot(a, b):
    return jnp.dot(a, b, preferred_element_type=F32)


def _dot_nt(a, b):
    return lax.dot_general(a, b, (((1,), (1,)), ((), ())), preferred_element_type=F32)


def _rms(x, g):
    return x * lax.rsqrt(jnp.mean(x * x, axis=-1, keepdims=True) + EPS) * g


def _mod_kernel(c_ref, w_ref, b_ref, o_ref):
    c = c_ref[...]
    sc = c * jax.nn.sigmoid(c)
    o_ref[0, :, 0, :] = _dot(sc.astype(BF16), w_ref[...].astype(BF16)) + b_ref[...]


def _mod_call(cs, w_mod, b_mod):
    n = w_mod.shape[1]
    tn = 1024
    return pl.pallas_call(
        _mod_kernel,
        grid=(n // tn,),
        in_specs=[
            pl.BlockSpec((8, D_MODEL), lambda j: (0, 0)),
            pl.BlockSpec((D_MODEL, tn), lambda j: (0, j)),
            pl.BlockSpec((1, tn), lambda j: (0, j)),
        ],
        out_specs=pl.BlockSpec((1, 8, 1, tn), lambda j: (j, 0, 0, 0)),
        out_shape=jax.ShapeDtypeStruct((n // tn, 8, 1, tn), F32),
        name="adaln_mod",
    )(cs, w_mod, b_mod)


def _proj_kernel(with_q, x_ref, shift_ref, scale_ref, ng_ref, w_ref, wkr_ref, gkv_ref, wk_ref, wv_ref,
                 ck_ref, sk_ref, *rest):
    if with_q:
        (gq_ref, wq_ref, wqr_ref, cq_ref, sq_ref,
         k_ref, v_ref, kb_ref, vb_ref, q_ref, qb_ref) = rest
    else:
        k_ref, v_ref, kb_ref, vb_ref = rest

    x = x_ref[0]
    h = _rms(x, ng_ref[...]) * (1.0 + scale_ref[0, 0]) + shift_ref[0, 0]
    hb = h.astype(BF16)

    def proj(c):
        return _dot_nt(hb, w_ref[c[0]:c[1], :])

    def proj_t(r):
        return _dot_nt(w_ref[r[0]:r[1], :], hb)

    def proj_kr(c):
        return _dot_nt(hb, wkr_ref[c[0]:c[1], :])

    ckvn = _rms(proj(_C_CKV), gkv_ref[...]).astype(BF16)
    if with_q:
        cqn = _rms(proj(_C_CQ), gq_ref[...]).astype(BF16)
    krope = proj_kr(_C_KR) * ck_ref[...] + proj_kr(_C_KRR) * sk_ref[...]
    kb_ref[0] = proj(_C_KB).astype(BF16)
    vb_ref[0] = proj_t(_R_VB).astype(BF16)
    if with_q:
        qb_ref[0] = (proj_t(_R_QB) * NA_QSCALE).astype(BF16)

    knope = _dot(ckvn, wk_ref[...])
    for hd in range(HEADS):
        sl = slice(hd * LANES, (hd + 1) * LANES)
        k_ref[0, :, sl] = (knope[:, sl] + krope).astype(BF16)
    v_ref[0] = _dot_nt(wv_ref[...], ckvn).astype(BF16)
    if with_q:
        q1 = _dot_nt(wq_ref[...], cqn)
        q2 = _dot_nt(wqr_ref[...], cqn)
        cq = cq_ref[...]
        sq = sq_ref[...]
        for hd in range(HEADS):
            sl = slice(hd * LANES, (hd + 1) * LANES)
            q_ref[0, sl, :] = (q1[sl] * cq + q2[sl] * sq).astype(BF16)


def _mod_spec(which, row=None):
    d = D_MODEL
    if row is None:
        return pl.BlockSpec((1, 1, 1, d), lambda i, j: (which, i, 0, 0))
    return pl.BlockSpec((1, 1, 1, d), lambda i, j: (which, row, 0, 0))


MOD_SHIFT, MOD_SCALE, MOD_GATE = 0, 1, 2


def _proj_call(x, mod, mod_row, norm_g, w_all, w_kr, g_ckv, wk, wv, ck, sk,
               q_side=None, tm=512):
    b, s, d = x.shape
    tm = min(tm, s)
    with_q = q_side is not None
    n_all = w_all.shape[0]
    n_kr = w_kr.shape[0]
    const2 = lambda i, j: (0, 0)
    tok3 = lambda i, j: (i, j, 0)
    tab = lambda i, j: (j, 0)

    in_specs = [
        pl.BlockSpec((1, tm, d), tok3),
        _mod_spec(MOD_SHIFT, mod_row),
        _mod_spec(MOD_SCALE, mod_row),
        pl.BlockSpec((1, d), const2),
        pl.BlockSpec((n_all, d), const2, pipeline_mode=pl.Buffered(1)),
        pl.BlockSpec((n_kr, d), const2),
        pl.BlockSpec((1, KV_LORA), const2),
        pl.BlockSpec((KV_LORA, HEADS * LANES), const2),
        pl.BlockSpec((HEADS * MLA_V, KV_LORA), const2),
        pl.BlockSpec((tm, LANES), tab),
        pl.BlockSpec((tm, LANES), tab),
    ]
    args = [x, mod, mod, norm_g, w_all, w_kr, g_ckv, wk, wv, ck, sk]
    tok_major = lambda w: (pl.BlockSpec((1, tm, w), tok3), jax.ShapeDtypeStruct((b, s, w), BF16))
    feat_major = lambda w: (pl.BlockSpec((1, w, tm), lambda i, j: (i, 0, j)),
                            jax.ShapeDtypeStruct((b, w, s), BF16))
    outs = [tok_major(HEADS * LANES), feat_major(HEADS * MLA_V), tok_major(HEADS * NA_DIM),
            feat_major(HEADS * NA_DIM)]
    if with_q:
        g_cq, wq, wqr, cq, sq = q_side
        in_specs += [
            pl.BlockSpec((1, Q_LORA), const2),
            pl.BlockSpec((HEADS * LANES, Q_LORA), const2),
            pl.BlockSpec((HEADS * LANES, Q_LORA), const2),
            pl.BlockSpec((LANES, tm), lambda i, j: (0, j)),
            pl.BlockSpec((LANES, tm), lambda i, j: (0, j)),
        ]
        args += [g_cq, wq, wqr, cq, sq]
        outs += [feat_major(HEADS * LANES), feat_major(HEADS * NA_DIM)]
    out_specs = [o[0] for o in outs]
    out_shape = [o[1] for o in outs]
    return pl.pallas_call(
        functools.partial(_proj_kernel, with_q),
        grid=(b, s // tm),
        in_specs=in_specs,
        out_specs=out_specs,
        out_shape=out_shape,
        compiler_params=pltpu.CompilerParams(vmem_limit_bytes=VMEM_LIMIT),
        name="in_proj_q" if with_q else "in_proj_ctx",
    )(*args)


BF16_SUBLANES = 16
MLA_TQ = 256
MLA_TILES = 8
MLA_CHUNK = 256
MLA_QK_CHUNKS = 2
MLA_PV_DELAY = 1
MLA_LOOKAHEAD = 6


def _mla_kernel(qt_ref, k_ref, kc_ref, vt_ref, vct_ref, o_ref):
    s_len = k_ref.shape[1]
    streams = [(t, hh) for t in range(MLA_TILES) for hh in range(2)]
    qts = {(t, hh): qt_ref[0, hh * LANES:(hh + 1) * LANES, t * MLA_TQ:(t + 1) * MLA_TQ] for t, hh in streams}
    qk_keys = MLA_QK_CHUNKS * MLA_CHUNK
    groups = [(st, k_ref, vt_ref, c * qk_keys, qk_keys) for c in range(s_len // qk_keys) for st in streams]
    groups += [(st, kc_ref, vct_ref, 0, kc_ref.shape[1]) for st in streams]
    items = [(st, vr, off + i, min(MLA_CHUNK, n - i))
             for st, _, vr, off, n in groups for i in range(0, n, MLA_CHUNK)]

    def group_scores(g):
        st, kr, _, off, n = g
        hh = st[1]
        big = _dot(kr[0, off:off + n, hh * LANES:(hh + 1) * LANES], qts[st]).astype(BF16)
        return [big[i:i + MLA_CHUNK] for i in range(0, n, MLA_CHUNK)]

    m = {st: None for st in streams}
    acc = {st: None for st in streams}
    pending = []
    gi = 0
    last_item = {st[0]: idx for idx, (st, _, _, _) in enumerate(items)}

    def weigh(idx, st, vr, off, n, p, alpha):
        hh = st[1]
        v_ones = jnp.concatenate([vr[0, hh * MLA_V:(hh + 1) * MLA_V, off:off + n],
                                  jnp.ones((BF16_SUBLANES, n), BF16)], axis=0)
        pv = _dot(v_ones, p)
        acc[st] = pv if alpha is None else acc[st] * alpha + pv
        t = st[0]
        if idx == last_item[t]:
            outs = [acc[(t, h2)][:MLA_V] / acc[(t, h2)][MLA_V:MLA_V + 1] for h2 in range(2)]
            o_ref[0, t * MLA_TQ:(t + 1) * MLA_TQ, :] = jnp.concatenate(outs, axis=0).T.astype(BF16)

    delayed = []
    for idx, (st, vr, off, n) in enumerate(items):
        while len(pending) <= MLA_LOOKAHEAD and gi < len(groups):
            pending.extend(group_scores(groups[gi]))
            gi += 1
        s = pending.pop(0)
        mc = jnp.max(s, axis=0, keepdims=True)
        m_new = mc if m[st] is None else jnp.maximum(m[st], mc)
        p = jnp.exp2(s - m_new)
        alpha = None if m[st] is None else jnp.exp2(m[st].astype(F32) - m_new.astype(F32))
        m[st] = m_new
        if len(delayed) == MLA_PV_DELAY:
            weigh(*delayed.pop(0))
        delayed.append((idx, st, vr, off, n, p, alpha))
    for d in delayed:
        weigh(*d)


def _mla_call(qt, k, kc, vt, vct):
    b, _, s = qt.shape
    t = kc.shape[1]
    step_q = MLA_TILES * MLA_TQ
    return pl.pallas_call(
        _mla_kernel,
        grid=(b, HEAD_PAIRS, s // step_q),
        in_specs=[
            pl.BlockSpec((1, 2 * LANES, step_q), lambda i, p, j: (i, p, j)),
            pl.BlockSpec((1, s, 2 * LANES), lambda i, p, j: (i, 0, p)),
            pl.BlockSpec((1, t, 2 * LANES), lambda i, p, j: (i, 0, p)),
            pl.BlockSpec((1, 2 * MLA_V, s), lambda i, p, j: (i, p, 0)),
            pl.BlockSpec((1, 2 * MLA_V, t), lambda i, p, j: (i, p, 0)),
        ],
        out_specs=pl.BlockSpec((1, step_q, LANES), lambda i, p, j: (i, j, p)),
        out_shape=jax.ShapeDtypeStruct((b, s, HEADS * MLA_V), BF16),
        compiler_params=pltpu.CompilerParams(vmem_limit_bytes=VMEM_LIMIT),
        name="mla_attention",
    )(qt, k, kc, vt, vct)


NA_HEADS_PER_STEP = 8
NA_BLOCKS_PER_STEP = 8
NA_CHUNK = 256
NA_LOOKAHEAD = 13


def _na_slab_start(blk):
    return jnp.clip(blk * NA_ROWS_PER_BLOCK - NA_KR // 2, 0, GRID_H - NA_SLAB_ROWS)


def _na_variant(blk):
    return jnp.where(blk == 0, 0, jnp.where(blk == NA_BLOCKS - 1, 2, 1))


def _with_ones(v):
    return jnp.concatenate([v, jnp.ones((BF16_SUBLANES, v.shape[1]), v.dtype)], axis=0)


def _na_kernel(qt_ref, k_ref, vt_ref, kc_ref, vct_ref, *rest):
    bias_refs, o_ref = rest[:-1], rest[-1]
    row_half = lax.broadcasted_iota(jnp.int32, (LANES, NA_TQ), 0) // NA_DIM
    starts = [pl.multiple_of(_na_slab_start(pl.program_id(2) * NA_BLOCKS_PER_STEP + n) * GRID_W, NA_TQ)
              for n in range(NA_BLOCKS_PER_STEP)]
    heads = [(n, h) for n in range(NA_BLOCKS_PER_STEP) for h in range(NA_HEADS_PER_STEP)]
    n_keys = NA_SLAB + kc_ref.shape[1]
    items = [(n, h, c) for n, h in heads for c in range(0, n_keys, NA_CHUNK)]

    def scores(head):
        n, h = head
        pair, hh = divmod(h, 2)
        cols = slice(pair * LANES, (pair + 1) * LANES)
        k_all = jnp.concatenate([k_ref[0, pl.ds(starts[n], NA_SLAB), cols], kc_ref[0, :, cols]], axis=0)
        qt = qt_ref[0, cols, n * NA_TQ:(n + 1) * NA_TQ]
        qm = jnp.where(row_half == hh, qt, jnp.zeros_like(qt))
        s = _dot(k_all, qm).astype(BF16)
        bias_ref = bias_refs[0] if n == 0 else bias_refs[2] if n == NA_BLOCKS_PER_STEP - 1 else bias_refs[1]
        s = jnp.concatenate([s[:NA_SLAB] + bias_ref[0, h], s[NA_SLAB:]], axis=0)
        return [s[c:c + NA_CHUNK] for c in range(0, n_keys, NA_CHUNK)]

    m, acc, outs = {}, {}, []
    pending = []
    gi = 0
    for n, h, c in items:
        while len(pending) <= NA_LOOKAHEAD and gi < len(heads):
            pending.extend(scores(heads[gi]))
            gi += 1
        s = pending.pop(0)
        mc = jnp.max(s, axis=0, keepdims=True)
        m_new = mc if c == 0 else jnp.maximum(m[n, h], mc)
        p = jnp.exp2(s - m_new)
        rows = slice(h * NA_DIM, (h + 1) * NA_DIM)
        v_all = jnp.concatenate([vt_ref[0, rows, pl.ds(starts[n], NA_SLAB)], vct_ref[0, rows, :]], axis=1)
        pv = _dot(_with_ones(v_all[:, c:c + NA_CHUNK]), p)
        if c == 0:
            acc[n, h] = pv
        else:
            acc[n, h] = acc[n, h] * jnp.exp2(m[n, h].astype(F32) - m_new.astype(F32)) + pv
        m[n, h] = m_new
        if c + NA_CHUNK >= n_keys:
            a = acc.pop((n, h))
            outs.append(a[:NA_DIM] / a[NA_DIM:NA_DIM + 1])
            if h == NA_HEADS_PER_STEP - 1:
                o_ref[0, n * NA_TQ:(n + 1) * NA_TQ, :] = jnp.concatenate(outs, axis=0).T.astype(BF16)
                outs = []


def _na_call(qbt, kb, vbt, kcb, vcbt, bias):
    b, s, _ = kb.shape
    t = kcb.shape[1]
    wide = NA_HEADS_PER_STEP * NA_DIM
    step_q = NA_BLOCKS_PER_STEP * NA_TQ

    def bias_spec(n):
        return pl.BlockSpec((1, NA_HEADS_PER_STEP, NA_SLAB, NA_TQ),
                            lambda g, i, j: (_na_variant(j * NA_BLOCKS_PER_STEP + n), g, 0, 0))

    return pl.pallas_call(
        _na_kernel,
        grid=(HEADS // NA_HEADS_PER_STEP, b, NA_BLOCKS // NA_BLOCKS_PER_STEP),
        in_specs=[
            pl.BlockSpec((1, wide, step_q), lambda g, i, j: (i, g, j)),
            pl.BlockSpec((1, s, wide), lambda g, i, j: (i, 0, g)),
            pl.BlockSpec((1, wide, s), lambda g, i, j: (i, g, 0)),
            pl.BlockSpec((1, t, wide), lambda g, i, j: (i, 0, g)),
            pl.BlockSpec((1, wide, t), lambda g, i, j: (i, g, 0)),
        ] + [bias_spec(0), bias_spec(1), bias_spec(NA_BLOCKS_PER_STEP - 1)],
        out_specs=pl.BlockSpec((1, step_q, wide), lambda g, i, j: (i, j, g)),
        out_shape=jax.ShapeDtypeStruct((b, s, HEADS * NA_DIM), BF16),
        compiler_params=pltpu.CompilerParams(vmem_limit_bytes=VMEM_LIMIT),
        name="na_attention",
    )(qbt, kb, vbt, kcb, vcbt, bias, bias, bias)


def _na_bias_kernel(rpb_ref, o_ref):
    rp = pltpu.roll(rpb_ref[0] * LOG2_E, LANES - (NA_KC - 1), axis=1)
    shape = (GRID_W, LANES)
    wk = lax.broadcasted_iota(jnp.int32, shape, 0)
    lane = lax.broadcasted_iota(jnp.int32, shape, 1)
    c_start = jnp.clip(lane - NA_KC // 2, 0, GRID_W - NA_KC)
    left = lane < GRID_W
    mask_l = jnp.where(left, NA_MASK, 0.0).astype(F32)
    mask_r = jnp.where(left, 0.0, NA_MASK).astype(F32)
    tiles_l, tiles_r = [], []
    for d in range(2 * NA_KR - 1):
        row = jnp.broadcast_to(rp[d:d + 1, :], shape)
        t = pltpu.roll(row, 0, axis=1, stride=1, stride_axis=0)
        t = jnp.where(wk >= c_start, jnp.where(wk < c_start + NA_KC, t, NA_MASK), NA_MASK)
        t = jnp.where(left, t, 0.0)
        tiles_l.append(t)
        tiles_r.append(pltpu.roll(t, GRID_W, axis=1))
    for n, blk in enumerate((0, 1, NA_BLOCKS - 1)):
        sb = min(max(blk * NA_ROWS_PER_BLOCK - NA_KR // 2, 0), GRID_H - NA_SLAB_ROWS)
        for j in range(NA_SLAB_ROWS):
            kr = sb + j

            def pick(i, tiles, masked):
                r = blk * NA_ROWS_PER_BLOCK + i
                r_start = min(max(r - NA_KR // 2, 0), GRID_H - NA_KR)
                if r_start <= kr < r_start + NA_KR:
                    return tiles[kr - r + NA_KR - 1]
                return masked

            for ii in range(NA_ROWS_PER_BLOCK // 2):
                o_ref[n, 0, j * GRID_W:(j + 1) * GRID_W, ii * LANES:(ii + 1) * LANES] = (
                    pick(2 * ii, tiles_l, mask_l) + pick(2 * ii + 1, tiles_r, mask_r)).astype(BF16)


def _na_bias_tables(rpb):
    rpb_p = jnp.pad(rpb[:, :, ::-1], ((0, 0), (0, 1), (0, LANES - (2 * NA_KC - 1))))
    return pl.pallas_call(
        _na_bias_kernel,
        grid=(HEADS,),
        in_specs=[pl.BlockSpec((1, 2 * NA_KR, LANES), lambda h: (h, 0, 0))],
        out_specs=pl.BlockSpec((3, 1, NA_SLAB, NA_TQ), lambda h: (0, h, 0, 0)),
        out_shape=jax.ShapeDtypeStruct((3, HEADS, NA_SLAB, NA_TQ), BF16),
        name="na_bias_table",
    )(rpb_p)


def _out_kernel(oa_ref, ob_ref, x_ref, shift_ref, scale_ref, gate_ref, ng_ref, wg_ref,
                woa_ref, wob_ref, wout_ref, fg_ref, o_ref):
    subs = [slice(i, i + OUT_SUB) for i in range(0, o_ref.shape[1], OUT_SUB)]

    def gates(sl):
        h = _rms(x_ref[0, sl], ng_ref[...]) * (1.0 + scale_ref[0, 0]) + shift_ref[0, 0]
        hb = h.astype(BF16)
        proj = lambda c: _dot_nt(hb, wg_ref[c[0]:c[1], :])
        za = proj(_G_ZA)
        zb = proj(_G_ZB)
        return ((za * jax.nn.sigmoid(za)).astype(BF16), (zb * jax.nn.sigmoid(zb)).astype(BF16),
                jax.nn.sigmoid(proj(_G_GA)).astype(BF16), jax.nn.sigmoid(proj(_G_GB)).astype(BF16))

    def branches(sl, g):
        sza, szb, sga, sgb = g
        ya = _dot(oa_ref[0, sl] * sza, woa_ref[...])
        yb = _dot(ob_ref[0, sl] * szb, wob_ref[...])
        return (sga.astype(F32) * ya + sgb.astype(F32) * yb).astype(BF16)

    def finish(sl, mix):
        y = _dot(mix, wout_ref[...])
        r = x_ref[0, sl] + gate_ref[0, 0] * y
        o_ref[0, sl] = _rms(r, fg_ref[...])

    order = sorted([(3 * i + off, stage, i) for i in range(len(subs))
                    for stage, off in (("gates", 0), ("branches", 4), ("finish", 8))])
    g, mix = {}, {}
    for _, stage, i in order:
        if stage == "gates":
            g[i] = gates(subs[i])
        elif stage == "branches":
            mix[i] = branches(subs[i], g.pop(i))
        else:
            finish(subs[i], mix.pop(i))


OUT_SUB = 256


def _out_call(oa, ob, x, mod, norm_g, w_gate, w_oa, w_ob, w_out, final_g, tm=1024):
    b, s, d = x.shape
    tok3 = lambda i, j: (i, j, 0)
    half = HEADS * MLA_V

    def const(shape):
        return pl.BlockSpec(shape, lambda i, j: (0, 0), pipeline_mode=pl.Buffered(1))

    return pl.pallas_call(
        _out_kernel,
        grid=(b, s // tm),
        in_specs=[
            pl.BlockSpec((1, tm, half), tok3),
            pl.BlockSpec((1, tm, half), tok3),
            pl.BlockSpec((1, tm, d), tok3),
            _mod_spec(MOD_SHIFT),
            _mod_spec(MOD_SCALE),
            _mod_spec(MOD_GATE),
            const((1, d)),
            const((w_gate.shape[0], d)),
            const((half, d)),
            const((half, d)),
            const((d, d)),
            const((1, d)),
        ],
        out_specs=pl.BlockSpec((1, tm, d), tok3),
        out_shape=jax.ShapeDtypeStruct((b, s, d), F32),
        compiler_params=pltpu.CompilerParams(vmem_limit_bytes=VMEM_LIMIT),
        name="gated_out",
    )(oa, ob, x, mod, mod, mod, norm_g, w_gate, w_oa, w_ob, w_out, final_g)


def _rope_partner(w, axis):
    q = MLA_ROPE // 4
    parts = [lax.slice_in_dim(w, n * q, (n + 1) * q, axis=axis) for n in range(4)]
    return jnp.concatenate([-parts[1], parts[0], -parts[3], parts[2]], axis=axis)


def _rope_tables():
    t = np.arange(SEQ)
    half = MLA_ROPE // 4
    freqs = (ROPE_THETA ** (-np.arange(half, dtype=np.float32) / half)).astype(np.float32)
    ang_r = (t // GRID_W).astype(np.float32)[:, None] * freqs[None, :]
    ang_c = (t % GRID_W).astype(np.float32)[:, None] * freqs[None, :]
    cos = np.concatenate([np.cos(ang_r)] * 2 + [np.cos(ang_c)] * 2, axis=-1).astype(np.float32)
    sin = np.concatenate([np.sin(ang_r)] * 2 + [np.sin(ang_c)] * 2, axis=-1).astype(np.float32)
    return cos, sin


def _np_lane_group(nope, rope):
    pad = np.zeros((rope.shape[0], LANES - MLA_NOPE - MLA_ROPE), np.float32)
    return np.concatenate([nope, rope, pad], axis=-1).astype(np.float32)


def _rope_rows(r):
    d = r.shape[1]
    return jnp.concatenate([jnp.zeros((MLA_NOPE, d), F32), r,
                            jnp.zeros((LANES - MLA_NOPE - MLA_ROPE, d), F32)], axis=0)


def kernel(x, c, ctx, c_ctx, w_mod, b_mod, norm_g, w_in, g_cq, w_uq, g_ckv, w_ukv, rpb,
           w_oa, w_ob, w_out, final_g):
    b, s, d = x.shape
    depth = w_mod.shape[0]
    assert depth == 1 and s == SEQ and d == D_MODEL
    mla_scale = (MLA_NOPE + MLA_ROPE) ** -0.5 * LOG2_E

    wt = jnp.transpose(w_in[0])
    w_all = wt.astype(BF16)
    w_kr = wt[KV_LORA:KV_LORA + MLA_ROPE]
    w_kr2 = jnp.concatenate([_rope_rows(w_kr), _rope_rows(_rope_partner(w_kr, 0))], axis=0).astype(BF16)

    wkv3 = w_ukv[0].reshape(KV_LORA, HEADS, MLA_NOPE + MLA_V)
    wk = jnp.concatenate([wkv3[:, :, :MLA_NOPE], jnp.zeros((KV_LORA, HEADS, LANES - MLA_NOPE), F32)],
                         axis=-1).reshape(KV_LORA, HEADS * LANES).astype(BF16)
    wv = wkv3[:, :, MLA_NOPE:].reshape(KV_LORA, HEADS * MLA_V).astype(BF16).T
    wq3 = w_uq[0].reshape(Q_LORA, HEADS, MLA_NOPE + MLA_ROPE)
    zq = jnp.zeros((Q_LORA, HEADS, LANES - MLA_NOPE - MLA_ROPE), F32)
    wq = jnp.concatenate([wq3, zq], axis=-1).reshape(Q_LORA, HEADS * LANES).astype(BF16).T
    wqr = jnp.concatenate([jnp.zeros((Q_LORA, HEADS, MLA_NOPE), F32),
                           _rope_partner(wq3[:, :, MLA_NOPE:], 2), zq],
                          axis=-1).reshape(Q_LORA, HEADS * LANES).astype(BF16).T

    cos, sin = _rope_tables()
    ones_s = np.ones((s, MLA_NOPE), np.float32)
    zeros_s = np.zeros((s, MLA_NOPE), np.float32)
    cq_tab = jnp.asarray(np.ascontiguousarray((_np_lane_group(ones_s, cos) * np.float32(mla_scale)).T))
    sq_tab = jnp.asarray(np.ascontiguousarray((_np_lane_group(zeros_s, sin) * np.float32(mla_scale)).T))
    ck_tab = jnp.asarray(_np_lane_group(zeros_s, cos))
    sk_tab = jnp.asarray(_np_lane_group(zeros_s, sin))
    zeros_c = np.zeros((CTX_LEN, MLA_NOPE), np.float32)
    ck_ctx = jnp.asarray(_np_lane_group(zeros_c, np.ones((CTX_LEN, MLA_ROPE), np.float32)))
    sk_ctx = jnp.zeros((CTX_LEN, LANES), F32)

    cs = jnp.concatenate([c, c_ctx[None, :], jnp.zeros((8 - b - 1, d), F32)], axis=0)
    mod = _mod_call(cs, w_mod[0], b_mod[0][None, :])

    ng = norm_g[0][None, :]
    gkv = g_ckv[0][None, :]
    k, v, kb, vb, q, qb = _proj_call(
        x, mod, None, ng, w_all, w_kr2, gkv, wk, wv, ck_tab, sk_tab,
        q_side=(g_cq[0][None, :], wq, wqr, cq_tab, sq_tab))
    kc, vc, kcb, vcb = _proj_call(
        ctx, mod, b, ng, w_all, w_kr2, gkv, wk, wv,
        ck_ctx, sk_ctx)

    oa = _mla_call(q, k, kc, v, vc)
    ob = _na_call(qb, kb, vb, kcb, vcb, _na_bias_tables(rpb[0]))

    return _out_call(oa, ob, x, mod, ng, w_all,
                     w_oa[0].astype(BF16), w_ob[0].astype(BF16), w_out[0].astype(BF16),
                     final_g[None, :])
```

```python
import functools

import jax
import jax.numpy as jnp
import numpy as np
from jax import lax
from jax.experimental import pallas as pl
from jax.experimental.pallas import tpu as pltpu

D_MODEL = 1024
SEQ = 4096
GRID_W = 64
GRID_H = SEQ // GRID_W
CTX_LEN = 256
HEADS = 8
MLA_NOPE = 64
MLA_ROPE = 32
MLA_V = 64
Q_LORA = 256
KV_LORA = 128
NA_DIM = 64
NA_KR = 8
NA_KC = 16
ROPE_THETA = 10000.0
EPS = 1e-6

LANES = 128
HEAD_PAIRS = HEADS // 2
VMEM_LIMIT = 56 * 1024 * 1024

_C_CKV = (0, 128)
_C_KB = (160, 672)
_C_CQ = (1184, 1440)
_G_ZA = (1440, 1952)
_G_ZB = (2464, 2976)
_G_GA = (2976, 4000)
_G_GB = (4000, 5024)
_R_VB = (672, 1184)
_R_QB = (1952, 2464)
_C_KR = (0, 128)
_C_KRR = (128, 256)

NA_ROWS_PER_BLOCK = 4
NA_BLOCKS = GRID_H // NA_ROWS_PER_BLOCK
NA_TQ = NA_ROWS_PER_BLOCK * GRID_W
NA_SLAB_ROWS = 12
NA_SLAB = NA_SLAB_ROWS * GRID_W
NA_MASK = -1e30
LOG2_E = 1.4426950408889634
NA_QSCALE = NA_DIM ** -0.5 * LOG2_E

F32 = jnp.float32
BF16 = jnp.bfloat16


def _dot(a, b):
    return jnp.dot(a, b, preferred_element_type=F32)


def _dot_nt(a, b):
    return lax.dot_general(a, b, (((1,), (1,)), ((), ())), preferred_element_type=F32)


def _rms(x, g):
    return x * lax.rsqrt(jnp.mean(x * x, axis=-1, keepdims=True) + EPS) * g


def _mod_kernel(c_ref, w_ref, b_ref, o_ref):
    c = c_ref[...]
    sc = c * jax.nn.sigmoid(c)
    o_ref[0, :, 0, :] = _dot(sc.astype(BF16), w_ref[...].astype(BF16)) + b_ref[...]


def _mod_call(cs, w_mod, b_mod):
    n = w_mod.shape[1]
    tn = 1024
    return pl.pallas_call(
        _mod_kernel,
        grid=(n // tn,),
        in_specs=[
            pl.BlockSpec((8, D_MODEL), lambda j: (0, 0)),
            pl.BlockSpec((D_MODEL, tn), lambda j: (0, j)),
            pl.BlockSpec((1, tn), lambda j: (0, j)),
        ],
        out_specs=pl.BlockSpec((1, 8, 1, tn), lambda j: (j, 0, 0, 0)),
        out_shape=jax.ShapeDtypeStruct((n // tn, 8, 1, tn), F32),
        name="adaln_mod",
    )(cs, w_mod, b_mod)


def _proj_kernel(with_q, x_ref, shift_ref, scale_ref, ng_ref, w_ref, wkr_ref, gkv_ref, wk_ref, wv_ref,
                 ck_ref, sk_ref, *rest):
    if with_q:
        (gq_ref, wq_ref, wqr_ref, cq_ref, sq_ref,
         k_ref, v_ref, kb_ref, vb_ref, q_ref, qb_ref) = rest
    else:
        k_ref, v_ref, kb_ref, vb_ref = rest

    x = x_ref[0]
    h = _rms(x, ng_ref[...]) * (1.0 + scale_ref[0, 0]) + shift_ref[0, 0]
    hb = h.astype(BF16)

    def proj(c):
        return _dot_nt(hb, w_ref[c[0]:c[1], :])

    def proj_t(r):
        return _dot_nt(w_ref[r[0]:r[1], :], hb)

    def proj_kr(c):
        return _dot_nt(hb, wkr_ref[c[0]:c[1], :])

    ckvn = _rms(proj(_C_CKV), gkv_ref[...]).astype(BF16)
    if with_q:
        cqn = _rms(proj(_C_CQ), gq_ref[...]).astype(BF16)
    krope = proj_kr(_C_KR) * ck_ref[...] + proj_kr(_C_KRR) * sk_ref[...]
    kb_ref[0] = proj(_C_KB).astype(BF16)
    vb_ref[0] = proj_t(_R_VB).astype(BF16)
    if with_q:
        qb_ref[0] = (proj_t(_R_QB) * NA_QSCALE).astype(BF16)

    knope = _dot(ckvn, wk_ref[...])
    for hd in range(HEADS):
        sl = slice(hd * LANES, (hd + 1) * LANES)
        k_ref[0, :, sl] = (knope[:, sl] + krope).astype(BF16)
    v_ref[0] = _dot_nt(wv_ref[...], ckvn).astype(BF16)
    if with_q:
        q1 = _dot_nt(wq_ref[...], cqn)
        q2 = _dot_nt(wqr_ref[...], cqn)
        cq = cq_ref[...]
        sq = sq_ref[...]
        for hd in range(HEADS):
            sl = slice(hd * LANES, (hd + 1) * LANES)
            q_ref[0, sl, :] = (q1[sl] * cq + q2[sl] * sq).astype(BF16)


def _mod_spec(which, row=None):
    d = D_MODEL
    if row is None:
        return pl.BlockSpec((1, 1, 1, d), lambda i, j: (which, i, 0, 0))
    return pl.BlockSpec((1, 1, 1, d), lambda i, j: (which, row, 0, 0))


MOD_SHIFT, MOD_SCALE, MOD_GATE = 0, 1, 2


def _proj_call(x, mod, mod_row, norm_g, w_all, w_kr, g_ckv, wk, wv, ck, sk,
               q_side=None, tm=1024):
    b, s, d = x.shape
    tm = min(tm, s)
    with_q = q_side is not None
    n_all = w_all.shape[0]
    n_kr = w_kr.shape[0]
    const2 = lambda i, j: (0, 0)
    tok3 = lambda i, j: (i, j, 0)
    tab = lambda i, j: (j, 0)

    in_specs = [
        pl.BlockSpec((1, tm, d), tok3),
        _mod_spec(MOD_SHIFT, mod_row),
        _mod_spec(MOD_SCALE, mod_row),
        pl.BlockSpec((1, d), const2),
        pl.BlockSpec((n_all, d), const2, pipeline_mode=pl.Buffered(1)),
        pl.BlockSpec((n_kr, d), const2),
        pl.BlockSpec((1, KV_LORA), const2),
        pl.BlockSpec((KV_LORA, HEADS * LANES), const2),
        pl.BlockSpec((HEADS * MLA_V, KV_LORA), const2),
        pl.BlockSpec((tm, LANES), tab),
        pl.BlockSpec((tm, LANES), tab),
    ]
    args = [x, mod, mod, norm_g, w_all, w_kr, g_ckv, wk, wv, ck, sk]
    tok_major = lambda w: (pl.BlockSpec((1, tm, w), tok3), jax.ShapeDtypeStruct((b, s, w), BF16))
    feat_major = lambda w: (pl.BlockSpec((1, w, tm), lambda i, j: (i, 0, j)),
                            jax.ShapeDtypeStruct((b, w, s), BF16))
    outs = [tok_major(HEADS * LANES), feat_major(HEADS * MLA_V), tok_major(HEADS * NA_DIM),
            feat_major(HEADS * NA_DIM)]
    if with_q:
        g_cq, wq, wqr, cq, sq = q_side
        in_specs += [
            pl.BlockSpec((1, Q_LORA), const2),
            pl.BlockSpec((HEADS * LANES, Q_LORA), const2),
            pl.BlockSpec((HEADS * LANES, Q_LORA), const2),
            pl.BlockSpec((LANES, tm), lambda i, j: (0, j)),
            pl.BlockSpec((LANES, tm), lambda i, j: (0, j)),
        ]
        args += [g_cq, wq, wqr, cq, sq]
        outs += [feat_major(HEADS * LANES), feat_major(HEADS * NA_DIM)]
    out_specs = [o[0] for o in outs]
    out_shape = [o[1] for o in outs]
    return pl.pallas_call(
        functools.partial(_proj_kernel, with_q),
        grid=(b, s // tm),
        in_specs=in_specs,
        out_specs=out_specs,
        out_shape=out_shape,
        compiler_params=pltpu.CompilerParams(vmem_limit_bytes=VMEM_LIMIT),
        name="in_proj_q" if with_q else "in_proj_ctx",
    )(*args)


BF16_SUBLANES = 16
MLA_TQ = 256
MLA_TILES = 8
MLA_CHUNK = 256
MLA_QK_CHUNKS = 2
MLA_PV_DELAY = 1
MLA_LOOKAHEAD = 6


def _mla_kernel(qt_ref, k_ref, kc_ref, vt_ref, vct_ref, o_ref):
    s_len = k_ref.shape[1]
    streams = [(t, hh) for t in range(MLA_TILES) for hh in range(2)]
    qts = {(t, hh): qt_ref[0, hh * LANES:(hh + 1) * LANES, t * MLA_TQ:(t + 1) * MLA_TQ] for t, hh in streams}
    qk_keys = MLA_QK_CHUNKS * MLA_CHUNK
    groups = [(st, k_ref, vt_ref, c * qk_keys, qk_keys) for c in range(s_len // qk_keys) for st in streams]
    groups += [(st, kc_ref, vct_ref, 0, kc_ref.shape[1]) for st in streams]
    items = [(st, vr, off + i, min(MLA_CHUNK, n - i))
             for st, _, vr, off, n in groups for i in range(0, n, MLA_CHUNK)]

    def group_scores(g):
        st, kr, _, off, n = g
        hh = st[1]
        big = _dot(kr[0, off:off + n, hh * LANES:(hh + 1) * LANES], qts[st]).astype(BF16)
        return [big[i:i + MLA_CHUNK] for i in range(0, n, MLA_CHUNK)]

    m = {st: None for st in streams}
    acc = {st: None for st in streams}
    pending = []
    gi = 0
    last_item = {st[0]: idx for idx, (st, _, _, _) in enumerate(items)}

    def weigh(idx, st, vr, off, n, p, alpha):
        hh = st[1]
        v_ones = jnp.concatenate([vr[0, hh * MLA_V:(hh + 1) * MLA_V, off:off + n],
                                  jnp.ones((BF16_SUBLANES, n), BF16)], axis=0)
        pv = _dot(v_ones, p)
        acc[st] = pv if alpha is None else acc[st] * alpha + pv
        t = st[0]
        if idx == last_item[t]:
            outs = [acc[(t, h2)][:MLA_V] / acc[(t, h2)][MLA_V:MLA_V + 1] for h2 in range(2)]
            o_ref[0, t * MLA_TQ:(t + 1) * MLA_TQ, :] = jnp.concatenate(outs, axis=0).T.astype(BF16)

    delayed = []
    for idx, (st, vr, off, n) in enumerate(items):
        while len(pending) <= MLA_LOOKAHEAD and gi < len(groups):
            pending.extend(group_scores(groups[gi]))
            gi += 1
        s = pending.pop(0)
        mc = jnp.max(s, axis=0, keepdims=True)
        m_new = mc if m[st] is None else jnp.maximum(m[st], mc)
        p = jnp.exp2(s - m_new)
        alpha = None if m[st] is None else jnp.exp2(m[st].astype(F32) - m_new.astype(F32))
        m[st] = m_new
        if len(delayed) == MLA_PV_DELAY:
            weigh(*delayed.pop(0))
        delayed.append((idx, st, vr, off, n, p, alpha))
    for d in delayed:
        weigh(*d)


def _mla_call(qt, k, kc, vt, vct):
    b, _, s = qt.shape
    t = kc.shape[1]
    step_q = MLA_TILES * MLA_TQ
    return pl.pallas_call(
        _mla_kernel,
        grid=(b, HEAD_PAIRS, s // step_q),
        in_specs=[
            pl.BlockSpec((1, 2 * LANES, step_q), lambda i, p, j: (i, p, j)),
            pl.BlockSpec((1, s, 2 * LANES), lambda i, p, j: (i, 0, p)),
            pl.BlockSpec((1, t, 2 * LANES), lambda i, p, j: (i, 0, p)),
            pl.BlockSpec((1, 2 * MLA_V, s), lambda i, p, j: (i, p, 0)),
            pl.BlockSpec((1, 2 * MLA_V, t), lambda i, p, j: (i, p, 0)),
        ],
        out_specs=pl.BlockSpec((1, step_q, LANES), lambda i, p, j: (i, j, p)),
        out_shape=jax.ShapeDtypeStruct((b, s, HEADS * MLA_V), BF16),
        compiler_params=pltpu.CompilerParams(vmem_limit_bytes=VMEM_LIMIT),
        name="mla_attention",
    )(qt, k, kc, vt, vct)


NA_HEADS_PER_STEP = 8
NA_BLOCKS_PER_STEP = 8
NA_CHUNK = 256
NA_LOOKAHEAD = 13


def _na_slab_start(blk):
    return jnp.clip(blk * NA_ROWS_PER_BLOCK - NA_KR // 2, 0, GRID_H - NA_SLAB_ROWS)


def _na_variant(blk):
    return jnp.where(blk == 0, 0, jnp.where(blk == NA_BLOCKS - 1, 2, 1))


def _with_ones(v):
    return jnp.concatenate([v, jnp.ones((BF16_SUBLANES, v.shape[1]), v.dtype)], axis=0)


def _na_kernel(qt_ref, k_ref, vt_ref, kc_ref, vct_ref, *rest):
    bias_refs, o_ref = rest[:-1], rest[-1]
    row_half = lax.broadcasted_iota(jnp.int32, (LANES, NA_TQ), 0) // NA_DIM
    starts = [pl.multiple_of(_na_slab_start(pl.program_id(2) * NA_BLOCKS_PER_STEP + n) * GRID_W, NA_TQ)
              for n in range(NA_BLOCKS_PER_STEP)]
    heads = [(n, h) for n in range(NA_BLOCKS_PER_STEP) for h in range(NA_HEADS_PER_STEP)]
    n_keys = NA_SLAB + kc_ref.shape[1]
    items = [(n, h, c) for n, h in heads for c in range(0, n_keys, NA_CHUNK)]

    def scores(head):
        n, h = head
        pair, hh = divmod(h, 2)
        cols = slice(pair * LANES, (pair + 1) * LANES)
        k_all = jnp.concatenate([k_ref[0, pl.ds(starts[n], NA_SLAB), cols], kc_ref[0, :, cols]], axis=0)
        qt = qt_ref[0, cols, n * NA_TQ:(n + 1) * NA_TQ]
        qm = jnp.where(row_half == hh, qt, jnp.zeros_like(qt))
        s = _dot(k_all, qm).astype(BF16)
        bias_ref = bias_refs[0] if n == 0 else bias_refs[2] if n == NA_BLOCKS_PER_STEP - 1 else bias_refs[1]
        s = jnp.concatenate([s[:NA_SLAB] + bias_ref[0, h], s[NA_SLAB:]], axis=0)
        return [s[c:c + NA_CHUNK] for c in range(0, n_keys, NA_CHUNK)]

    m, acc, outs = {}, {}, []
    pending = []
    gi = 0
    for n, h, c in items:
        while len(pending) <= NA_LOOKAHEAD and gi < len(heads):
            pending.extend(scores(heads[gi]))
            gi += 1
        s = pending.pop(0)
        mc = jnp.max(s, axis=0, keepdims=True)
        m_new = mc if c == 0 else jnp.maximum(m[n, h], mc)
        p = jnp.exp2(s - m_new)
        rows = slice(h * NA_DIM, (h + 1) * NA_DIM)
        v_all = jnp.concatenate([vt_ref[0, rows, pl.ds(starts[n], NA_SLAB)], vct_ref[0, rows, :]], axis=1)
        pv = _dot(_with_ones(v_all[:, c:c + NA_CHUNK]), p)
        if c == 0:
            acc[n, h] = pv
        else:
            acc[n, h] = acc[n, h] * jnp.exp2(m[n, h].astype(F32) - m_new.astype(F32)) + pv
        m[n, h] = m_new
        if c + NA_CHUNK >= n_keys:
            a = acc.pop((n, h))
            outs.append(a[:NA_DIM] / a[NA_DIM:NA_DIM + 1])
            if h == NA_HEADS_PER_STEP - 1:
                o_ref[0, n * NA_TQ:(n + 1) * NA_TQ, :] = jnp.concatenate(outs, axis=0).T.astype(BF16)
                outs = []


def _na_call(qbt, kb, vbt, kcb, vcbt, bias):
    b, s, _ = kb.shape
    t = kcb.shape[1]
    wide = NA_HEADS_PER_STEP * NA_DIM
    step_q = NA_BLOCKS_PER_STEP * NA_TQ

    def bias_spec(n):
        return pl.BlockSpec((1, NA_HEADS_PER_STEP, NA_SLAB, NA_TQ),
                            lambda g, i, j: (_na_variant(j * NA_BLOCKS_PER_STEP + n), g, 0, 0))

    return pl.pallas_call(
        _na_kernel,
        grid=(HEADS // NA_HEADS_PER_STEP, b, NA_BLOCKS // NA_BLOCKS_PER_STEP),
        in_specs=[
            pl.BlockSpec((1, wide, step_q), lambda g, i, j: (i, g, j)),
            pl.BlockSpec((1, s, wide), lambda g, i, j: (i, 0, g)),
            pl.BlockSpec((1, wide, s), lambda g, i, j: (i, g, 0)),
            pl.BlockSpec((1, t, wide), lambda g, i, j: (i, 0, g)),
            pl.BlockSpec((1, wide, t), lambda g, i, j: (i, g, 0)),
        ] + [bias_spec(0), bias_spec(1), bias_spec(NA_BLOCKS_PER_STEP - 1)],
        out_specs=pl.BlockSpec((1, step_q, wide), lambda g, i, j: (i, j, g)),
        out_shape=jax.ShapeDtypeStruct((b, s, HEADS * NA_DIM), BF16),
        compiler_params=pltpu.CompilerParams(vmem_limit_bytes=VMEM_LIMIT),
        name="na_attention",
    )(qbt, kb, vbt, kcb, vcbt, bias, bias, bias)


def _na_bias_kernel(rpb_ref, o_ref):
    rp = pltpu.roll(rpb_ref[0] * LOG2_E, LANES - (NA_KC - 1), axis=1)
    shape = (GRID_W, LANES)
    wk = lax.broadcasted_iota(jnp.int32, shape, 0)
    lane = lax.broadcasted_iota(jnp.int32, shape, 1)
    c_start = jnp.clip(lane - NA_KC // 2, 0, GRID_W - NA_KC)
    left = lane < GRID_W
    mask_l = jnp.where(left, NA_MASK, 0.0).astype(F32)
    mask_r = jnp.where(left, 0.0, NA_MASK).astype(F32)
    tiles_l, tiles_r = [], []
    for d in range(2 * NA_KR - 1):
        row = jnp.broadcast_to(rp[d:d + 1, :], shape)
        t = pltpu.roll(row, 0, axis=1, stride=1, stride_axis=0)
        t = jnp.where(wk >= c_start, jnp.where(wk < c_start + NA_KC, t, NA_MASK), NA_MASK)
        t = jnp.where(left, t, 0.0)
        tiles_l.append(t)
        tiles_r.append(pltpu.roll(t, GRID_W, axis=1))
    for n, blk in enumerate((0, 1, NA_BLOCKS - 1)):
        sb = min(max(blk * NA_ROWS_PER_BLOCK - NA_KR // 2, 0), GRID_H - NA_SLAB_ROWS)
        for j in range(NA_SLAB_ROWS):
            kr = sb + j

            def pick(i, tiles, masked):
                r = blk * NA_ROWS_PER_BLOCK + i
                r_start = min(max(r - NA_KR // 2, 0), GRID_H - NA_KR)
                if r_start <= kr < r_start + NA_KR:
                    return tiles[kr - r + NA_KR - 1]
                return masked

            for ii in range(NA_ROWS_PER_BLOCK // 2):
                o_ref[n, 0, j * GRID_W:(j + 1) * GRID_W, ii * LANES:(ii + 1) * LANES] = (
                    pick(2 * ii, tiles_l, mask_l) + pick(2 * ii + 1, tiles_r, mask_r)).astype(BF16)


def _na_bias_tables(rpb):
    rpb_p = jnp.pad(rpb[:, :, ::-1], ((0, 0), (0, 1), (0, LANES - (2 * NA_KC - 1))))
    return pl.pallas_call(
        _na_bias_kernel,
        grid=(HEADS,),
        in_specs=[pl.BlockSpec((1, 2 * NA_KR, LANES), lambda h: (h, 0, 0))],
        out_specs=pl.BlockSpec((3, 1, NA_SLAB, NA_TQ), lambda h: (0, h, 0, 0)),
        out_shape=jax.ShapeDtypeStruct((3, HEADS, NA_SLAB, NA_TQ), BF16),
        name="na_bias_table",
    )(rpb_p)


def _out_kernel(oa_ref, ob_ref, x_ref, shift_ref, scale_ref, gate_ref, ng_ref, wg_ref,
                woa_ref, wob_ref, wout_ref, fg_ref, o_ref):
    subs = [slice(i, i + OUT_SUB) for i in range(0, o_ref.shape[1], OUT_SUB)]

    def gates(sl):
        h = _rms(x_ref[0, sl], ng_ref[...]) * (1.0 + scale_ref[0, 0]) + shift_ref[0, 0]
        hb = h.astype(BF16)
        proj = lambda c: _dot_nt(hb, wg_ref[c[0]:c[1], :])
        za = proj(_G_ZA)
        zb = proj(_G_ZB)
        return ((za * jax.nn.sigmoid(za)).astype(BF16), (zb * jax.nn.sigmoid(zb)).astype(BF16),
                jax.nn.sigmoid(proj(_G_GA)).astype(BF16), jax.nn.sigmoid(proj(_G_GB)).astype(BF16))

    def branches(sl, g):
        sza, szb, sga, sgb = g
        ya = _dot(oa_ref[0, sl] * sza, woa_ref[...])
        yb = _dot(ob_ref[0, sl] * szb, wob_ref[...])
        return (sga.astype(F32) * ya + sgb.astype(F32) * yb).astype(BF16)

    def finish(sl, mix):
        y = _dot(mix, wout_ref[...])
        r = x_ref[0, sl] + gate_ref[0, 0] * y
        o_ref[0, sl] = _rms(r, fg_ref[...])

    order = sorted([(3 * i + off, stage, i) for i in range(len(subs))
                    for stage, off in (("gates", 0), ("branches", 4), ("finish", 8))])
    g, mix = {}, {}
    for _, stage, i in order:
        if stage == "gates":
            g[i] = gates(subs[i])
        elif stage == "branches":
            mix[i] = branches(subs[i], g.pop(i))
        else:
            finish(subs[i], mix.pop(i))


OUT_SUB = 256


def _out_call(oa, ob, x, mod, norm_g, w_gate, w_oa, w_ob, w_out, final_g, tm=1024):
    b, s, d = x.shape
    tok3 = lambda i, j: (i, j, 0)
    half = HEADS * MLA_V

    def const(shape):
        return pl.BlockSpec(shape, lambda i, j: (0, 0), pipeline_mode=pl.Buffered(1))

    return pl.pallas_call(
        _out_kernel,
        grid=(b, s // tm),
        in_specs=[
            pl.BlockSpec((1, tm, half), tok3),
            pl.BlockSpec((1, tm, half), tok3),
            pl.BlockSpec((1, tm, d), tok3),
            _mod_spec(MOD_SHIFT),
            _mod_spec(MOD_SCALE),
            _mod_spec(MOD_GATE),
            const((1, d)),
            const((w_gate.shape[0], d)),
            const((half, d)),
            const((half, d)),
            const((d, d)),
            const((1, d)),
        ],
        out_specs=pl.BlockSpec((1, tm, d), tok3),
        out_shape=jax.ShapeDtypeStruct((b, s, d), F32),
        compiler_params=pltpu.CompilerParams(vmem_limit_bytes=VMEM_LIMIT),
        name="gated_out",
    )(oa, ob, x, mod, mod, mod, norm_g, w_gate, w_oa, w_ob, w_out, final_g)


def _rope_partner(w, axis):
    q = MLA_ROPE // 4
    parts = [lax.slice_in_dim(w, n * q, (n + 1) * q, axis=axis) for n in range(4)]
    return jnp.concatenate([-parts[1], parts[0], -parts[3], parts[2]], axis=axis)


def _rope_tables():
    t = np.arange(SEQ)
    half = MLA_ROPE // 4
    freqs = (ROPE_THETA ** (-np.arange(half, dtype=np.float32) / half)).astype(np.float32)
    ang_r = (t // GRID_W).astype(np.float32)[:, None] * freqs[None, :]
    ang_c = (t % GRID_W).astype(np.float32)[:, None] * freqs[None, :]
    cos = np.concatenate([np.cos(ang_r)] * 2 + [np.cos(ang_c)] * 2, axis=-1).astype(np.float32)
    sin = np.concatenate([np.sin(ang_r)] * 2 + [np.sin(ang_c)] * 2, axis=-1).astype(np.float32)
    return cos, sin


def _np_lane_group(nope, rope):
    pad = np.zeros((rope.shape[0], LANES - MLA_NOPE - MLA_ROPE), np.float32)
    return np.concatenate([nope, rope, pad], axis=-1).astype(np.float32)


def _rope_rows(r):
    d = r.shape[1]
    return jnp.concatenate([jnp.zeros((MLA_NOPE, d), F32), r,
                            jnp.zeros((LANES - MLA_NOPE - MLA_ROPE, d), F32)], axis=0)


def kernel(x, c, ctx, c_ctx, w_mod, b_mod, norm_g, w_in, g_cq, w_uq, g_ckv, w_ukv, rpb,
           w_oa, w_ob, w_out, final_g):
    b, s, d = x.shape
    depth = w_mod.shape[0]
    assert depth == 1 and s == SEQ and d == D_MODEL
    mla_scale = (MLA_NOPE + MLA_ROPE) ** -0.5 * LOG2_E

    wt = jnp.transpose(w_in[0])
    w_all = wt.astype(BF16)
    w_kr = wt[KV_LORA:KV_LORA + MLA_ROPE]
    w_kr2 = jnp.concatenate([_rope_rows(w_kr), _rope_rows(_rope_partner(w_kr, 0))], axis=0).astype(BF16)

    wkv3 = w_ukv[0].reshape(KV_LORA, HEADS, MLA_NOPE + MLA_V)
    wk = jnp.concatenate([wkv3[:, :, :MLA_NOPE], jnp.zeros((KV_LORA, HEADS, LANES - MLA_NOPE), F32)],
                         axis=-1).reshape(KV_LORA, HEADS * LANES).astype(BF16)
    wv = wkv3[:, :, MLA_NOPE:].reshape(KV_LORA, HEADS * MLA_V).astype(BF16).T
    wq3 = w_uq[0].reshape(Q_LORA, HEADS, MLA_NOPE + MLA_ROPE)
    zq = jnp.zeros((Q_LORA, HEADS, LANES - MLA_NOPE - MLA_ROPE), F32)
    wq = jnp.concatenate([wq3, zq], axis=-1).reshape(Q_LORA, HEADS * LANES).astype(BF16).T
    wqr = jnp.concatenate([jnp.zeros((Q_LORA, HEADS, MLA_NOPE), F32),
                           _rope_partner(wq3[:, :, MLA_NOPE:], 2), zq],
                          axis=-1).reshape(Q_LORA, HEADS * LANES).astype(BF16).T

    cos, sin = _rope_tables()
    ones_s = np.ones((s, MLA_NOPE), np.float32)
    zeros_s = np.zeros((s, MLA_NOPE), np.float32)
    cq_tab = jnp.asarray(np.ascontiguousarray((_np_lane_group(ones_s, cos) * np.float32(mla_scale)).T))
    sq_tab = jnp.asarray(np.ascontiguousarray((_np_lane_group(zeros_s, sin) * np.float32(mla_scale)).T))
    ck_tab = jnp.asarray(_np_lane_group(zeros_s, cos))
    sk_tab = jnp.asarray(_np_lane_group(zeros_s, sin))
    zeros_c = np.zeros((CTX_LEN, MLA_NOPE), np.float32)
    ck_ctx = jnp.asarray(_np_lane_group(zeros_c, np.ones((CTX_LEN, MLA_ROPE), np.float32)))
    sk_ctx = jnp.zeros((CTX_LEN, LANES), F32)

    cs = jnp.concatenate([c, c_ctx[None, :], jnp.zeros((8 - b - 1, d), F32)], axis=0)
    mod = _mod_call(cs, w_mod[0], b_mod[0][None, :])

    ng = norm_g[0][None, :]
    gkv = g_ckv[0][None, :]
    k, v, kb, vb, q, qb = _proj_call(
        x, mod, None, ng, w_all, w_kr2, gkv, wk, wv, ck_tab, sk_tab,
        q_side=(g_cq[0][None, :], wq, wqr, cq_tab, sq_tab))
    kc, vc, kcb, vcb = _proj_call(
        ctx, mod, b, ng, w_all, w_kr2, gkv, wk, wv,
        ck_ctx, sk_ctx)

    oa = _mla_call(q, k, kc, v, vc)
    ob = _na_call(qb, kb, vb, kcb, vcb, _na_bias_tables(rpb[0]))

    return _out_call(oa, ob, x, mod, ng, w_all,
                     w_oa[0].astype(BF16), w_ob[0].astype(BF16), w_out[0].astype(BF16),
                     final_g[None, :])
```
